```python
import jax
import jax.numpy as jnp
from jax import lax
import numpy as np

D_MODEL = 4096
BATCH = 4
SEQ = 2048
DEPTH = 1
DEC_BATCH = 16
DEC_SEQ = 64
PAST_LEN = 2048

CHUNK = 64
N_META = 16
WINDOW = 128
WIN_CHUNKS = WINDOW // CHUNK
HEAD_DIM = 128
N_HEADS_SWA = (D_MODEL // 2) // HEAD_DIM
N_KV_SWA = N_HEADS_SWA // 4
GQA_GROUP = N_HEADS_SWA // N_KV_SWA
N_HEADS_GDN = (D_MODEL // 2) // HEAD_DIM
DK_GDN = 128
DV_GDN = 128
CONV_WIDTH = 4
D_SWA_Q = N_HEADS_SWA * HEAD_DIM
D_SWA_KV = N_KV_SWA * HEAD_DIM
D_GDN_QK = N_HEADS_GDN * DK_GDN
D_GDN_V = N_HEADS_GDN * DV_GDN
D_CONV = 2 * D_GDN_QK + D_GDN_V
D_MIX = D_SWA_Q + D_GDN_V
D_IN = D_SWA_Q + 2 * D_SWA_KV + D_CONV + D_GDN_V + 2 * N_HEADS_GDN
D_FF = ((8 * D_MODEL // 3 + 255) // 256) * 256
ROPE_THETA = 10000.0
LN_EPS = 1e-5
RMS_EPS = 1e-6
ALPHA = (2 * DEPTH) ** 0.25
BETA = (8 * DEPTH) ** -0.25

kernel_name = 'hymba_swa_gdn_macaron_stream_step'


def _layernorm(x, g, b):
    xf = x.astype(jnp.float32)
    mu = xf.mean(-1, keepdims=True)
    var = jnp.square(xf - mu).mean(-1, keepdims=True)
    return ((xf - mu) * lax.rsqrt(var + LN_EPS) * g.astype(jnp.float32) + b.astype(jnp.float32)).astype(x.dtype)


def _swiglu(x, w_gate, w_up, w_down):
    return (jax.nn.silu(x @ w_gate) * (x @ w_up)) @ w_down


def _split_in(h):
    sizes = [D_SWA_Q, D_SWA_KV, D_SWA_KV, D_CONV, D_GDN_V, N_HEADS_GDN, N_HEADS_GDN]
    offs = [int(o) for o in np.cumsum(sizes)[:-1]]
    return jnp.split(h, offs, axis=-1)


def _rope(x, pos):
    half = HEAD_DIM // 2
    inv = ROPE_THETA ** (-jnp.arange(half, dtype=jnp.float32) / half)
    ang = pos.astype(jnp.float32)[:, None] * inv[None, :]
    cos = jnp.cos(ang)[None, :, None, :]
    sin = jnp.sin(ang)[None, :, None, :]
    xf = x.astype(jnp.float32)
    x1, x2 = xf[..., :half], xf[..., half:]
    return jnp.concatenate([x1 * cos - x2 * sin, x2 * cos + x1 * sin], -1).astype(x.dtype)


def _attend(q, k, v, mask, sinks):
    s = jnp.einsum('bnqhgd,bnshd->bnhgqs', q, k, preferred_element_type=jnp.float32) * (HEAD_DIM ** -0.5)
    s = jnp.where(mask, s, -jnp.inf)
    sk = jnp.broadcast_to(sinks.astype(jnp.float32).reshape(N_KV_SWA, GQA_GROUP, 1, 1), s.shape[:-1] + (1,))
    p = jax.nn.softmax(jnp.concatenate([s, sk], -1), axis=-1)[..., :-1]
    return jnp.einsum('bnhgqs,bnshd->bnqhgd', p.astype(v.dtype), v)


def _swa_prompt(q, k, v, sinks):
    B, T = q.shape[:2]
    S = T - N_META
    NC = S // CHUNK
    qm, qf = q[:, :N_META], q[:, N_META:]
    km, kf = k[:, :N_META], k[:, N_META:]
    vm, vf = v[:, :N_META], v[:, N_META:]
    om = _attend(qm.reshape(B, 1, N_META, N_KV_SWA, GQA_GROUP, HEAD_DIM), km[:, None], vm[:, None], True, sinks)
    om = om.reshape(B, N_META, D_SWA_Q)

    def band(t):
        tp = jnp.pad(t, ((0, 0), (WIN_CHUNKS * CHUNK, 0), (0, 0), (0, 0)))
        tp = tp.reshape(B, NC + WIN_CHUNKS, CHUNK, N_KV_SWA, HEAD_DIM)
        return jnp.concatenate([tp[:, j:j + NC] for j in range(WIN_CHUNKS + 1)], axis=2)

    kb = jnp.concatenate([jnp.broadcast_to(km[:, None], (B, NC, N_META, N_KV_SWA, HEAD_DIM)), band(kf)], 2)
    vb = jnp.concatenate([jnp.broadcast_to(vm[:, None], (B, NC, N_META, N_KV_SWA, HEAD_DIM)), band(vf)], 2)
    nb = (WIN_CHUNKS + 1) * CHUNK
    valid = (jnp.arange(NC)[:, None] + jnp.arange(nb)[None, :] // CHUNK - WIN_CHUNKS) >= 0
    mask = jnp.concatenate([jnp.ones((NC, N_META), bool), valid], 1)[None, :, None, None, None, :]
    of = _attend(qf.reshape(B, NC, CHUNK, N_KV_SWA, GQA_GROUP, HEAD_DIM), kb, vb, mask, sinks)
    return jnp.concatenate([om, of.reshape(B, S, D_SWA_Q)], 1)


def _l2norm(x):
    return x * lax.rsqrt(jnp.sum(x * x, -1, keepdims=True) + RMS_EPS)


def _gdn_chunked(q, k, v, g, beta, s0, chunk):
    B, T, H, _ = q.shape
    N = T // chunk

    def blk(t):
        return jnp.moveaxis(t.reshape((B, N, chunk) + t.shape[2:]), 3, 1)

    qc, kc, vc, bc = blk(q), blk(k), blk(v), blk(beta)
    gc = jnp.cumsum(blk(g), axis=-1)
    idx = jnp.arange(chunk)
    incl = idx[:, None] >= idx[None, :]
    strict = idx[:, None] > idx[None, :]
    decay = jnp.exp(jnp.where(incl, gc[..., :, None] - gc[..., None, :], -jnp.inf))
    kb = kc * bc[..., None]
    a = jnp.where(strict, jnp.einsum('bhnid,bhnjd->bhnij', kb, kc) * decay, 0.0)
    eye = jnp.broadcast_to(jnp.eye(chunk, dtype=jnp.float32), a.shape)
    rhs = jnp.concatenate([vc * bc[..., None], kb * jnp.exp(gc)[..., None]], -1)
    sol = lax.linalg.triangular_solve(eye + a, rhs, left_side=True, lower=True, unit_diagonal=True)
    u, w = sol[..., :DV_GDN], sol[..., DV_GDN:]
    qk = jnp.einsum('bhnid,bhnjd->bhnij', qc, kc) * decay
    q_dec = qc * jnp.exp(gc)[..., None]
    k_dec = kc * jnp.exp(gc[..., -1:] - gc)[..., None]
    g_last = jnp.exp(gc[..., -1])
    xs = tuple(jnp.moveaxis(t, 2, 0) for t in (u, w, qk, q_dec, k_dec, g_last))

    def step(s, inp):
        u_n, w_n, qk_n, qd_n, kd_n, gl_n = inp
        v_new = u_n - jnp.einsum('bhcd,bhde->bhce', w_n, s)
        o_n = jnp.einsum('bhcd,bhde->bhce', qd_n, s) + jnp.einsum('bhij,bhje->bhie', qk_n, v_new)
        s = s * gl_n[..., None, None] + jnp.einsum('bhcd,bhce->bhde', kd_n, v_new)
        return s, o_n

    s_final, o = lax.scan(step, s0, xs)
    o = jnp.moveaxis(jnp.moveaxis(o, 0, 2), 1, 3).reshape(B, T, H, DV_GDN)
    return o, s_final


def _gdn_mix(conv_in, z, b_lin, a_lin, conv_buf, s0, conv_w, a_log, dt_bias, norm_w, n_pad, chunk):
    B, T, _ = conv_in.shape
    xcat = jnp.concatenate([conv_buf.astype(conv_in.dtype), conv_in], 1)
    y = sum(xcat[:, j:j + T] * conv_w[j] for j in range(CONV_WIDTH))
    y = jax.nn.silu(y.astype(jnp.float32))
    q, k, v = jnp.split(y, [D_GDN_QK, 2 * D_GDN_QK], axis=-1)
    q = _l2norm(q.reshape(B, T, N_HEADS_GDN, DK_GDN)) * (DK_GDN ** -0.5)
    k = _l2norm(k.reshape(B, T, N_HEADS_GDN, DK_GDN))
    v = v.reshape(B, T, N_HEADS_GDN, DV_GDN)
    beta = jax.nn.sigmoid(b_lin.astype(jnp.float32))
    g = -jnp.exp(a_log.astype(jnp.float32)) * jax.nn.softplus(a_lin.astype(jnp.float32) + dt_bias.astype(jnp.float32))
    if n_pad > 0:
        pad4 = ((0, 0), (n_pad, 0), (0, 0), (0, 0))
        pad3 = ((0, 0), (n_pad, 0), (0, 0))
        q, k, v = jnp.pad(q, pad4), jnp.pad(k, pad4), jnp.pad(v, pad4)
        g, beta = jnp.pad(g, pad3), jnp.pad(beta, pad3)
    o, s = _gdn_chunked(q, k, v, g, beta, s0, chunk)
    o = o[:, n_pad:]
    o = o * lax.rsqrt(jnp.mean(o * o, -1, keepdims=True) + RMS_EPS) * norm_w.astype(jnp.float32)
    o = o * jax.nn.silu(z.astype(jnp.float32).reshape(B, T, N_HEADS_GDN, DV_GDN))
    return o.reshape(B, T, D_GDN_V).astype(conv_in.dtype), xcat[:, -(CONV_WIDTH - 1):], s


def _mix_prompt(h, pos, sinks, conv_w, a_log, dt_bias, norm_w):
    B, T, _ = h.shape
    q, k, v, conv_in, z, b_lin, a_lin = _split_in(h)
    q = _rope(q.reshape(B, T, N_HEADS_SWA, HEAD_DIM), pos)
    k = _rope(k.reshape(B, T, N_KV_SWA, HEAD_DIM), pos)
    v = v.reshape(B, T, N_KV_SWA, HEAD_DIM)
    o_swa = _swa_prompt(q, k, v, sinks)
    buf0 = jnp.zeros((B, CONV_WIDTH - 1, D_CONV), h.dtype)
    s0 = jnp.zeros((B, N_HEADS_GDN, DK_GDN, DV_GDN), jnp.float32)
    o_gdn, buf, s = _gdn_mix(conv_in, z, b_lin, a_lin, buf0, s0, conv_w, a_log, dt_bias, norm_w,
                             CHUNK - N_META, CHUNK)
    mixed = jnp.concatenate([o_swa, o_gdn], -1)
    return mixed, (k[:, :N_META], v[:, :N_META], k[:, T - WINDOW:], v[:, T - WINDOW:], buf, s.astype(h.dtype))


def _mix_sample(h, pos, meta_k, meta_v, win_k, win_v, conv_buf, s_prev, sinks, conv_w, a_log, dt_bias, norm_w):
    B, T, _ = h.shape
    q, k, v, conv_in, z, b_lin, a_lin = _split_in(h)
    q = _rope(q.reshape(B, T, N_HEADS_SWA, HEAD_DIM), pos)
    k = _rope(k.reshape(B, T, N_KV_SWA, HEAD_DIM), pos)
    v = v.reshape(B, T, N_KV_SWA, HEAD_DIM)
    k_all = jnp.concatenate([meta_k.astype(k.dtype), win_k.astype(k.dtype), k], 1)[:, None]
    v_all = jnp.concatenate([meta_v.astype(v.dtype), win_v.astype(v.dtype), v], 1)[:, None]
    o_swa = _attend(q.reshape(B, 1, T, N_KV_SWA, GQA_GROUP, HEAD_DIM), k_all, v_all, True, sinks)
    o_swa = o_swa.reshape(B, T, D_SWA_Q)
    o_gdn, buf, s = _gdn_mix(conv_in, z, b_lin, a_lin, conv_buf, s_prev.astype(jnp.float32), conv_w, a_log,
                             dt_bias, norm_w, 0, T)
    mixed = jnp.concatenate([o_swa, o_gdn], -1)
    return mixed, (k, v, buf, s.astype(s_prev.dtype))


def _layer(x, ln_g, ln_b, w_gate, w_up, w_down, w_in, w_out, mixer):
    x = _layernorm(ALPHA * x + 0.5 * _swiglu(x, w_gate[0], w_up[0], w_down[0]), ln_g[0], ln_b[0])
    mixed, states = mixer(x @ w_in)
    x = _layernorm(ALPHA * x + mixed @ w_out, ln_g[1], ln_b[1])
    x = _layernorm(ALPHA * x + 0.5 * _swiglu(x, w_gate[1], w_up[1], w_down[1]), ln_g[2], ln_b[2])
    return x, states


def setup_inputs(seed: int = 0) -> dict:
    key = jax.random.key(seed)
    ks = jax.random.split(key, 24)
    f32 = jnp.float32

    def nrm(k, shape, scale):
        return jax.random.normal(k, shape, f32) * scale

    win = min(WINDOW, PAST_LEN)
    dt = jnp.exp(jax.random.uniform(ks[20], (DEPTH, N_HEADS_GDN), f32, float(np.log(1e-3)), float(np.log(1e-1))))
    return {
        'x_prompt': nrm(ks[0], (BATCH, SEQ, D_MODEL), 1.0),
        'x_sample': nrm(ks[1], (DEC_BATCH, DEC_SEQ, D_MODEL), 1.0),
        'cache_meta_k': nrm(ks[2], (DEPTH, DEC_BATCH, N_META, N_KV_SWA, HEAD_DIM), 1.0),
        'cache_meta_v': nrm(ks[3], (DEPTH, DEC_BATCH, N_META, N_KV_SWA, HEAD_DIM), 1.0),
        'cache_win_k': nrm(ks[4], (DEPTH, DEC_BATCH, win, N_KV_SWA, HEAD_DIM), 1.0),
        'cache_win_v': nrm(ks[5], (DEPTH, DEC_BATCH, win, N_KV_SWA, HEAD_DIM), 1.0),
        'state_conv': nrm(ks[6], (DEPTH, DEC_BATCH, CONV_WIDTH - 1, D_CONV), 1.0),
        'state_gdn': nrm(ks[7], (DEPTH, DEC_BATCH, N_HEADS_GDN, DK_GDN, DV_GDN), 0.5),
        'meta_tokens': nrm(ks[8], (N_META, D_MODEL), 1.0),
        'ln_g': 1.0 + nrm(ks[9], (DEPTH, 3, D_MODEL), 0.02),
        'ln_b': nrm(ks[10], (DEPTH, 3, D_MODEL), 0.02),
        'ffn_w_gate': nrm(ks[11], (DEPTH, 2, D_MODEL, D_FF), D_MODEL ** -0.5),
        'ffn_w_up': nrm(ks[12], (DEPTH, 2, D_MODEL, D_FF), D_MODEL ** -0.5),
        'ffn_w_down': nrm(ks[13], (DEPTH, 2, D_FF, D_MODEL), BETA * D_FF ** -0.5),
        'w_in': nrm(ks[14], (DEPTH, D_MODEL, D_IN), D_MODEL ** -0.5),
        'w_out': nrm(ks[15], (DEPTH, D_MIX, D_MODEL), BETA * D_MIX ** -0.5),
        'attn_sinks': nrm(ks[16], (DEPTH, N_HEADS_SWA), 0.5),
        'conv_w': nrm(ks[17], (DEPTH, CONV_WIDTH, D_CONV), CONV_WIDTH ** -0.5),
        'gdn_a_log': jnp.log(jax.random.uniform(ks[18], (DEPTH, N_HEADS_GDN), f32, 1.0, 16.0)),
        'gdn_dt_bias': dt + jnp.log(-jnp.expm1(-dt)),
        'gdn_norm_w': 1.0 + nrm(ks[19], (DEPTH, DV_GDN), 0.02),
    }


def reference(x_prompt, x_sample, cache_meta_k, cache_meta_v, cache_win_k, cache_win_v, state_conv, state_gdn,
              meta_tokens, ln_g, ln_b, ffn_w_gate, ffn_w_up, ffn_w_down, w_in, w_out, attn_sinks, conv_w,
              gdn_a_log, gdn_dt_bias, gdn_norm_w):
    B, S, _ = x_prompt.shape
    DS = x_sample.shape[1]
    pos_p = jnp.arange(N_META + S, dtype=jnp.int32)
    pos_s = N_META + PAST_LEN + jnp.arange(DS, dtype=jnp.int32)
    xp = jnp.concatenate([jnp.broadcast_to(meta_tokens.astype(x_prompt.dtype)[None], (B, N_META, D_MODEL)),
                          x_prompt], 1)
    xs = x_sample
    p_states = []
    s_states = []
    for l in range(DEPTH):
        mix_p = lambda h: _mix_prompt(h, pos_p, attn_sinks[l], conv_w[l], gdn_a_log[l], gdn_dt_bias[l],
                                      gdn_norm_w[l])
        mix_s = lambda h: _mix_sample(h, pos_s, cache_meta_k[l], cache_meta_v[l], cache_win_k[l], cache_win_v[l],
                                      state_conv[l], state_gdn[l], attn_sinks[l], conv_w[l], gdn_a_log[l],
                                      gdn_dt_bias[l], gdn_norm_w[l])
        xp, sp = _layer(xp, ln_g[l], ln_b[l], ffn_w_gate[l], ffn_w_up[l], ffn_w_down[l], w_in[l], w_out[l], mix_p)
        xs, ss = _layer(xs, ln_g[l], ln_b[l], ffn_w_gate[l], ffn_w_up[l], ffn_w_down[l], w_in[l], w_out[l], mix_s)
        p_states.append(sp)
        s_states.append(ss)
    y_prompt = xp[:, N_META:]
    y_sample = xs
    p_meta_k = jnp.stack([st[0] for st in p_states])
    p_meta_v = jnp.stack([st[1] for st in p_states])
    p_win_k = jnp.stack([st[2] for st in p_states])
    p_win_v = jnp.stack([st[3] for st in p_states])
    p_conv = jnp.stack([st[4] for st in p_states])
    p_gdn = jnp.stack([st[5] for st in p_states])
    s_win_k = jnp.stack([st[0] for st in s_states])
    s_win_v = jnp.stack([st[1] for st in s_states])
    s_conv = jnp.stack([st[2] for st in s_states])
    s_gdn = jnp.stack([st[3] for st in s_states])
    return (y_prompt, y_sample, p_meta_k, p_meta_v, p_win_k, p_win_v, p_conv, p_gdn, s_win_k, s_win_v, s_conv, s_gdn)
```

```python
import functools

import jax
import jax.numpy as jnp
from jax import lax
from jax.experimental import pallas as pl
from jax.experimental.pallas import tpu as pltpu

D_MODEL = 4096
BATCH = 4
SEQ = 2048
DEC_BATCH = 16
DEC_SEQ = 64
PAST_LEN = 2048
CHUNK = 64
N_META = 16
WINDOW = 128
HEAD_DIM = 128
N_HEADS_SWA = 16
N_KV_SWA = 4
GQA_GROUP = 4
N_HEADS_GDN = 16
DK_GDN = 128
DV_GDN = 128
CONV_WIDTH = 4
D_SWA_Q = N_HEADS_SWA * HEAD_DIM
D_SWA_KV = N_KV_SWA * HEAD_DIM
D_GDN = N_HEADS_GDN * DK_GDN
D_CONV = 3 * D_GDN
D_FF = 11008
ROPE_THETA = 10000.0
LN_EPS = 1e-5
RMS_EPS = 1e-6
ALPHA = 2.0 ** 0.25

N_PROMPT_ROWS = BATCH * SEQ
N_SAMPLE_ROWS = DEC_BATCH * DEC_SEQ
N_ROWS = N_PROMPT_ROWS + N_SAMPLE_ROWS
CHUNKS_PER_SEQ = SEQ // CHUNK

COL_CONV = 0
COL_Z = D_CONV
COL_Q = COL_Z + D_GDN
COL_K = COL_Q + D_SWA_Q
COL_V = COL_K + D_SWA_KV
D_MAIN = COL_V + D_SWA_KV
TAIL_W = 128

FFN_TF = 256
PROJ_TN = 512
OUT_TK = 512
GDN_HG = 4
GDN_GW = GDN_HG * DK_GDN

VMEM_LIMIT = 60 * 1024 * 1024

BF16 = jnp.bfloat16
F32 = jnp.float32
HI = lax.Precision.HIGHEST


def _dot(a, b, precision=None):
    return jnp.dot(a, b, preferred_element_type=F32, precision=precision)


def _dot_nt(a, b, precision=None):
    return lax.dot_general(a, b, (((1,), (1,)), ((), ())), preferred_element_type=F32, precision=precision)


def _dot_tn(a, b, precision=None):
    return lax.dot_general(a, b, (((0,), (0,)), ((), ())), preferred_element_type=F32, precision=precision)


def _layernorm_rows(y, g, b):
    mu = jnp.mean(y, axis=-1, keepdims=True)
    d = y - mu
    var = jnp.mean(d * d, axis=-1, keepdims=True)
    return d * lax.rsqrt(var + LN_EPS) * g + b


def _ffn_body(x_ref, wgu_ref, wd_ref, g_ref, b_ref, o_ref, *rest, emit_bf16):
    if emit_bf16:
        ob_ref, xb_ref = rest
    else:
        (xb_ref,) = rest
    f = pl.program_id(1)

    @pl.when(f == 0)
    def _():
        xb_ref[...] = x_ref[...].astype(BF16)
        o_ref[...] = jnp.zeros_like(o_ref)

    gu = _dot(xb_ref[...], wgu_ref[...])
    gate = gu[:, :FFN_TF]
    up = gu[:, FFN_TF:]
    hidden = gate * jax.nn.sigmoid(gate) * up
    o_ref[...] += _dot(hidden.astype(BF16), wd_ref[...])

    @pl.when(f == pl.num_programs(1) - 1)
    def _():
        y = _layernorm_rows(ALPHA * x_ref[...] + 0.5 * o_ref[...], g_ref[...], b_ref[...])
        o_ref[...] = y
        if emit_bf16:
            ob_ref[...] = y.astype(BF16)


def _ffn(x, wgu, wd, g, b, *, tm, emit_bf16):
    rows = x.shape[0]
    n_f = wgu.shape[0]
    out_shape = [jax.ShapeDtypeStruct((rows, D_MODEL), F32)]
    out_specs = [pl.BlockSpec((tm, D_MODEL), lambda i, f: (i, 0))]
    if emit_bf16:
        out_shape.append(jax.ShapeDtypeStruct((rows, D_MODEL), BF16))
        out_specs.append(pl.BlockSpec((tm, D_MODEL), lambda i, f: (i, 0)))
    return pl.pallas_call(
        functools.partial(_ffn_body, emit_bf16=emit_bf16),
        grid=(rows // tm, n_f),
        in_specs=[
            pl.BlockSpec((tm, D_MODEL), lambda i, f: (i, 0), pipeline_mode=pl.Buffered(1)),
            pl.BlockSpec((None, D_MODEL, 2 * FFN_TF), lambda i, f: (f, 0, 0)),
            pl.BlockSpec((FFN_TF, D_MODEL), lambda i, f: (f, 0)),
            pl.BlockSpec((1, D_MODEL), lambda i, f: (0, 0)),
            pl.BlockSpec((1, D_MODEL), lambda i, f: (0, 0)),
        ],
        out_specs=out_specs,
        out_shape=out_shape,
        scratch_shapes=[pltpu.VMEM((tm, D_MODEL), BF16)],
        compiler_params=pltpu.CompilerParams(
            dimension_semantics=("parallel", "arbitrary"), vmem_limit_bytes=VMEM_LIMIT),
        name="ffn",
    )(x, wgu, wd, g, b)


def _proj_body(x_ref, w_ref, wt_ref, cos_ref, sin_ref, o_ref, t_ref):
    j = pl.program_id(1)
    acc = _dot(x_ref[...], w_ref[...])
    is_rope = jnp.logical_and(j >= COL_Q // PROJ_TN, j < COL_V // PROJ_TN)

    @pl.when(is_rope)
    def _():
        cos = cos_ref[...]
        sin = sin_ref[...]
        for s in range(PROJ_TN // HEAD_DIM):
            blk = acc[:, s * HEAD_DIM:(s + 1) * HEAD_DIM]
            o_ref[:, s * HEAD_DIM:(s + 1) * HEAD_DIM] = blk * cos + pltpu.roll(blk, HEAD_DIM // 2, 1) * sin

    @pl.when(jnp.logical_not(is_rope))
    def _():
        o_ref[...] = acc

    @pl.when(j == 0)
    def _():
        t_ref[...] = _dot(x_ref[...], wt_ref[...])


def _proj(xb, w_tiles, w_tail, cos, sin, *, tm):
    rows = xb.shape[0]
    n_j = w_tiles.shape[0]
    return pl.pallas_call(
        _proj_body,
        grid=(rows // tm, n_j),
        in_specs=[
            pl.BlockSpec((tm, D_MODEL), lambda i, j: (i, 0)),
            pl.BlockSpec((None, D_MODEL, PROJ_TN), lambda i, j: (j, 0, 0)),
            pl.BlockSpec((D_MODEL, TAIL_W), lambda i, j: (0, 0)),
            pl.BlockSpec((tm, HEAD_DIM), lambda i, j: (i, 0)),
            pl.BlockSpec((tm, HEAD_DIM), lambda i, j: (i, 0)),
        ],
        out_specs=[
            pl.BlockSpec((tm, PROJ_TN), lambda i, j: (i, j)),
            pl.BlockSpec((tm, TAIL_W), lambda i, j: (i, 0)),
        ],
        out_shape=[
            jax.ShapeDtypeStruct((rows, D_MAIN), F32),
            jax.ShapeDtypeStruct((rows, TAIL_W), F32),
        ],
        compiler_params=pltpu.CompilerParams(
            dimension_semantics=("parallel", "arbitrary"), vmem_limit_bytes=VMEM_LIMIT),
        name="proj_in",
    )(xb, w_tiles, w_tail, cos, sin)


N_KEYS = N_META + 3 * CHUNK


def _attn_body(sink_ref, q_ref, km_ref, vm_ref, k2_ref, v2_ref, k1_ref, v1_ref, k0_ref, v0_ref, o_ref, *,
               prompt):
    col = lax.broadcasted_iota(jnp.int32, (1, N_KEYS), 1)
    if prompt:
        c = pl.program_id(0) % CHUNKS_PER_SEQ
        ok2 = jnp.logical_or(col >= N_META + CHUNK, c >= 2)
        ok1 = jnp.logical_or(jnp.logical_or(col < N_META + CHUNK, col >= N_META + 2 * CHUNK), c >= 1)
        mask = jnp.logical_or(col < N_META, jnp.logical_and(ok2, ok1))
    else:
        mask = col >= 0
    scale = HEAD_DIM ** -0.5
    for g in range(N_KV_SWA):
        ks = slice(g * HEAD_DIM, (g + 1) * HEAD_DIM)
        k = jnp.concatenate([km_ref[:, ks], k2_ref[:, ks], k1_ref[:, ks], k0_ref[:, ks]], axis=0).astype(BF16)
        v = jnp.concatenate([vm_ref[:, ks], v2_ref[:, ks], v1_ref[:, ks], v0_ref[:, ks]], axis=0).astype(BF16)
        heads = [g * GQA_GROUP + j for j in range(GQA_GROUP)]
        q = jnp.concatenate([q_ref[:, h * HEAD_DIM:(h + 1) * HEAD_DIM] for h in heads], axis=0).astype(BF16)
        sink = jnp.concatenate([jnp.full((CHUNK, 1), sink_ref[h], F32) for h in heads], axis=0)
        s = _dot_nt(q, k) * scale
        s = jnp.where(mask, s, -jnp.inf)
        m = jnp.maximum(jnp.max(s, axis=-1, keepdims=True), sink)
        p = jnp.exp(s - m)
        den = jnp.sum(p, axis=-1, keepdims=True) + jnp.exp(sink - m)
        o = _dot(p.astype(BF16), v) / den
        for j, h in enumerate(heads):
            o_ref[:, h * HEAD_DIM:(h + 1) * HEAD_DIM] = o[j * CHUNK:(j + 1) * CHUNK].astype(BF16)


def _attn_prompt(sinks, h_main, h_meta):
    kcol = COL_K // D_SWA_KV
    vcol = COL_V // D_SWA_KV
    qcol = COL_Q // D_SWA_Q

    def prev(n, colblk):
        return lambda p: (p - jnp.where(p % CHUNKS_PER_SEQ >= n, n, 0), colblk)

    kv = lambda imap: pl.BlockSpec((CHUNK, D_SWA_KV), imap)
    return pl.pallas_call(
        functools.partial(_attn_body, prompt=True),
        grid=(N_PROMPT_ROWS // CHUNK,),
        in_specs=[
            pl.BlockSpec(memory_space=pltpu.SMEM),
            pl.BlockSpec((CHUNK, D_SWA_Q), lambda p: (p, qcol)),
            pl.BlockSpec((N_META, D_SWA_KV), lambda p: (0, kcol)),
            pl.BlockSpec((N_META, D_SWA_KV), lambda p: (0, vcol)),
            kv(prev(2, kcol)), kv(prev(2, vcol)),
            kv(prev(1, kcol)), kv(prev(1, vcol)),
            kv(lambda p: (p, kcol)), kv(lambda p: (p, vcol)),
        ],
        out_specs=pl.BlockSpec((CHUNK, D_SWA_Q), lambda p: (p, 0)),
        out_shape=jax.ShapeDtypeStruct((N_PROMPT_ROWS, D_SWA_Q), BF16),
        compiler_params=pltpu.CompilerParams(dimension_semantics=("parallel",)),
        name="swa_prompt",
    )(sinks, h_main, h_meta, h_meta, h_main, h_main, h_main, h_main, h_main, h_main)


def _attn_sample(sinks, h_main, meta_k, meta_v, win_k, win_v):
    kcol = COL_K // D_SWA_KV
    vcol = COL_V // D_SWA_KV
    qcol = COL_Q // D_SWA_Q
    base = N_PROMPT_ROWS // CHUNK
    kv = lambda imap: pl.BlockSpec((CHUNK, D_SWA_KV), imap)
    return pl.pallas_call(
        functools.partial(_attn_body, prompt=False),
        grid=(DEC_BATCH,),
        in_specs=[
            pl.BlockSpec(memory_space=pltpu.SMEM),
            pl.BlockSpec((CHUNK, D_SWA_Q), lambda s: (base + s, qcol)),
            pl.BlockSpec((N_META, D_SWA_KV), lambda s: (s, 0)),
            pl.BlockSpec((N_META, D_SWA_KV), lambda s: (s, 0)),
            kv(lambda s: (2 * s, 0)), kv(lambda s: (2 * s, 0)),
            kv(lambda s: (2 * s + 1, 0)), kv(lambda s: (2 * s + 1, 0)),
            kv(lambda s: (base + s, kcol)), kv(lambda s: (base + s, vcol)),
        ],
        out_specs=pl.BlockSpec((CHUNK, D_SWA_Q), lambda s: (s, 0)),
        out_shape=jax.ShapeDtypeStruct((N_SAMPLE_ROWS, D_SWA_Q), BF16),
        compiler_params=pltpu.CompilerParams(dimension_semantics=("parallel",)),
        name="swa_sample",
    )(sinks, h_main, meta_k, meta_v, win_k, win_v, win_k, win_v, h_main, h_main)


def _unit_lower_inverse(a, n):
    r = lax.broadcasted_iota(jnp.int32, (n, n), 0)
    c = lax.broadcasted_iota(jnp.int32, (n, n), 1)
    eye = (r == c).astype(F32)
    a0 = jnp.where((r >> 3) == (c >> 3), a, 0.0)
    a2 = _dot(a0, a0, HI)
    a4 = _dot(a2, a2, HI)
    x = _dot(_dot(eye - a0, eye + a2, HI), eye + a4, HI)
    shift = 3
    while (1 << shift) < n:
        pair = jnp.logical_and((r >> (shift + 1)) == (c >> (shift + 1)), (r >> shift) != (c >> shift))
        ak = jnp.where(pair, a, 0.0)
        x = x - _dot(x, _dot(ak, x, HI), HI)
        shift += 1
    return x


def _gdn_body(xq_ref, xk_ref, xv_ref, z_ref, t_ref, wq_ref, wk_ref, wv_ref, alog_ref, dt_ref, nw_ref,
              s0_ref, bq_ref, bk_ref, bv_ref, o_ref, sout_ref, s_scr, carry_scr, xe_scr, *, C, n_chunks):
    c = pl.program_id(1)
    hg = pl.program_id(2)

    @pl.when(c == 0)
    def _():
        for j in range(GDN_HG):
            s_scr[hg * GDN_HG + j] = s0_ref[0, j]
        carry_scr[hg * 3 + 0] = bq_ref[0]
        carry_scr[hg * 3 + 1] = bk_ref[0]
        carry_scr[hg * 3 + 2] = bv_ref[0]

    def conv(which, x_ref, w_ref):
        xe_scr[which, 0:8, :] = carry_scr[hg * 3 + which]
        xe_scr[which, 8:8 + C, :] = x_ref[...]
        acc = w_ref[0:1, :] * xe_scr[which, 5:5 + C, :]
        for j in range(1, CONV_WIDTH):
            acc = acc + w_ref[j:j + 1, :] * xe_scr[which, 5 + j:5 + j + C, :]
        carry_scr[hg * 3 + which] = xe_scr[which, C:C + 8, :]
        return acc * jax.nn.sigmoid(acc)

    yq = conv(0, xq_ref, wq_ref)
    yk = conv(1, xk_ref, wk_ref)
    yv = conv(2, xv_ref, wv_ref)

    t = t_ref[...]
    beta = jax.nn.sigmoid(t)
    ta = t + dt_ref[...]
    softplus = jnp.maximum(ta, 0.0) + jnp.log(1.0 + jnp.exp(-jnp.abs(ta)))
    g = -jnp.exp(alog_ref[...]) * softplus
    r = lax.broadcasted_iota(jnp.int32, (C, C), 0)
    cc = lax.broadcasted_iota(jnp.int32, (C, C), 1)
    incl = r >= cc
    strict = r > cc
    gc = _dot(incl.astype(F32), g, HI)
    gc_t = gc.T
    lane = lax.broadcasted_iota(jnp.int32, (C, TAIL_W), 1)
    sub = lax.broadcasted_iota(jnp.int32, (TAIL_W, C), 0)

    for j in range(GDN_HG):
        head = hg * GDN_HG + j
        hs = slice(j * DK_GDN, (j + 1) * DK_GDN)
        bh = jnp.sum(jnp.where(lane == head, beta, 0.0), axis=-1, keepdims=True)
        gcol = jnp.sum(jnp.where(lane == N_HEADS_GDN + head, gc, 0.0), axis=-1, keepdims=True)
        grow = jnp.sum(jnp.where(sub == N_HEADS_GDN + head, gc_t, 0.0), axis=0, keepdims=True)
        glast = gcol[C - 1:C, :]
        decay = jnp.where(incl, jnp.exp(jnp.where(incl, gcol - grow, 0.0)), 0.0)
        e_g = jnp.exp(gcol)

        qh = yq[:, hs]
        kh = yk[:, hs]
        vh = yv[:, hs]
        qh = qh * lax.rsqrt(jnp.sum(qh * qh, axis=-1, keepdims=True) + RMS_EPS) * (DK_GDN ** -0.5)
        kh = kh * lax.rsqrt(jnp.sum(kh * kh, axis=-1, keepdims=True) + RMS_EPS)
        kb = kh * bh

        a = jnp.where(strict, _dot_nt(kb, kh, HI) * decay, 0.0)
        qk = _dot_nt(qh, kh, HI) * decay
        t_inv = _unit_lower_inverse(a, C)

        s_prev = s_scr[head]
        v_new = _dot(t_inv, vh * bh - _dot(kb * e_g, s_prev, HI), HI)
        o = _dot(qh * e_g, s_prev, HI) + _dot(qk, v_new, HI)
        k_dec = kh * jnp.exp(glast - gcol)
        s_new = s_prev * jnp.exp(glast) + _dot_tn(k_dec, v_new, HI)
        s_scr[head] = s_new

        o = o * lax.rsqrt(jnp.mean(o * o, axis=-1, keepdims=True) + RMS_EPS) * nw_ref[...]
        zh = z_ref[:, hs]
        o_ref[:, hs] = (o * (zh * jax.nn.sigmoid(zh))).astype(BF16)

        sout_ref[0, j] = s_new


def _gdn(h_src, tail_src, conv_w, alog_row, dt_row, nw_row, s0, buf0, *, n_seq, n_chunks, C, row_blk0,
         shared_init):
    n_hg = N_HEADS_GDN // GDN_HG
    qblk = COL_CONV // GDN_GW
    kblk = (COL_CONV + D_GDN) // GDN_GW
    vblk = (COL_CONV + 2 * D_GDN) // GDN_GW
    zblk = COL_Z // GDN_GW

    def rows(s, c):
        return row_blk0 + s * n_chunks + c

    init = (lambda s: 0) if shared_init else (lambda s: s)
    x_spec = lambda blk: pl.BlockSpec((C, GDN_GW), lambda s, c, g: (rows(s, c), blk + g))
    w_spec = lambda blk: pl.BlockSpec((CONV_WIDTH, GDN_GW), lambda s, c, g: (0, blk + g))
    b_spec = lambda blk: pl.BlockSpec((1, 8, GDN_GW), lambda s, c, g: (init(s), 0, blk + g))
    vec = pl.BlockSpec((1, TAIL_W), lambda s, c, g: (0, 0))
    return pl.pallas_call(
        functools.partial(_gdn_body, C=C, n_chunks=n_chunks),
        grid=(n_seq, n_chunks, n_hg),
        in_specs=[
            x_spec(qblk), x_spec(kblk), x_spec(vblk), x_spec(zblk),
            pl.BlockSpec((C, TAIL_W), lambda s, c, g: (rows(s, c), 0)),
            w_spec(0), w_spec(n_hg), w_spec(2 * n_hg),
            vec, vec, vec,
            pl.BlockSpec((1, GDN_HG, DK_GDN, DV_GDN), lambda s, c, g: (init(s), g, 0, 0)),
            b_spec(0), b_spec(n_hg), b_spec(2 * n_hg),
        ],
        out_specs=[
            pl.BlockSpec((C, GDN_GW), lambda s, c, g: (s * n_chunks + c, g)),
            pl.BlockSpec((1, GDN_HG, DK_GDN, DV_GDN), lambda s, c, g: (s, g, 0, 0)),
        ],
        out_shape=[
            jax.ShapeDtypeStruct((n_seq * n_chunks * C, D_GDN), BF16),
            jax.ShapeDtypeStruct((n_seq, N_HEADS_GDN, DK_GDN, DV_GDN), F32),
        ],
        scratch_shapes=[
            pltpu.VMEM((N_HEADS_GDN, DK_GDN, DV_GDN), F32),
            pltpu.VMEM((3 * n_hg, 8, GDN_GW), F32),
            pltpu.VMEM((3, 8 + C, GDN_GW), F32),
        ],
        compiler_params=pltpu.CompilerParams(dimension_semantics=("parallel", "arbitrary", "arbitrary")),
        name="gdn",
    )(h_src, h_src, h_src, h_src, tail_src, conv_w, conv_w, conv_w, alog_row, dt_row, nw_row, s0, buf0, buf0,
      buf0)


def _out_body(a_ref, b_ref, w_ref, x_ref, g_ref, beta_ref, o_ref, ob_ref):
    k = pl.program_id(1)
    half = D_SWA_Q // OUT_TK

    @pl.when(k == 0)
    def _():
        o_ref[...] = jnp.zeros_like(o_ref)

    @pl.when(k < half)
    def _():
        o_ref[...] += _dot(a_ref[...], w_ref[...])

    @pl.when(k >= half)
    def _():
        o_ref[...] += _dot(b_ref[...], w_ref[...])

    @pl.when(k == pl.num_programs(1) - 1)
    def _():
        y = _layernorm_rows(ALPHA * x_ref[...] + o_ref[...], g_ref[...], beta_ref[...])
        o_ref[...] = y
        ob_ref[...] = y.astype(BF16)


def _out_proj(o_swa, o_gdn, w_out, x1, g, b, *, tm):
    rows = x1.shape[0]
    half = D_SWA_Q // OUT_TK
    return pl.pallas_call(
        _out_body,
        grid=(rows // tm, 2 * half),
        in_specs=[
            pl.BlockSpec((tm, OUT_TK), lambda i, k: (i, jnp.minimum(k, half - 1))),
            pl.BlockSpec((tm, OUT_TK), lambda i, k: (i, jnp.maximum(k - half, 0))),
            pl.BlockSpec((OUT_TK, D_MODEL), lambda i, k: (k, 0)),
            pl.BlockSpec((tm, D_MODEL), lambda i, k: (i, 0), pipeline_mode=pl.Buffered(1)),
            pl.BlockSpec((1, D_MODEL), lambda i, k: (0, 0)),
            pl.BlockSpec((1, D_MODEL), lambda i, k: (0, 0)),
        ],
        out_specs=[
            pl.BlockSpec((tm, D_MODEL), lambda i, k: (i, 0)),
            pl.BlockSpec((tm, D_MODEL), lambda i, k: (i, 0)),
        ],
        out_shape=[
            jax.ShapeDtypeStruct((rows, D_MODEL), F32),
            jax.ShapeDtypeStruct((rows, D_MODEL), BF16),
        ],
        compiler_params=pltpu.CompilerParams(
            dimension_semantics=("parallel", "arbitrary"), vmem_limit_bytes=VMEM_LIMIT),
        name="proj_out",
    )(o_swa, o_gdn, w_out, x1, g, b)


def _rope_tables(pos):
    half = HEAD_DIM // 2
    inv = ROPE_THETA ** (-jnp.arange(half, dtype=F32) / half)
    ang = pos.astype(F32)[:, None] * inv[None, :]
    cos = jnp.cos(ang)
    sin = jnp.sin(ang)
    return jnp.concatenate([cos, cos], axis=1), jnp.concatenate([-sin, sin], axis=1)


def _ffn_weights(w_gate, w_up, w_down):
    n_f = D_FF // FFN_TF
    wg = w_gate.reshape(D_MODEL, n_f, FFN_TF)
    wu = w_up.reshape(D_MODEL, n_f, FFN_TF)
    wgu = jnp.concatenate([wg, wu], axis=2).transpose(1, 0, 2).astype(BF16)
    return wgu, w_down.astype(BF16)


def kernel(x_prompt, x_sample, cache_meta_k, cache_meta_v, cache_win_k, cache_win_v, state_conv, state_gdn,
           meta_tokens, ln_g, ln_b, ffn_w_gate, ffn_w_up, ffn_w_down, w_in, w_out, attn_sinks, conv_w,
           gdn_a_log, gdn_dt_bias, gdn_norm_w):
    l = 0
    wgu1, wd1 = _ffn_weights(ffn_w_gate[l, 0], ffn_w_up[l, 0], ffn_w_down[l, 0])
    wgu2, wd2 = _ffn_weights(ffn_w_gate[l, 1], ffn_w_up[l, 1], ffn_w_down[l, 1])
    wi = w_in[l]
    o_q, o_k, o_v = 0, D_SWA_Q, D_SWA_Q + D_SWA_KV
    o_conv = o_v + D_SWA_KV
    o_z = o_conv + D_CONV
    o_b = o_z + D_GDN
    w_main = jnp.concatenate([wi[:, o_conv:o_z], wi[:, o_z:o_b], wi[:, o_q:o_k], wi[:, o_k:o_v],
                              wi[:, o_v:o_conv]], axis=1)
    w_tiles = w_main.reshape(D_MODEL, D_MAIN // PROJ_TN, PROJ_TN).transpose(1, 0, 2).astype(BF16)
    w_tail = jnp.pad(wi[:, o_b:], ((0, 0), (0, TAIL_W - 2 * N_HEADS_GDN))).astype(BF16)
    wo = w_out[l].astype(BF16)
    g1, g2, g3 = (ln_g[l, i][None, :] for i in range(3))
    b1, b2, b3 = (ln_b[l, i][None, :] for i in range(3))
    pad_tail = lambda v, off: jnp.pad(v.astype(F32), (off, TAIL_W - off - N_HEADS_GDN))[None, :]
    alog_row = pad_tail(gdn_a_log[l], N_HEADS_GDN)
    dt_row = pad_tail(gdn_dt_bias[l], N_HEADS_GDN)
    nw_row = gdn_norm_w[l].astype(F32)[None, :]
    cw = conv_w[l]
    sinks = attn_sinks[l].astype(F32)

    x_main = jnp.concatenate([x_prompt.reshape(N_PROMPT_ROWS, D_MODEL), x_sample.reshape(N_SAMPLE_ROWS, D_MODEL)], 0)
    pos_main = jnp.concatenate([
        jnp.tile(N_META + jnp.arange(SEQ, dtype=jnp.int32), BATCH),
        jnp.tile(N_META + PAST_LEN + jnp.arange(DEC_SEQ, dtype=jnp.int32), DEC_BATCH)])
    cos_main, sin_main = _rope_tables(pos_main)
    cos_meta, sin_meta = _rope_tables(jnp.arange(N_META, dtype=jnp.int32))

    x1, x1b = _ffn(x_main, wgu1, wd1, g1, b1, tm=512, emit_bf16=True)
    _, x1b_meta = _ffn(meta_tokens.astype(F32), wgu1, wd1, g1, b1, tm=N_META, emit_bf16=True)
    h_main, t_main = _proj(x1b, w_tiles, w_tail, cos_main, sin_main, tm=1024)
    h_meta, t_meta = _proj(x1b_meta, w_tiles, w_tail, cos_meta, sin_meta, tm=N_META)

    o_swa_p = _attn_prompt(sinks, h_main, h_meta)
    o_swa_s = _attn_sample(
        sinks, h_main,
        cache_meta_k[l].reshape(DEC_BATCH * N_META, D_SWA_KV), cache_meta_v[l].reshape(DEC_BATCH * N_META, D_SWA_KV),
        cache_win_k[l].reshape(DEC_BATCH * WINDOW, D_SWA_KV), cache_win_v[l].reshape(DEC_BATCH * WINDOW, D_SWA_KV))
    o_swa = jnp.concatenate([o_swa_p, o_swa_s], axis=0)

    zero_s = jnp.zeros((1, N_HEADS_GDN, DK_GDN, DV_GDN), F32)
    zero_buf = jnp.zeros((1, 8, D_CONV), F32)
    gdn = functools.partial(_gdn, conv_w=cw, alog_row=alog_row, dt_row=dt_row, nw_row=nw_row)
    _, s_meta = gdn(h_meta, t_meta, s0=zero_s, buf0=zero_buf, n_seq=1, n_chunks=1, C=N_META, row_blk0=0,
                    shared_init=True)
    buf_meta = h_meta[None, N_META - 8:, COL_CONV:COL_CONV + D_CONV]
    o_gdn_p, s_prompt = gdn(h_main, t_main, s0=s_meta, buf0=buf_meta, n_seq=BATCH, n_chunks=CHUNKS_PER_SEQ,
                            C=CHUNK, row_blk0=0, shared_init=True)
    buf_s = jnp.pad(state_conv[l].astype(F32), ((0, 0), (8 - (CONV_WIDTH - 1), 0), (0, 0)))
    o_gdn_s, s_sample = gdn(h_main, t_main, s0=state_gdn[l].astype(F32), buf0=buf_s, n_seq=DEC_BATCH, n_chunks=1,
                            C=CHUNK, row_blk0=N_PROMPT_ROWS // CHUNK, shared_init=False)
    o_gdn = jnp.concatenate([o_gdn_p, o_gdn_s], axis=0)

    x2, _ = _out_proj(o_swa, o_gdn, wo, x1, g2, b2, tm=512)
    (y,) = _ffn(x2, wgu2, wd2, g3, b3, tm=512, emit_bf16=False)

    y_prompt = y[:N_PROMPT_ROWS].reshape(BATCH, SEQ, D_MODEL)
    y_sample = y[N_PROMPT_ROWS:].reshape(DEC_BATCH, DEC_SEQ, D_MODEL)
    k_meta = h_meta[:, COL_K:COL_K + D_SWA_KV].reshape(N_META, N_KV_SWA, HEAD_DIM)
    v_meta = h_meta[:, COL_V:COL_V + D_SWA_KV].reshape(N_META, N_KV_SWA, HEAD_DIM)
    p_meta_k = jnp.broadcast_to(k_meta[None, None], (1, BATCH, N_META, N_KV_SWA, HEAD_DIM))
    p_meta_v = jnp.broadcast_to(v_meta[None, None], (1, BATCH, N_META, N_KV_SWA, HEAD_DIM))
    hp = h_main[:N_PROMPT_ROWS].reshape(BATCH, SEQ, D_MAIN)
    hs = h_main[N_PROMPT_ROWS:].reshape(DEC_BATCH, DEC_SEQ, D_MAIN)
    p_win_k = hp[:, SEQ - WINDOW:, COL_K:COL_K + D_SWA_KV].reshape(1, BATCH, WINDOW, N_KV_SWA, HEAD_DIM)
    p_win_v = hp[:, SEQ - WINDOW:, COL_V:COL_V + D_SWA_KV].reshape(1, BATCH, WINDOW, N_KV_SWA, HEAD_DIM)
    p_conv = hp[:, SEQ - (CONV_WIDTH - 1):, COL_CONV:COL_CONV + D_CONV][None]
    p_gdn = s_prompt[None]
    s_win_k = hs[:, :, COL_K:COL_K + D_SWA_KV].reshape(1, DEC_BATCH, DEC_SEQ, N_KV_SWA, HEAD_DIM)
    s_win_v = hs[:, :, COL_V:COL_V + D_SWA_KV].reshape(1, DEC_BATCH, DEC_SEQ, N_KV_SWA, HEAD_DIM)
    s_conv = hs[:, DEC_SEQ - (CONV_WIDTH - 1):, COL_CONV:COL_CONV + D_CONV][None]
    s_gdn = s_sample[None]
    return (y_prompt, y_sample, p_meta_k, p_meta_v, p_win_k, p_win_v, p_conv, p_gdn, s_win_k, s_win_v, s_conv,
            s_gdn)
```

```python
import functools

import jax
import jax.numpy as jnp
from jax import lax
from jax.experimental import pallas as pl
from jax.experimental.pallas import tpu as pltpu

D_MODEL = 4096
BATCH = 4
SEQ = 2048
DEC_BATCH = 16
DEC_SEQ = 64
PAST_LEN = 2048
CHUNK = 64
N_META = 16
WINDOW = 128
HEAD_DIM = 128
N_HEADS_SWA = 16
N_KV_SWA = 4
GQA_GROUP = 4
N_HEADS_GDN = 16
DK_GDN = 128
DV_GDN = 128
CONV_WIDTH = 4
D_SWA_Q = N_HEADS_SWA * HEAD_DIM
D_SWA_KV = N_KV_SWA * HEAD_DIM
D_GDN = N_HEADS_GDN * DK_GDN
D_CONV = 3 * D_GDN
D_FF = 11008
ROPE_THETA = 10000.0
LN_EPS = 1e-5
RMS_EPS = 1e-6
ALPHA = 2.0 ** 0.25

N_PROMPT_ROWS = BATCH * SEQ
N_SAMPLE_ROWS = DEC_BATCH * DEC_SEQ
CHUNKS_PER_SEQ = SEQ // CHUNK

COL_CONV = 0
COL_Z = D_CONV
COL_Q = COL_Z + D_GDN
COL_K = COL_Q + D_SWA_Q
COL_V = COL_K + D_SWA_KV
D_MAIN = COL_V + D_SWA_KV
TAIL_W = 128

FFN_TF = 256
PROJ_TN = 512
OUT_TK = 1024
CARRY = 8

VMEM_LIMIT = 60 * 1024 * 1024

BF16 = jnp.bfloat16
F32 = jnp.float32
HI = lax.Precision.HIGHEST


def _dot(a, b, precision=None):
    return jnp.dot(a, b, preferred_element_type=F32, precision=precision)


def _dot_nt(a, b):
    return lax.dot_general(a, b, (((1,), (1,)), ((), ())), preferred_element_type=F32)


def _dot_tn(a, b):
    return lax.dot_general(a, b, (((0,), (0,)), ((), ())), preferred_element_type=F32)


def _head_dot(a, b, lhs_contract, rhs_contract):
    dims = (((lhs_contract,), (rhs_contract,)), ((0,), (0,)))
    return lax.dot_general(a.astype(BF16), b.astype(BF16), dims, preferred_element_type=F32)


_hdot = functools.partial(_head_dot, lhs_contract=2, rhs_contract=1)
_hdot_nt = functools.partial(_head_dot, lhs_contract=2, rhs_contract=2)
_hdot_tn = functools.partial(_head_dot, lhs_contract=1, rhs_contract=1)


def _layernorm_rows(y, g, b):
    mu = jnp.mean(y, axis=-1, keepdims=True)
    d = y - mu
    var = jnp.mean(d * d, axis=-1, keepdims=True)
    return d * lax.rsqrt(var + LN_EPS) * g + b


def _silu(x):
    return x * jax.nn.sigmoid(x)


def _ffn_body(x_ref, wg_ref, wu_ref, wd_ref, g_ref, b_ref, o_ref, *rest, emit_bf16):
    if emit_bf16:
        ob_ref, xb_ref = rest
    else:
        (xb_ref,) = rest
    f = pl.program_id(1)

    @pl.when(f == 0)
    def _():
        xb_ref[...] = x_ref[...].astype(BF16)
        o_ref[...] = jnp.zeros_like(o_ref)

    xb = xb_ref[...]
    hidden = _silu(_dot(xb, wg_ref[...])) * _dot(xb, wu_ref[...])
    o_ref[...] += _dot(hidden.astype(BF16), wd_ref[...])

    @pl.when(f == pl.num_programs(1) - 1)
    def _():
        y = _layernorm_rows(ALPHA * x_ref[...] + 0.5 * o_ref[...], g_ref[...], b_ref[...])
        o_ref[...] = y
        if emit_bf16:
            ob_ref[...] = y.astype(BF16)


def _ffn(x, wg, wu, wd, g, b, *, tm, emit_bf16):
    rows = x.shape[0]
    out_shape = [jax.ShapeDtypeStruct((rows, D_MODEL), F32)]
    out_specs = [pl.BlockSpec((tm, D_MODEL), lambda i, f: (i, 0))]
    if emit_bf16:
        out_shape.append(jax.ShapeDtypeStruct((rows, D_MODEL), BF16))
        out_specs.append(pl.BlockSpec((tm, D_MODEL), lambda i, f: (i, 0)))
    return pl.pallas_call(
        functools.partial(_ffn_body, emit_bf16=emit_bf16),
        grid=(rows // tm, D_FF // FFN_TF),
        in_specs=[
            pl.BlockSpec((tm, D_MODEL), lambda i, f: (i, 0), pipeline_mode=pl.Buffered(1)),
            pl.BlockSpec((D_MODEL, FFN_TF), lambda i, f: (0, f)),
            pl.BlockSpec((D_MODEL, FFN_TF), lambda i, f: (0, f)),
            pl.BlockSpec((FFN_TF, D_MODEL), lambda i, f: (f, 0)),
            pl.BlockSpec((1, D_MODEL), lambda i, f: (0, 0)),
            pl.BlockSpec((1, D_MODEL), lambda i, f: (0, 0)),
        ],
        out_specs=out_specs,
        out_shape=out_shape,
        scratch_shapes=[pltpu.VMEM((tm, D_MODEL), BF16)],
        compiler_params=pltpu.CompilerParams(
            dimension_semantics=("parallel", "arbitrary"), vmem_limit_bytes=VMEM_LIMIT),
        name="ffn",
    )(x, wg, wu, wd, g, b)


def _proj_body(x_ref, w_ref, wt_ref, cos_ref, sin_ref, o_ref, t_ref):
    j = pl.program_id(1)
    acc = _dot(x_ref[...], w_ref[...])
    is_rope = jnp.logical_and(j >= COL_Q // PROJ_TN, j < COL_V // PROJ_TN)

    @pl.when(is_rope)
    def _():
        cos = cos_ref[...]
        sin = sin_ref[...]
        for s in range(PROJ_TN // HEAD_DIM):
            blk = acc[:, s * HEAD_DIM:(s + 1) * HEAD_DIM]
            o_ref[:, s * HEAD_DIM:(s + 1) * HEAD_DIM] = blk * cos + pltpu.roll(blk, HEAD_DIM // 2, 1) * sin

    @pl.when(jnp.logical_not(is_rope))
    def _():
        o_ref[...] = acc

    @pl.when(j == 0)
    def _():
        t_ref[...] = _dot(x_ref[...], wt_ref[...])


def _proj(xb, w_main, w_tail, cos, sin, *, tm):
    rows = xb.shape[0]
    return pl.pallas_call(
        _proj_body,
        grid=(rows // tm, D_MAIN // PROJ_TN),
        in_specs=[
            pl.BlockSpec((tm, D_MODEL), lambda i, j: (i, 0)),
            pl.BlockSpec((D_MODEL, PROJ_TN), lambda i, j: (0, j)),
            pl.BlockSpec((D_MODEL, TAIL_W), lambda i, j: (0, 0)),
            pl.BlockSpec((tm, HEAD_DIM), lambda i, j: (i, 0)),
            pl.BlockSpec((tm, HEAD_DIM), lambda i, j: (i, 0)),
        ],
        out_specs=[
            pl.BlockSpec((tm, PROJ_TN), lambda i, j: (i, j)),
            pl.BlockSpec((tm, TAIL_W), lambda i, j: (i, 0)),
        ],
        out_shape=[
            jax.ShapeDtypeStruct((rows, D_MAIN), F32),
            jax.ShapeDtypeStruct((rows, TAIL_W), F32),
        ],
        compiler_params=pltpu.CompilerParams(
            dimension_semantics=("parallel", "arbitrary"), vmem_limit_bytes=VMEM_LIMIT),
        name="proj_in",
    )(xb, w_main, w_tail, cos, sin)


N_KEYS = N_META + 3 * CHUNK


def _attn_body(sink_ref, q_ref, km_ref, vm_ref, k2_ref, v2_ref, k1_ref, v1_ref, k0_ref, v0_ref, o_ref, *,
               prompt):
    col = lax.broadcasted_iota(jnp.int32, (1, N_KEYS), 1)
    if prompt:
        c = pl.program_id(0) % CHUNKS_PER_SEQ
        ok2 = jnp.logical_or(col >= N_META + CHUNK, c >= 2)
        ok1 = jnp.logical_or(jnp.logical_or(col < N_META + CHUNK, col >= N_META + 2 * CHUNK), c >= 1)
        mask = jnp.logical_or(col < N_META, jnp.logical_and(ok2, ok1))
    else:
        mask = col >= 0
    scale = HEAD_DIM ** -0.5
    for g in range(N_KV_SWA):
        ks = slice(g * HEAD_DIM, (g + 1) * HEAD_DIM)
        k = jnp.concatenate([km_ref[:, ks], k2_ref[:, ks], k1_ref[:, ks], k0_ref[:, ks]], axis=0).astype(BF16)
        v = jnp.concatenate([vm_ref[:, ks], v2_ref[:, ks], v1_ref[:, ks], v0_ref[:, ks]], axis=0).astype(BF16)
        heads = [g * GQA_GROUP + j for j in range(GQA_GROUP)]
        q = jnp.concatenate([q_ref[:, h * HEAD_DIM:(h + 1) * HEAD_DIM] for h in heads], axis=0).astype(BF16)
        sink = jnp.concatenate([jnp.full((CHUNK, 1), sink_ref[h], F32) for h in heads], axis=0)
        s = _dot_nt(q, k) * scale
        s = jnp.where(mask, s, -jnp.inf)
        m = jnp.maximum(jnp.max(s, axis=-1, keepdims=True), sink)
        p = jnp.exp(s - m)
        den = jnp.sum(p, axis=-1, keepdims=True) + jnp.exp(sink - m)
        o = _dot(p.astype(BF16), v) / den
        for j, h in enumerate(heads):
            o_ref[:, h * HEAD_DIM:(h + 1) * HEAD_DIM] = o[j * CHUNK:(j + 1) * CHUNK].astype(BF16)


def _attn_prompt(sinks, h_p, h_m):
    kcol = COL_K // D_SWA_KV
    vcol = COL_V // D_SWA_KV
    qcol = COL_Q // D_SWA_Q

    def prev(n, colblk):
        return lambda p: (p - jnp.where(p % CHUNKS_PER_SEQ >= n, n, 0), colblk)

    kv = lambda imap: pl.BlockSpec((CHUNK, D_SWA_KV), imap)
    return pl.pallas_call(
        functools.partial(_attn_body, prompt=True),
        grid=(N_PROMPT_ROWS // CHUNK,),
        in_specs=[
            pl.BlockSpec(memory_space=pltpu.SMEM),
            pl.BlockSpec((CHUNK, D_SWA_Q), lambda p: (p, qcol)),
            pl.BlockSpec((N_META, D_SWA_KV), lambda p: (0, kcol)),
            pl.BlockSpec((N_META, D_SWA_KV), lambda p: (0, vcol)),
            kv(prev(2, kcol)), kv(prev(2, vcol)),
            kv(prev(1, kcol)), kv(prev(1, vcol)),
            kv(lambda p: (p, kcol)), kv(lambda p: (p, vcol)),
        ],
        out_specs=pl.BlockSpec((CHUNK, D_SWA_Q), lambda p: (p, 0)),
        out_shape=jax.ShapeDtypeStruct((N_PROMPT_ROWS, D_SWA_Q), BF16),
        compiler_params=pltpu.CompilerParams(dimension_semantics=("parallel",)),
        name="swa_prompt",
    )(sinks, h_p, h_m, h_m, h_p, h_p, h_p, h_p, h_p, h_p)


def _attn_sample(sinks, h_s, meta_k, meta_v, win_k, win_v):
    kcol = COL_K // D_SWA_KV
    vcol = COL_V // D_SWA_KV
    qcol = COL_Q // D_SWA_Q
    kv = lambda imap: pl.BlockSpec((CHUNK, D_SWA_KV), imap)
    return pl.pallas_call(
        functools.partial(_attn_body, prompt=False),
        grid=(DEC_BATCH,),
        in_specs=[
            pl.BlockSpec(memory_space=pltpu.SMEM),
            pl.BlockSpec((CHUNK, D_SWA_Q), lambda s: (s, qcol)),
            pl.BlockSpec((N_META, D_SWA_KV), lambda s: (s, 0)),
            pl.BlockSpec((N_META, D_SWA_KV), lambda s: (s, 0)),
            kv(lambda s: (2 * s, 0)), kv(lambda s: (2 * s, 0)),
            kv(lambda s: (2 * s + 1, 0)), kv(lambda s: (2 * s + 1, 0)),
            kv(lambda s: (s, kcol)), kv(lambda s: (s, vcol)),
        ],
        out_specs=pl.BlockSpec((CHUNK, D_SWA_Q), lambda s: (s, 0)),
        out_shape=jax.ShapeDtypeStruct((N_SAMPLE_ROWS, D_SWA_Q), BF16),
        compiler_params=pltpu.CompilerParams(dimension_semantics=("parallel",)),
        name="swa_sample",
    )(sinks, h_s, meta_k, meta_v, win_k, win_v, win_k, win_v, h_s, h_s)


def _unit_lower_inverse(a, n):
    r = lax.broadcasted_iota(jnp.int32, (n, n), 0)
    c = lax.broadcasted_iota(jnp.int32, (n, n), 1)
    eye = (r == c).astype(F32)
    a0 = jnp.where((r >> 3) == (c >> 3), a, 0.0)
    a2 = _hdot(a0, a0)
    a4 = _hdot(a2, a2)
    x = _hdot(_hdot(eye - a0, eye + a2), eye + a4)
    shift = 3
    while (1 << shift) < n:
        pair = jnp.logical_and((r >> (shift + 1)) == (c >> (shift + 1)), (r >> shift) != (c >> shift))
        ak = jnp.where(pair, a, 0.0)
        x = x - _hdot(x, _hdot(ak, x))
        shift += 1
    return x


def _gdn_body(x_ref, z_ref, t_ref, w_ref, alog_ref, dt_ref, nw_ref, s0_ref, b0_ref, o_ref, sout_ref, s_scr,
              xe_scr, *, C, n_chunks):
    c = pl.program_id(1)

    @pl.when(c == 0)
    def _():
        s_scr[...] = s0_ref[0]
        xe_scr[0:CARRY, :] = b0_ref[0]

    @pl.when(c > 0)
    def _():
        xe_scr[0:CARRY, :] = xe_scr[C:C + CARRY, :]

    xe_scr[CARRY:CARRY + C, :] = x_ref[...]

    t = t_ref[...]
    beta = jax.nn.sigmoid(t)
    ta = t + dt_ref[...]
    softplus = jnp.maximum(ta, 0.0) + jnp.log(1.0 + jnp.exp(-jnp.abs(ta)))
    g = -jnp.exp(alog_ref[...]) * softplus
    r = lax.broadcasted_iota(jnp.int32, (C, C), 0)
    cc = lax.broadcasted_iota(jnp.int32, (C, C), 1)
    incl = r >= cc
    strict = r > cc
    gc = _dot(incl.astype(F32), g, HI)
    gc_t = gc.T
    heads = range(N_HEADS_GDN)
    bh = jnp.stack([beta[:, h:h + 1] for h in heads])
    gcol = jnp.stack([gc[:, N_HEADS_GDN + h:N_HEADS_GDN + h + 1] for h in heads])
    grow = jnp.stack([gc_t[N_HEADS_GDN + h:N_HEADS_GDN + h + 1, :] for h in heads])
    glast = gcol[:, C - 1:C, :]
    decay = jnp.where(incl, jnp.exp(jnp.where(incl, gcol - grow, 0.0)), 0.0)
    e_g = jnp.exp(gcol)

    first = CARRY - (CONV_WIDTH - 1)
    acc = w_ref[0:1, :] * xe_scr[first:first + C, :]
    for j in range(1, CONV_WIDTH):
        acc = acc + w_ref[j:j + 1, :] * xe_scr[first + j:first + j + C, :]
    y = _silu(acc)
    split = lambda base: jnp.stack([y[:, base + h * DK_GDN:base + (h + 1) * DK_GDN] for h in heads])
    q, k, v = split(0), split(D_GDN), split(2 * D_GDN)
    q = q * lax.rsqrt(jnp.sum(q * q, axis=-1, keepdims=True) + RMS_EPS) * (DK_GDN ** -0.5)
    k = k * lax.rsqrt(jnp.sum(k * k, axis=-1, keepdims=True) + RMS_EPS)
    kb = k * bh

    kq = _hdot_nt(jnp.concatenate([kb, q], axis=1), k)
    a = jnp.where(strict, kq[:, :C] * decay, 0.0)
    qk = kq[:, C:] * decay
    t_inv = _unit_lower_inverse(a, C)

    s_prev = s_scr[...]
    ws = _hdot(jnp.concatenate([kb * e_g, q * e_g], axis=1), s_prev)
    v_new = _hdot(t_inv, v * bh - ws[:, :C])
    o = ws[:, C:] + _hdot(qk, v_new)
    s_scr[...] = s_prev * jnp.exp(glast) + _hdot_tn(k * jnp.exp(glast - gcol), v_new)

    o = o * lax.rsqrt(jnp.mean(o * o, axis=-1, keepdims=True) + RMS_EPS) * nw_ref[...]
    for h in heads:
        hs = slice(h * DV_GDN, (h + 1) * DV_GDN)
        o_ref[:, hs] = (o[h] * _silu(z_ref[:, hs])).astype(BF16)

    @pl.when(c == n_chunks - 1)
    def _():
        sout_ref[0] = s_scr[...]


def _gdn(h_src, tail_src, conv_w, alog_row, dt_row, nw_row, s0, buf0, *, n_seq, n_chunks, C, shared_init):
    init = (lambda s: 0) if shared_init else (lambda s: s)
    vec = pl.BlockSpec((1, TAIL_W), lambda s, c: (0, 0))
    return pl.pallas_call(
        functools.partial(_gdn_body, C=C, n_chunks=n_chunks),
        grid=(n_seq, n_chunks),
        in_specs=[
            pl.BlockSpec((C, D_CONV), lambda s, c: (s * n_chunks + c, COL_CONV // D_CONV)),
            pl.BlockSpec((C, D_GDN), lambda s, c: (s * n_chunks + c, COL_Z // D_GDN)),
            pl.BlockSpec((C, TAIL_W), lambda s, c: (s * n_chunks + c, 0)),
            pl.BlockSpec((CONV_WIDTH, D_CONV), lambda s, c: (0, 0)),
            vec, vec, vec,
            pl.BlockSpec((1, N_HEADS_GDN, DK_GDN, DV_GDN), lambda s, c: (init(s), 0, 0, 0)),
            pl.BlockSpec((1, CARRY, D_CONV), lambda s, c: (init(s), 0, 0)),
        ],
        out_specs=[
            pl.BlockSpec((C, D_GDN), lambda s, c: (s * n_chunks + c, 0)),
            pl.BlockSpec((1, N_HEADS_GDN, DK_GDN, DV_GDN), lambda s, c: (s, 0, 0, 0)),
        ],
        out_shape=[
            jax.ShapeDtypeStruct((n_seq * n_chunks * C, D_GDN), BF16),
            jax.ShapeDtypeStruct((n_seq, N_HEADS_GDN, DK_GDN, DV_GDN), F32),
        ],
        scratch_shapes=[
            pltpu.VMEM((N_HEADS_GDN, DK_GDN, DV_GDN), F32),
            pltpu.VMEM((CARRY + C, D_CONV), F32),
        ],
        compiler_params=pltpu.CompilerParams(dimension_semantics=("parallel", "arbitrary")),
        name="gdn",
    )(h_src, h_src, tail_src, conv_w, alog_row, dt_row, nw_row, s0, buf0)


def _out_body(a_ref, b_ref, w_ref, x_ref, g_ref, beta_ref, o_ref):
    k = pl.program_id(1)
    half = D_SWA_Q // OUT_TK

    @pl.when(k == 0)
    def _():
        o_ref[...] = jnp.zeros_like(o_ref)

    @pl.when(k < half)
    def _():
        o_ref[...] += _dot(a_ref[...], w_ref[...])

    @pl.when(k >= half)
    def _():
        o_ref[...] += _dot(b_ref[...], w_ref[...])

    @pl.when(k == pl.num_programs(1) - 1)
    def _():
        o_ref[...] = _layernorm_rows(ALPHA * x_ref[...] + o_ref[...], g_ref[...], beta_ref[...])


def _out_proj(o_swa, o_gdn, w_out, x1, g, b, *, tm):
    rows = x1.shape[0]
    half = D_SWA_Q // OUT_TK
    return pl.pallas_call(
        _out_body,
        grid=(rows // tm, 2 * half),
        in_specs=[
            pl.BlockSpec((tm, OUT_TK), lambda i, k: (i, jnp.minimum(k, half - 1))),
            pl.BlockSpec((tm, OUT_TK), lambda i, k: (i, jnp.maximum(k - half, 0))),
            pl.BlockSpec((OUT_TK, D_MODEL), lambda i, k: (k, 0)),
            pl.BlockSpec((tm, D_MODEL), lambda i, k: (i, 0), pipeline_mode=pl.Buffered(1)),
            pl.BlockSpec((1, D_MODEL), lambda i, k: (0, 0)),
            pl.BlockSpec((1, D_MODEL), lambda i, k: (0, 0)),
        ],
        out_specs=pl.BlockSpec((tm, D_MODEL), lambda i, k: (i, 0)),
        out_shape=jax.ShapeDtypeStruct((rows, D_MODEL), F32),
        compiler_params=pltpu.CompilerParams(
            dimension_semantics=("parallel", "arbitrary"), vmem_limit_bytes=VMEM_LIMIT),
        name="proj_out",
    )(o_swa, o_gdn, w_out, x1, g, b)


def _rope_tables(pos):
    half = HEAD_DIM // 2
    inv = ROPE_THETA ** (-jnp.arange(half, dtype=F32) / half)
    ang = pos.astype(F32)[:, None] * inv[None, :]
    cos = jnp.cos(ang)
    sin = jnp.sin(ang)
    return jnp.concatenate([cos, cos], axis=1), jnp.concatenate([-sin, sin], axis=1)


def kernel(x_prompt, x_sample, cache_meta_k, cache_meta_v, cache_win_k, cache_win_v, state_conv, state_gdn,
           meta_tokens, ln_g, ln_b, ffn_w_gate, ffn_w_up, ffn_w_down, w_in, w_out, attn_sinks, conv_w,
           gdn_a_log, gdn_dt_bias, gdn_norm_w):
    l = 0
    wg1, wu1, wd1 = (w[l, 0].astype(BF16) for w in (ffn_w_gate, ffn_w_up, ffn_w_down))
    wg2, wu2, wd2 = (w[l, 1].astype(BF16) for w in (ffn_w_gate, ffn_w_up, ffn_w_down))
    wi = w_in[l]
    o_q, o_k, o_v = 0, D_SWA_Q, D_SWA_Q + D_SWA_KV
    o_conv = o_v + D_SWA_KV
    o_z = o_conv + D_CONV
    o_b = o_z + D_GDN
    w_main = jnp.concatenate([wi[:, o_conv:o_z], wi[:, o_z:o_b], wi[:, o_q:o_k], wi[:, o_k:o_v],
                              wi[:, o_v:o_conv]], axis=1).astype(BF16)
    w_tail = jnp.pad(wi[:, o_b:], ((0, 0), (0, TAIL_W - 2 * N_HEADS_GDN))).astype(BF16)
    wo = w_out[l].astype(BF16)
    g1, g2, g3 = (ln_g[l, i][None, :] for i in range(3))
    b1, b2, b3 = (ln_b[l, i][None, :] for i in range(3))
    pad_tail = lambda v, off: jnp.pad(v.astype(F32), (off, TAIL_W - off - N_HEADS_GDN))[None, :]
    alog_row = pad_tail(gdn_a_log[l], N_HEADS_GDN)
    dt_row = pad_tail(gdn_dt_bias[l], N_HEADS_GDN)
    nw_row = gdn_norm_w[l].astype(F32)[None, :]
    cw = conv_w[l]
    sinks = attn_sinks[l].astype(F32)

    xp = x_prompt.reshape(N_PROMPT_ROWS, D_MODEL)
    xs = x_sample.reshape(N_SAMPLE_ROWS, D_MODEL)
    xm = meta_tokens.astype(F32)
    cos_p, sin_p = _rope_tables(jnp.tile(N_META + jnp.arange(SEQ, dtype=jnp.int32), BATCH))
    cos_s, sin_s = _rope_tables(jnp.tile(N_META + PAST_LEN + jnp.arange(DEC_SEQ, dtype=jnp.int32), DEC_BATCH))
    cos_m, sin_m = _rope_tables(jnp.arange(N_META, dtype=jnp.int32))

    ffn1 = functools.partial(_ffn, wg=wg1, wu=wu1, wd=wd1, g=g1, b=b1, emit_bf16=True)
    x1_p, x1b_p = ffn1(xp, tm=512)
    x1_s, x1b_s = ffn1(xs, tm=512)
    _, x1b_m = ffn1(xm, tm=N_META)
    h_p, t_p = _proj(x1b_p, w_main, w_tail, cos_p, sin_p, tm=1024)
    h_s, t_s = _proj(x1b_s, w_main, w_tail, cos_s, sin_s, tm=1024)
    h_m, t_m = _proj(x1b_m, w_main, w_tail, cos_m, sin_m, tm=N_META)

    o_swa_p = _attn_prompt(sinks, h_p, h_m)
    o_swa_s = _attn_sample(
        sinks, h_s,
        cache_meta_k[l].reshape(DEC_BATCH * N_META, D_SWA_KV), cache_meta_v[l].reshape(DEC_BATCH * N_META, D_SWA_KV),
        cache_win_k[l].reshape(DEC_BATCH * WINDOW, D_SWA_KV), cache_win_v[l].reshape(DEC_BATCH * WINDOW, D_SWA_KV))

    zero_s = jnp.zeros((1, N_HEADS_GDN, DK_GDN, DV_GDN), F32)
    zero_buf = jnp.zeros((1, CARRY, D_CONV), F32)
    gdn = functools.partial(_gdn, conv_w=cw, alog_row=alog_row, dt_row=dt_row, nw_row=nw_row)
    _, s_meta = gdn(h_m, t_m, s0=zero_s, buf0=zero_buf, n_seq=1, n_chunks=1, C=N_META, shared_init=True)
    buf_meta = h_m[None, N_META - CARRY:, COL_CONV:COL_CONV + D_CONV]
    o_gdn_p, s_prompt = gdn(h_p, t_p, s0=s_meta, buf0=buf_meta, n_seq=BATCH, n_chunks=CHUNKS_PER_SEQ, C=CHUNK,
                            shared_init=True)
    buf_s = jnp.pad(state_conv[l].astype(F32), ((0, 0), (CARRY - (CONV_WIDTH - 1), 0), (0, 0)))
    o_gdn_s, s_sample = gdn(h_s, t_s, s0=state_gdn[l].astype(F32), buf0=buf_s, n_seq=DEC_BATCH, n_chunks=1,
                            C=CHUNK, shared_init=False)

    x2_p = _out_proj(o_swa_p, o_gdn_p, wo, x1_p, g2, b2, tm=512)
    x2_s = _out_proj(o_swa_s, o_gdn_s, wo, x1_s, g2, b2, tm=512)
    ffn2 = functools.partial(_ffn, wg=wg2, wu=wu2, wd=wd2, g=g3, b=b3, emit_bf16=False)
    (y_p,) = ffn2(x2_p, tm=512)
    (y_s,) = ffn2(x2_s, tm=512)

    y_prompt = y_p.reshape(BATCH, SEQ, D_MODEL)
    y_sample = y_s.reshape(DEC_BATCH, DEC_SEQ, D_MODEL)
    k_meta = h_m[:, COL_K:COL_K + D_SWA_KV].reshape(N_META, N_KV_SWA, HEAD_DIM)
    v_meta = h_m[:, COL_V:COL_V + D_SWA_KV].reshape(N_META, N_KV_SWA, HEAD_DIM)
    p_meta_k = jnp.broadcast_to(k_meta[None, None], (1, BATCH, N_META, N_KV_SWA, HEAD_DIM))
    p_meta_v = jnp.broadcast_to(v_meta[None, None], (1, BATCH, N_META, N_KV_SWA, HEAD_DIM))
    hp = h_p.reshape(BATCH, SEQ, D_MAIN)
    hs = h_s.reshape(DEC_BATCH, DEC_SEQ, D_MAIN)
    p_win_k = hp[:, SEQ - WINDOW:, COL_K:COL_K + D_SWA_KV].reshape(1, BATCH, WINDOW, N_KV_SWA, HEAD_DIM)
    p_win_v = hp[:, SEQ - WINDOW:, COL_V:COL_V + D_SWA_KV].reshape(1, BATCH, WINDOW, N_KV_SWA, HEAD_DIM)
    p_conv = hp[:, SEQ - (CONV_WIDTH - 1):, COL_CONV:COL_CONV + D_CONV][None]
    p_gdn = s_prompt[None]
    s_win_k = hs[:, :, COL_K:COL_K + D_SWA_KV].reshape(1, DEC_BATCH, DEC_SEQ, N_KV_SWA, HEAD_DIM)
    s_win_v = hs[:, :, COL_V:COL_V + D_SWA_KV].reshape(1, DEC_BATCH, DEC_SEQ, N_KV_SWA, HEAD_DIM)
    s_conv = hs[:, DEC_SEQ - (CONV_WIDTH - 1):, COL_CONV:COL_CONV + D_CONV][None]
    s_gdn = s_sample[None]
    return (y_prompt, y_sample, p_meta_k, p_meta_v, p_win_k, p_win_v, p_conv, p_gdn, s_win_k, s_win_v, s_conv,
            s_gdn)
```

```python
import functools

import jax
import jax.numpy as jnp
from jax import lax
from jax.experimental import pallas as pl
from jax.experimental.pallas import tpu as pltpu

D_MODEL = 4096
BATCH = 4
SEQ = 2048
DEC_BATCH = 16
DEC_SEQ = 64
PAST_LEN = 2048
CHUNK = 64
N_META = 16
WINDOW = 128
HEAD_DIM = 128
N_HEADS_SWA = 16
N_KV_SWA = 4
GQA_GROUP = 4
N_HEADS_GDN = 16
DK_GDN = 128
DV_GDN = 128
CONV_WIDTH = 4
D_SWA_Q = N_HEADS_SWA * HEAD_DIM
D_SWA_KV = N_KV_SWA * HEAD_DIM
D_GDN = N_HEADS_GDN * DK_GDN
D_CONV = 3 * D_GDN
D_FF = 11008
ROPE_THETA = 10000.0
LN_EPS = 1e-5
RMS_EPS = 1e-6
ALPHA = 2.0 ** 0.25

N_PROMPT_ROWS = BATCH * SEQ
N_SAMPLE_ROWS = DEC_BATCH * DEC_SEQ
CHUNKS_PER_SEQ = SEQ // CHUNK

COL_CONV = 0
COL_Z = D_CONV
COL_Q = COL_Z + D_GDN
COL_K = COL_Q + D_SWA_Q
COL_V = COL_K + D_SWA_KV
D_MAIN = COL_V + D_SWA_KV
TAIL_W = 128

FFN_TF = 256
PROJ_TN = 512
OUT_TK = 1024
CARRY = 8

VMEM_LIMIT = 60 * 1024 * 1024

BF16 = jnp.bfloat16
F32 = jnp.float32
HI = lax.Precision.HIGHEST


def _dot(a, b, precision=None):
    return jnp.dot(a, b, preferred_element_type=F32, precision=precision)


def _dot_nt(a, b):
    return lax.dot_general(a, b, (((1,), (1,)), ((), ())), preferred_element_type=F32)


def _dot_tn(a, b):
    return lax.dot_general(a, b, (((0,), (0,)), ((), ())), preferred_element_type=F32)


def _head_dot(a, b, lhs_contract, rhs_contract):
    dims = (((lhs_contract,), (rhs_contract,)), ((0,), (0,)))
    return lax.dot_general(a.astype(BF16), b.astype(BF16), dims, preferred_element_type=F32)


_hdot = functools.partial(_head_dot, lhs_contract=2, rhs_contract=1)
_hdot_nt = functools.partial(_head_dot, lhs_contract=2, rhs_contract=2)
_hdot_tn = functools.partial(_head_dot, lhs_contract=1, rhs_contract=1)


def _layernorm_rows(y, g, b):
    mu = jnp.mean(y, axis=-1, keepdims=True)
    d = y - mu
    var = jnp.mean(d * d, axis=-1, keepdims=True)
    return d * lax.rsqrt(var + LN_EPS) * g + b


def _silu(x):
    return x * jax.nn.sigmoid(x)


def _ffn_body(x_ref, wg_ref, wu_ref, wd_ref, g_ref, b_ref, o_ref, *rest, emit_bf16):
    if emit_bf16:
        ob_ref, xb_ref = rest
    else:
        (xb_ref,) = rest
    f = pl.program_id(1)

    @pl.when(f == 0)
    def _():
        xb_ref[...] = x_ref[...].astype(BF16)
        o_ref[...] = jnp.zeros_like(o_ref)

    xb = xb_ref[...]
    hidden = _silu(_dot(xb, wg_ref[...])) * _dot(xb, wu_ref[...])
    o_ref[...] += _dot(hidden.astype(BF16), wd_ref[...])

    @pl.when(f == pl.num_programs(1) - 1)
    def _():
        y = _layernorm_rows(ALPHA * x_ref[...] + 0.5 * o_ref[...], g_ref[...], b_ref[...])
        o_ref[...] = y
        if emit_bf16:
            ob_ref[...] = y.astype(BF16)


def _ffn(x, wg, wu, wd, g, b, *, which, tm, emit_bf16):
    rows = x.shape[0]
    out_shape = [jax.ShapeDtypeStruct((rows, D_MODEL), F32)]
    out_specs = [pl.BlockSpec((tm, D_MODEL), lambda i, f: (i, 0))]
    if emit_bf16:
        out_shape.append(jax.ShapeDtypeStruct((rows, D_MODEL), BF16))
        out_specs.append(pl.BlockSpec((tm, D_MODEL), lambda i, f: (i, 0)))
    return pl.pallas_call(
        functools.partial(_ffn_body, emit_bf16=emit_bf16),
        grid=(rows // tm, D_FF // FFN_TF),
        in_specs=[
            pl.BlockSpec((tm, D_MODEL), lambda i, f: (i, 0), pipeline_mode=pl.Buffered(1)),
            pl.BlockSpec((None, None, D_MODEL, FFN_TF), lambda i, f: (0, which, 0, f)),
            pl.BlockSpec((None, None, D_MODEL, FFN_TF), lambda i, f: (0, which, 0, f)),
            pl.BlockSpec((None, None, FFN_TF, D_MODEL), lambda i, f: (0, which, f, 0)),
            pl.BlockSpec((1, D_MODEL), lambda i, f: (0, 0)),
            pl.BlockSpec((1, D_MODEL), lambda i, f: (0, 0)),
        ],
        out_specs=out_specs,
        out_shape=out_shape,
        scratch_shapes=[pltpu.VMEM((tm, D_MODEL), BF16)],
        compiler_params=pltpu.CompilerParams(
            dimension_semantics=("parallel", "arbitrary"), vmem_limit_bytes=VMEM_LIMIT),
        name="ffn",
    )(x, wg, wu, wd, g, b)


def _proj_body(x_ref, w_ref, wt_ref, cos_ref, sin_ref, o_ref, t_ref):
    j = pl.program_id(1)
    acc = _dot(x_ref[...], w_ref[...])
    is_rope = j < (D_SWA_Q + D_SWA_KV) // PROJ_TN

    @pl.when(is_rope)
    def _():
        cos = cos_ref[...]
        sin = sin_ref[...]
        for s in range(PROJ_TN // HEAD_DIM):
            blk = acc[:, s * HEAD_DIM:(s + 1) * HEAD_DIM]
            o_ref[:, s * HEAD_DIM:(s + 1) * HEAD_DIM] = blk * cos + pltpu.roll(blk, HEAD_DIM // 2, 1) * sin

    @pl.when(jnp.logical_not(is_rope))
    def _():
        o_ref[...] = acc

    @pl.when(j == 0)
    def _():
        t_ref[...] = _dot(x_ref[...], wt_ref[...])


def _proj(xb, w_in, w_tail, cos, sin, *, tm):
    rows = xb.shape[0]
    n_swa = (D_SWA_Q + 2 * D_SWA_KV) // PROJ_TN

    def dest(i, j):
        return i, jnp.where(j < n_swa, j + COL_Q // PROJ_TN, j - n_swa)

    return pl.pallas_call(
        _proj_body,
        grid=(rows // tm, D_MAIN // PROJ_TN),
        in_specs=[
            pl.BlockSpec((tm, D_MODEL), lambda i, j: (i, 0)),
            pl.BlockSpec((None, D_MODEL, PROJ_TN), lambda i, j: (0, 0, j)),
            pl.BlockSpec((D_MODEL, TAIL_W), lambda i, j: (0, 0)),
            pl.BlockSpec((tm, HEAD_DIM), lambda i, j: (i, 0)),
            pl.BlockSpec((tm, HEAD_DIM), lambda i, j: (i, 0)),
        ],
        out_specs=[
            pl.BlockSpec((tm, PROJ_TN), dest),
            pl.BlockSpec((tm, TAIL_W), lambda i, j: (i, 0)),
        ],
        out_shape=[
            jax.ShapeDtypeStruct((rows, D_MAIN), F32),
            jax.ShapeDtypeStruct((rows, TAIL_W), F32),
        ],
        compiler_params=pltpu.CompilerParams(
            dimension_semantics=("parallel", "arbitrary"), vmem_limit_bytes=VMEM_LIMIT),
        name="proj_in",
    )(xb, w_in, w_tail, cos, sin)


N_KEYS = N_META + 3 * CHUNK


def _attn_body(sink_ref, q_ref, km_ref, vm_ref, k2_ref, v2_ref, k1_ref, v1_ref, k0_ref, v0_ref, o_ref, *,
               prompt):
    col = lax.broadcasted_iota(jnp.int32, (1, N_KEYS), 1)
    if prompt:
        c = pl.program_id(0) % CHUNKS_PER_SEQ
        ok2 = jnp.logical_or(col >= N_META + CHUNK, c >= 2)
        ok1 = jnp.logical_or(jnp.logical_or(col < N_META + CHUNK, col >= N_META + 2 * CHUNK), c >= 1)
        mask = jnp.logical_or(col < N_META, jnp.logical_and(ok2, ok1))
    else:
        mask = col >= 0
    scale = HEAD_DIM ** -0.5
    groups = range(N_KV_SWA)
    head_cols = lambda h: slice(h * HEAD_DIM, (h + 1) * HEAD_DIM)
    group_heads = lambda g: range(g * GQA_GROUP, (g + 1) * GQA_GROUP)
    k = jnp.stack([jnp.concatenate([r[:, head_cols(g)] for r in (km_ref, k2_ref, k1_ref, k0_ref)], axis=0)
                   for g in groups])
    v = jnp.stack([jnp.concatenate([r[:, head_cols(g)] for r in (vm_ref, v2_ref, v1_ref, v0_ref)], axis=0)
                   for g in groups])
    q = jnp.stack([jnp.concatenate([q_ref[:, head_cols(h)] for h in group_heads(g)], axis=0)
                   for g in groups])
    sink = jnp.stack([jnp.concatenate([jnp.full((CHUNK, 1), sink_ref[h], F32) for h in group_heads(g)], axis=0)
                      for g in groups])
    s = _hdot_nt(q, k) * scale
    s = jnp.where(mask, s, -jnp.inf)
    m = jnp.maximum(jnp.max(s, axis=-1, keepdims=True), sink)
    p = jnp.exp(s - m)
    den = jnp.sum(p, axis=-1, keepdims=True) + jnp.exp(sink - m)
    o = _hdot(p, v) / den
    for g in groups:
        for j, h in enumerate(group_heads(g)):
            o_ref[:, head_cols(h)] = o[g, j * CHUNK:(j + 1) * CHUNK].astype(BF16)


def _attn_prompt(sinks, h_p, h_m):
    kcol = COL_K // D_SWA_KV
    vcol = COL_V // D_SWA_KV
    qcol = COL_Q // D_SWA_Q

    def prev(n, colblk):
        return lambda p: (p - jnp.where(p % CHUNKS_PER_SEQ >= n, n, 0), colblk)

    kv = lambda imap: pl.BlockSpec((CHUNK, D_SWA_KV), imap)
    return pl.pallas_call(
        functools.partial(_attn_body, prompt=True),
        grid=(N_PROMPT_ROWS // CHUNK,),
        in_specs=[
            pl.BlockSpec(memory_space=pltpu.SMEM),
            pl.BlockSpec((CHUNK, D_SWA_Q), lambda p: (p, qcol)),
            pl.BlockSpec((N_META, D_SWA_KV), lambda p: (0, kcol)),
            pl.BlockSpec((N_META, D_SWA_KV), lambda p: (0, vcol)),
            kv(prev(2, kcol)), kv(prev(2, vcol)),
            kv(prev(1, kcol)), kv(prev(1, vcol)),
            kv(lambda p: (p, kcol)), kv(lambda p: (p, vcol)),
        ],
        out_specs=pl.BlockSpec((CHUNK, D_SWA_Q), lambda p: (p, 0)),
        out_shape=jax.ShapeDtypeStruct((N_PROMPT_ROWS, D_SWA_Q), BF16),
        compiler_params=pltpu.CompilerParams(dimension_semantics=("parallel",)),
        name="swa_prompt",
    )(sinks, h_p, h_m, h_m, h_p, h_p, h_p, h_p, h_p, h_p)


def _attn_sample(sinks, h_s, meta_k, meta_v, win_k, win_v):
    kcol = COL_K // D_SWA_KV
    vcol = COL_V // D_SWA_KV
    qcol = COL_Q // D_SWA_Q
    kv = lambda imap: pl.BlockSpec((CHUNK, D_SWA_KV), imap)
    return pl.pallas_call(
        functools.partial(_attn_body, prompt=False),
        grid=(DEC_BATCH,),
        in_specs=[
            pl.BlockSpec(memory_space=pltpu.SMEM),
            pl.BlockSpec((CHUNK, D_SWA_Q), lambda s: (s, qcol)),
            pl.BlockSpec((N_META, D_SWA_KV), lambda s: (s, 0)),
            pl.BlockSpec((N_META, D_SWA_KV), lambda s: (s, 0)),
            kv(lambda s: (2 * s, 0)), kv(lambda s: (2 * s, 0)),
            kv(lambda s: (2 * s + 1, 0)), kv(lambda s: (2 * s + 1, 0)),
            kv(lambda s: (s, kcol)), kv(lambda s: (s, vcol)),
        ],
        out_specs=pl.BlockSpec((CHUNK, D_SWA_Q), lambda s: (s, 0)),
        out_shape=jax.ShapeDtypeStruct((N_SAMPLE_ROWS, D_SWA_Q), BF16),
        compiler_params=pltpu.CompilerParams(dimension_semantics=("parallel",)),
        name="swa_sample",
    )(sinks, h_s, meta_k, meta_v, win_k, win_v, win_k, win_v, h_s, h_s)


def _unit_lower_inverse(a, n):
    r = lax.broadcasted_iota(jnp.int32, (n, n), 0)
    c = lax.broadcasted_iota(jnp.int32, (n, n), 1)
    eye = (r == c).astype(F32)
    a0 = jnp.where((r >> 3) == (c >> 3), a, 0.0)
    a2 = _hdot(a0, a0)
    a4 = _hdot(a2, a2)
    x = _hdot(_hdot(eye - a0, eye + a2), eye + a4)
    shift = 3
    while (1 << shift) < n:
        pair = jnp.logical_and((r >> (shift + 1)) == (c >> (shift + 1)), (r >> shift) != (c >> shift))
        ak = jnp.where(pair, a, 0.0)
        x = x - _hdot(x, _hdot(ak, x))
        shift += 1
    return x


def _gdn_body(x_ref, z_ref, t_ref, w_ref, alog_ref, dt_ref, nw_ref, s0_ref, b0_ref, o_ref, sout_ref, s_scr,
              xe_scr, *, C, n_chunks):
    c = pl.program_id(1)

    @pl.when(c == 0)
    def _():
        s_scr[...] = s0_ref[0]
        xe_scr[0:CARRY, :] = b0_ref[0]

    @pl.when(c > 0)
    def _():
        xe_scr[0:CARRY, :] = xe_scr[C:C + CARRY, :]

    xe_scr[CARRY:CARRY + C, :] = x_ref[...]

    t = t_ref[...]
    beta = jax.nn.sigmoid(t)
    ta = t + dt_ref[...]
    softplus = jnp.maximum(ta, 0.0) + jnp.log(1.0 + jnp.exp(-jnp.abs(ta)))
    g = -jnp.exp(alog_ref[...]) * softplus
    r = lax.broadcasted_iota(jnp.int32, (C, C), 0)
    cc = lax.broadcasted_iota(jnp.int32, (C, C), 1)
    incl = r >= cc
    strict = r > cc
    gc = _dot(incl.astype(F32), g, HI)
    gc_t = gc.T
    heads = range(N_HEADS_GDN)
    bh = jnp.stack([beta[:, h:h + 1] for h in heads])
    gcol = jnp.stack([gc[:, N_HEADS_GDN + h:N_HEADS_GDN + h + 1] for h in heads])
    grow = jnp.stack([gc_t[N_HEADS_GDN + h:N_HEADS_GDN + h + 1, :] for h in heads])
    glast = gcol[:, C - 1:C, :]
    decay = jnp.where(incl, jnp.exp(jnp.where(incl, gcol - grow, 0.0)), 0.0)
    e_g = jnp.exp(gcol)

    first = CARRY - (CONV_WIDTH - 1)
    acc = w_ref[0:1, :] * xe_scr[first:first + C, :]
    for j in range(1, CONV_WIDTH):
        acc = acc + w_ref[j:j + 1, :] * xe_scr[first + j:first + j + C, :]
    y = _silu(acc)
    split = lambda base: jnp.stack([y[:, base + h * DK_GDN:base + (h + 1) * DK_GDN] for h in heads])
    q, k, v = split(0), split(D_GDN), split(2 * D_GDN)
    q = q * lax.rsqrt(jnp.sum(q * q, axis=-1, keepdims=True) + RMS_EPS) * (DK_GDN ** -0.5)
    k = k * lax.rsqrt(jnp.sum(k * k, axis=-1, keepdims=True) + RMS_EPS)
    kb = k * bh

    kq = _hdot_nt(jnp.concatenate([kb, q], axis=1), k)
    a = jnp.where(strict, kq[:, :C] * decay, 0.0)
    qk = kq[:, C:] * decay
    t_inv = _unit_lower_inverse(a, C)

    s_prev = s_scr[...]
    ws = _hdot(jnp.concatenate([kb * e_g, q * e_g], axis=1), s_prev)
    v_new = _hdot(t_inv, v * bh - ws[:, :C])
    o = ws[:, C:] + _hdot(qk, v_new)
    s_scr[...] = s_prev * jnp.exp(glast) + _hdot_tn(k * jnp.exp(glast - gcol), v_new)

    o = o * lax.rsqrt(jnp.mean(o * o, axis=-1, keepdims=True) + RMS_EPS) * nw_ref[...]
    for h in heads:
        hs = slice(h * DV_GDN, (h + 1) * DV_GDN)
        o_ref[:, hs] = (o[h] * _silu(z_ref[:, hs])).astype(BF16)

    @pl.when(c == n_chunks - 1)
    def _():
        sout_ref[0] = s_scr[...]


def _gdn(h_src, tail_src, conv_w, alog_row, dt_row, nw_row, s0, buf0, *, n_seq, n_chunks, C, shared_init):
    init = (lambda s: 0) if shared_init else (lambda s: s)
    vec = pl.BlockSpec((1, TAIL_W), lambda s, c: (0, 0))
    return pl.pallas_call(
        functools.partial(_gdn_body, C=C, n_chunks=n_chunks),
        grid=(n_seq, n_chunks),
        in_specs=[
            pl.BlockSpec((C, D_CONV), lambda s, c: (s * n_chunks + c, COL_CONV // D_CONV)),
            pl.BlockSpec((C, D_GDN), lambda s, c: (s * n_chunks + c, COL_Z // D_GDN)),
            pl.BlockSpec((C, TAIL_W), lambda s, c: (s * n_chunks + c, 0)),
            pl.BlockSpec((CONV_WIDTH, D_CONV), lambda s, c: (0, 0)),
            vec, vec, vec,
            pl.BlockSpec((1, N_HEADS_GDN, DK_GDN, DV_GDN), lambda s, c: (init(s), 0, 0, 0)),
            pl.BlockSpec((1, CARRY, D_CONV), lambda s, c: (init(s), 0, 0)),
        ],
        out_specs=[
            pl.BlockSpec((C, D_GDN), lambda s, c: (s * n_chunks + c, 0)),
            pl.BlockSpec((1, N_HEADS_GDN, DK_GDN, DV_GDN), lambda s, c: (s, 0, 0, 0)),
        ],
        out_shape=[
            jax.ShapeDtypeStruct((n_seq * n_chunks * C, D_GDN), BF16),
            jax.ShapeDtypeStruct((n_seq, N_HEADS_GDN, DK_GDN, DV_GDN), F32),
        ],
        scratch_shapes=[
            pltpu.VMEM((N_HEADS_GDN, DK_GDN, DV_GDN), F32),
            pltpu.VMEM((CARRY + C, D_CONV), F32),
        ],
        compiler_params=pltpu.CompilerParams(dimension_semantics=("parallel", "arbitrary")),
        name="gdn",
    )(h_src, h_src, tail_src, conv_w, alog_row, dt_row, nw_row, s0, buf0)


def _out_body(a_ref, b_ref, w_ref, x_ref, g_ref, beta_ref, o_ref):
    mixed = _dot(a_ref[...], w_ref[0:D_SWA_Q, :]) + _dot(b_ref[...], w_ref[D_SWA_Q:, :])
    o_ref[...] = _layernorm_rows(ALPHA * x_ref[...] + mixed, g_ref[...], beta_ref[...])


def _out_proj(o_swa, o_gdn, w_out, x1, g, b, *, tm):
    rows = x1.shape[0]
    return pl.pallas_call(
        _out_body,
        grid=(rows // tm,),
        in_specs=[
            pl.BlockSpec((tm, D_SWA_Q), lambda i: (i, 0)),
            pl.BlockSpec((tm, D_GDN), lambda i: (i, 0)),
            pl.BlockSpec((D_MODEL, D_MODEL), lambda i: (0, 0), pipeline_mode=pl.Buffered(1)),
            pl.BlockSpec((tm, D_MODEL), lambda i: (i, 0)),
            pl.BlockSpec((1, D_MODEL), lambda i: (0, 0)),
            pl.BlockSpec((1, D_MODEL), lambda i: (0, 0)),
        ],
        out_specs=pl.BlockSpec((tm, D_MODEL), lambda i: (i, 0)),
        out_shape=jax.ShapeDtypeStruct((rows, D_MODEL), F32),
        compiler_params=pltpu.CompilerParams(
            dimension_semantics=("parallel",), vmem_limit_bytes=VMEM_LIMIT),
        name="proj_out",
    )(o_swa, o_gdn, w_out, x1, g, b)


def _rope_tables(pos):
    half = HEAD_DIM // 2
    inv = ROPE_THETA ** (-jnp.arange(half, dtype=F32) / half)
    ang = pos.astype(F32)[:, None] * inv[None, :]
    cos = jnp.cos(ang)
    sin = jnp.sin(ang)
    return jnp.concatenate([cos, cos], axis=1), jnp.concatenate([-sin, sin], axis=1)


def kernel(x_prompt, x_sample, cache_meta_k, cache_meta_v, cache_win_k, cache_win_v, state_conv, state_gdn,
           meta_tokens, ln_g, ln_b, ffn_w_gate, ffn_w_up, ffn_w_down, w_in, w_out, attn_sinks, conv_w,
           gdn_a_log, gdn_dt_bias, gdn_norm_w):
    l = 0
    wg, wu, wd = (w.astype(BF16) for w in (ffn_w_gate, ffn_w_up, ffn_w_down))
    w_main = w_in.astype(BF16)
    w_tail = jnp.pad(w_in[l, :, D_MAIN:], ((0, 0), (0, TAIL_W - 2 * N_HEADS_GDN))).astype(BF16)
    wo = w_out[l].astype(BF16)
    g1, g2, g3 = (ln_g[l, i][None, :] for i in range(3))
    b1, b2, b3 = (ln_b[l, i][None, :] for i in range(3))
    pad_tail = lambda v, off: jnp.pad(v.astype(F32), (off, TAIL_W - off - N_HEADS_GDN))[None, :]
    alog_row = pad_tail(gdn_a_log[l], N_HEADS_GDN)
    dt_row = pad_tail(gdn_dt_bias[l], N_HEADS_GDN)
    nw_row = gdn_norm_w[l].astype(F32)[None, :]
    cw = conv_w[l]
    sinks = attn_sinks[l].astype(F32)

    xp = x_prompt.reshape(N_PROMPT_ROWS, D_MODEL)
    xs = x_sample.reshape(N_SAMPLE_ROWS, D_MODEL)
    xm = meta_tokens.astype(F32)
    cos_p, sin_p = _rope_tables(jnp.tile(N_META + jnp.arange(SEQ, dtype=jnp.int32), BATCH))
    cos_s, sin_s = _rope_tables(jnp.tile(N_META + PAST_LEN + jnp.arange(DEC_SEQ, dtype=jnp.int32), DEC_BATCH))
    cos_m, sin_m = _rope_tables(jnp.arange(N_META, dtype=jnp.int32))

    ffn1 = functools.partial(_ffn, wg=wg, wu=wu, wd=wd, g=g1, b=b1, which=0, emit_bf16=True)
    x1_p, x1b_p = ffn1(xp, tm=512)
    x1_s, x1b_s = ffn1(xs, tm=512)
    _, x1b_m = ffn1(xm, tm=N_META)
    h_p, t_p = _proj(x1b_p, w_main, w_tail, cos_p, sin_p, tm=1024)
    h_s, t_s = _proj(x1b_s, w_main, w_tail, cos_s, sin_s, tm=1024)
    h_m, t_m = _proj(x1b_m, w_main, w_tail, cos_m, sin_m, tm=N_META)

    o_swa_p = _attn_prompt(sinks, h_p, h_m)
    o_swa_s = _attn_sample(
        sinks, h_s,
        cache_meta_k[l].reshape(DEC_BATCH * N_META, D_SWA_KV), cache_meta_v[l].reshape(DEC_BATCH * N_META, D_SWA_KV),
        cache_win_k[l].reshape(DEC_BATCH * WINDOW, D_SWA_KV), cache_win_v[l].reshape(DEC_BATCH * WINDOW, D_SWA_KV))

    zero_s = jnp.zeros((1, N_HEADS_GDN, DK_GDN, DV_GDN), F32)
    zero_buf = jnp.zeros((1, CARRY, D_CONV), F32)
    gdn = functools.partial(_gdn, conv_w=cw, alog_row=alog_row, dt_row=dt_row, nw_row=nw_row)
    _, s_meta = gdn(h_m, t_m, s0=zero_s, buf0=zero_buf, n_seq=1, n_chunks=1, C=N_META, shared_init=True)
    buf_meta = h_m[None, N_META - CARRY:, COL_CONV:COL_CONV + D_CONV]
    o_gdn_p, s_prompt = gdn(h_p, t_p, s0=s_meta, buf0=buf_meta, n_seq=BATCH, n_chunks=CHUNKS_PER_SEQ, C=CHUNK,
                            shared_init=True)
    buf_s = jnp.pad(state_conv[l].astype(F32), ((0, 0), (CARRY - (CONV_WIDTH - 1), 0), (0, 0)))
    o_gdn_s, s_sample = gdn(h_s, t_s, s0=state_gdn[l].astype(F32), buf0=buf_s, n_seq=DEC_BATCH, n_chunks=1,
                            C=CHUNK, shared_init=False)

    x2_p = _out_proj(o_swa_p, o_gdn_p, wo, x1_p, g2, b2, tm=256)
    x2_s = _out_proj(o_swa_s, o_gdn_s, wo, x1_s, g2, b2, tm=256)
    ffn2 = functools.partial(_ffn, wg=wg, wu=wu, wd=wd, g=g3, b=b3, which=1, emit_bf16=False)
    (y_p,) = ffn2(x2_p, tm=512)
    (y_s,) = ffn2(x2_s, tm=512)

    y_prompt = y_p.reshape(BATCH, SEQ, D_MODEL)
    y_sample = y_s.reshape(DEC_BATCH, DEC_SEQ, D_MODEL)
    k_meta = h_m[:, COL_K:COL_K + D_SWA_KV].reshape(N_META, N_KV_SWA, HEAD_DIM)
    v_meta = h_m[:, COL_V:COL_V + D_SWA_KV].reshape(N_META, N_KV_SWA, HEAD_DIM)
    p_meta_k = jnp.broadcast_to(k_meta[None, None], (1, BATCH, N_META, N_KV_SWA, HEAD_DIM))
    p_meta_v = jnp.broadcast_to(v_meta[None, None], (1, BATCH, N_META, N_KV_SWA, HEAD_DIM))
    hp = h_p.reshape(BATCH, SEQ, D_MAIN)
    hs = h_s.reshape(DEC_BATCH, DEC_SEQ, D_MAIN)
    p_win_k = hp[:, SEQ - WINDOW:, COL_K:COL_K + D_SWA_KV].reshape(1, BATCH, WINDOW, N_KV_SWA, HEAD_DIM)
    p_win_v = hp[:, SEQ - WINDOW:, COL_V:COL_V + D_SWA_KV].reshape(1, BATCH, WINDOW, N_KV_SWA, HEAD_DIM)
    p_conv = hp[:, SEQ - (CONV_WIDTH - 1):, COL_CONV:COL_CONV + D_CONV][None]
    p_gdn = s_prompt[None]
    s_win_k = hs[:, :, COL_K:COL_K + D_SWA_KV].reshape(1, DEC_BATCH, DEC_SEQ, N_KV_SWA, HEAD_DIM)
    s_win_v = hs[:, :, COL_V:COL_V + D_SWA_KV].reshape(1, DEC_BATCH, DEC_SEQ, N_KV_SWA, HEAD_DIM)
    s_conv = hs[:, DEC_SEQ - (CONV_WIDTH - 1):, COL_CONV:COL_CONV + D_CONV][None]
    s_gdn = s_sample[None]
    return (y_prompt, y_sample, p_meta_k, p_meta_v, p_win_k, p_win_v, p_conv, p_gdn, s_win_k, s_win_v, s_conv,
            s_gdn)
```

```python
import functools

import jax
import jax.numpy as jnp
from jax import lax
from jax.experimental import pallas as pl
from jax.experimental.pallas import tpu as pltpu

D_MODEL = 4096
BATCH = 4
SEQ = 2048
DEC_BATCH = 16
DEC_SEQ = 64
PAST_LEN = 2048
CHUNK = 64
N_META = 16
WINDOW = 128
HEAD_DIM = 128
N_HEADS_SWA = 16
N_KV_SWA = 4
GQA_GROUP = 4
N_HEADS_GDN = 16
DK_GDN = 128
DV_GDN = 128
CONV_WIDTH = 4
D_SWA_Q = N_HEADS_SWA * HEAD_DIM
D_SWA_KV = N_KV_SWA * HEAD_DIM
D_GDN = N_HEADS_GDN * DK_GDN
D_CONV = 3 * D_GDN
D_FF = 11008
ROPE_THETA = 10000.0
LN_EPS = 1e-5
RMS_EPS = 1e-6
ALPHA = 2.0 ** 0.25

N_PROMPT_ROWS = BATCH * SEQ
N_SAMPLE_ROWS = DEC_BATCH * DEC_SEQ
CHUNKS_PER_SEQ = SEQ // CHUNK
META_ROW0 = N_SAMPLE_ROWS
SM_TILE = 528
N_SM_ROWS = 2 * SM_TILE

COL_CONV = 0
COL_Z = D_CONV
COL_Q = COL_Z + D_GDN
COL_K = COL_Q + D_SWA_Q
COL_V = COL_K + D_SWA_KV
D_MAIN = COL_V + D_SWA_KV
TAIL_W = 128

FFN_TF = 256
PROJ_TN = 1024
CARRY = 8

VMEM_LIMIT = 60 * 1024 * 1024

BF16 = jnp.bfloat16
F32 = jnp.float32
HI = lax.Precision.HIGHEST


def _dot(a, b, precision=None):
    return jnp.dot(a, b, preferred_element_type=F32, precision=precision)


def _dot_nt(a, b):
    return lax.dot_general(a, b, (((1,), (1,)), ((), ())), preferred_element_type=F32)


def _dot_tn(a, b):
    return lax.dot_general(a, b, (((0,), (0,)), ((), ())), preferred_element_type=F32)


def _head_dot(a, b, lhs_contract, rhs_contract):
    dims = (((lhs_contract,), (rhs_contract,)), ((0,), (0,)))
    return lax.dot_general(a.astype(BF16), b.astype(BF16), dims, preferred_element_type=F32)


_hdot = functools.partial(_head_dot, lhs_contract=2, rhs_contract=1)
_hdot_nt = functools.partial(_head_dot, lhs_contract=2, rhs_contract=2)
_hdot_tn = functools.partial(_head_dot, lhs_contract=1, rhs_contract=1)


def _layernorm_rows(y, g, b):
    mu = jnp.mean(y, axis=-1, keepdims=True)
    d = y - mu
    var = jnp.mean(d * d, axis=-1, keepdims=True)
    return d * lax.rsqrt(var + LN_EPS) * g + b


def _silu(x):
    return x * jax.nn.sigmoid(x)


def _ffn_body(x_ref, wg_ref, wu_ref, wd_ref, g_ref, b_ref, *rest, emit_bf16, n_cast):
    cast_in, rest = rest[:n_cast], rest[n_cast:]
    o_ref, rest = rest[0], rest[1:]
    if emit_bf16:
        ob_ref, rest = rest[0], rest[1:]
    cast_out, (xb_ref,) = rest[:n_cast], rest[n_cast:]
    f = pl.program_id(1)

    for src, dst in zip(cast_in, cast_out):
        dst[...] = src[...].astype(BF16)

    @pl.when(f == 0)
    def _():
        xb_ref[...] = x_ref[...].astype(BF16)
        o_ref[...] = jnp.zeros_like(o_ref)

    xb = xb_ref[...]
    hidden = _silu(_dot(xb, wg_ref[...])) * _dot(xb, wu_ref[...])
    o_ref[...] += _dot(hidden.astype(BF16), wd_ref[...])

    @pl.when(f == pl.num_programs(1) - 1)
    def _():
        y = _layernorm_rows(ALPHA * x_ref[...] + 0.5 * o_ref[...], g_ref[...], b_ref[...])
        o_ref[...] = y
        if emit_bf16:
            ob_ref[...] = y.astype(BF16)


def _ffn(x, wg, wu, wd, g, b, *, tm, emit_bf16, cast_next=None):
    rows = x.shape[0]
    grid = (rows // tm, D_FF // FFN_TF)
    row_spec = lambda: pl.BlockSpec((tm, D_MODEL), lambda i, f: (i, 0))
    in_specs = [
        pl.BlockSpec((tm, D_MODEL), lambda i, f: (i, 0), pipeline_mode=pl.Buffered(1)),
        pl.BlockSpec((D_MODEL, FFN_TF), lambda i, f: (0, f)),
        pl.BlockSpec((D_MODEL, FFN_TF), lambda i, f: (0, f)),
        pl.BlockSpec((FFN_TF, D_MODEL), lambda i, f: (f, 0)),
        pl.BlockSpec((1, D_MODEL), lambda i, f: (0, 0)),
        pl.BlockSpec((1, D_MODEL), lambda i, f: (0, 0)),
    ]
    args = [x, wg, wu, wd, g, b]
    out_shape = [jax.ShapeDtypeStruct((rows, D_MODEL), F32)]
    out_specs = [row_spec()]
    if emit_bf16:
        out_shape.append(jax.ShapeDtypeStruct((rows, D_MODEL), BF16))
        out_specs.append(row_spec())
    n_cast = 0
    if cast_next is not None:
        assert grid[0] * FFN_TF == D_MODEL
        n_cast = 3
        up_map = lambda i, f: (0, 1, i, f)
        down_map = lambda i, f: (0, 1, f, i)
        sq = (None, None, FFN_TF, FFN_TF)
        in_specs += [pl.BlockSpec(sq, up_map), pl.BlockSpec(sq, up_map), pl.BlockSpec(sq, down_map)]
        args += list(cast_next)
        out_shape += [jax.ShapeDtypeStruct((D_MODEL, D_FF), BF16), jax.ShapeDtypeStruct((D_MODEL, D_FF), BF16),
                      jax.ShapeDtypeStruct((D_FF, D_MODEL), BF16)]
        out_specs += [pl.BlockSpec((FFN_TF, FFN_TF), lambda i, f: (i, f)),
                      pl.BlockSpec((FFN_TF, FFN_TF), lambda i, f: (i, f)),
                      pl.BlockSpec((FFN_TF, FFN_TF), lambda i, f: (f, i))]
    return pl.pallas_call(
        functools.partial(_ffn_body, emit_bf16=emit_bf16, n_cast=n_cast),
        grid=grid,
        in_specs=in_specs,
        out_specs=out_specs,
        out_shape=out_shape,
        scratch_shapes=[pltpu.VMEM((tm, D_MODEL), BF16)],
        compiler_params=pltpu.CompilerParams(
            dimension_semantics=("parallel", "arbitrary"), vmem_limit_bytes=VMEM_LIMIT),
        name="ffn",
    )(*args)


def _proj_body(x_ref, w_ref, wt_ref, cos_ref, sin_ref, o_ref, t_ref):
    j = pl.program_id(1)
    acc = _dot(x_ref[...], w_ref[...])
    n_full, n_part = divmod((D_SWA_Q + D_SWA_KV) // HEAD_DIM, PROJ_TN // HEAD_DIM)

    def store(n_rope_heads):
        cos = cos_ref[...]
        sin = sin_ref[...]
        for s in range(PROJ_TN // HEAD_DIM):
            cols = slice(s * HEAD_DIM, (s + 1) * HEAD_DIM)
            blk = acc[:, cols]
            o_ref[:, cols] = blk * cos + pltpu.roll(blk, HEAD_DIM // 2, 1) * sin if s < n_rope_heads else blk

    pl.when(j < n_full)(lambda: store(PROJ_TN // HEAD_DIM))
    pl.when(j == n_full)(lambda: store(n_part))

    @pl.when(j > n_full)
    def _():
        o_ref[...] = acc

    @pl.when(j == 0)
    def _():
        t_ref[...] = _dot(x_ref[...], wt_ref[...])


def _proj(xb, w_in, w_tail, cos, sin, *, tm):
    rows = xb.shape[0]
    n_swa = (D_SWA_Q + 2 * D_SWA_KV) // PROJ_TN

    def dest(i, j):
        return i, jnp.where(j < n_swa, j + COL_Q // PROJ_TN, j - n_swa)

    return pl.pallas_call(
        _proj_body,
        grid=(rows // tm, D_MAIN // PROJ_TN),
        in_specs=[
            pl.BlockSpec((tm, D_MODEL), lambda i, j: (i, 0)),
            pl.BlockSpec((None, D_MODEL, PROJ_TN), lambda i, j: (0, 0, j)),
            pl.BlockSpec((D_MODEL, TAIL_W), lambda i, j: (0, 0)),
            pl.BlockSpec((tm, HEAD_DIM), lambda i, j: (i, 0)),
            pl.BlockSpec((tm, HEAD_DIM), lambda i, j: (i, 0)),
        ],
        out_specs=[
            pl.BlockSpec((tm, PROJ_TN), dest),
            pl.BlockSpec((tm, TAIL_W), lambda i, j: (i, 0)),
        ],
        out_shape=[
            jax.ShapeDtypeStruct((rows, D_MAIN), F32),
            jax.ShapeDtypeStruct((rows, TAIL_W), F32),
        ],
        compiler_params=pltpu.CompilerParams(
            dimension_semantics=("parallel", "arbitrary"), vmem_limit_bytes=VMEM_LIMIT),
        name="proj_in",
    )(xb, w_in, w_tail, cos, sin)


N_KEYS = N_META + 3 * CHUNK


def _attn_body(sink_ref, q_ref, km_ref, vm_ref, k2_ref, v2_ref, k1_ref, v1_ref, k0_ref, v0_ref, o_ref, *,
               prompt):
    col = lax.broadcasted_iota(jnp.int32, (1, N_KEYS), 1)
    if prompt:
        c = pl.program_id(0) % CHUNKS_PER_SEQ
        ok2 = jnp.logical_or(col >= N_META + CHUNK, c >= 2)
        ok1 = jnp.logical_or(jnp.logical_or(col < N_META + CHUNK, col >= N_META + 2 * CHUNK), c >= 1)
        mask = jnp.logical_or(col < N_META, jnp.logical_and(ok2, ok1))
    else:
        mask = col >= 0
    scale = HEAD_DIM ** -0.5
    groups = range(N_KV_SWA)
    head_cols = lambda h: slice(h * HEAD_DIM, (h + 1) * HEAD_DIM)
    group_heads = lambda g: range(g * GQA_GROUP, (g + 1) * GQA_GROUP)
    k = jnp.stack([jnp.concatenate([r[:, head_cols(g)] for r in (km_ref, k2_ref, k1_ref, k0_ref)], axis=0)
                   for g in groups])
    v = jnp.stack([jnp.concatenate([r[:, head_cols(g)] for r in (vm_ref, v2_ref, v1_ref, v0_ref)], axis=0)
                   for g in groups])
    q = jnp.stack([jnp.concatenate([q_ref[:, head_cols(h)] for h in group_heads(g)], axis=0)
                   for g in groups])
    sink = jnp.stack([jnp.concatenate([jnp.full((CHUNK, 1), sink_ref[h], F32) for h in group_heads(g)], axis=0)
                      for g in groups])
    s = _hdot_nt(q, k) * scale
    s = jnp.where(mask, s, -jnp.inf)
    m = jnp.maximum(jnp.max(s, axis=-1, keepdims=True), sink)
    p = jnp.exp(s - m)
    den = jnp.sum(p, axis=-1, keepdims=True) + jnp.exp(sink - m)
    o = _hdot(p, v) / den
    for g in groups:
        for j, h in enumerate(group_heads(g)):
            o_ref[:, head_cols(h)] = o[g, j * CHUNK:(j + 1) * CHUNK].astype(BF16)


def _attn_prompt(sinks, h_p, h_m):
    kcol = COL_K // D_SWA_KV
    vcol = COL_V // D_SWA_KV
    qcol = COL_Q // D_SWA_Q

    def prev(n, colblk):
        return lambda p: (p - jnp.where(p % CHUNKS_PER_SEQ >= n, n, 0), colblk)

    kv = lambda imap: pl.BlockSpec((CHUNK, D_SWA_KV), imap)
    return pl.pallas_call(
        functools.partial(_attn_body, prompt=True),
        grid=(N_PROMPT_ROWS // CHUNK,),
        in_specs=[
            pl.BlockSpec(memory_space=pltpu.SMEM),
            pl.BlockSpec((CHUNK, D_SWA_Q), lambda p: (p, qcol)),
            pl.BlockSpec((N_META, D_SWA_KV), lambda p: (META_ROW0 // N_META, kcol)),
            pl.BlockSpec((N_META, D_SWA_KV), lambda p: (META_ROW0 // N_META, vcol)),
            kv(prev(2, kcol)), kv(prev(2, vcol)),
            kv(prev(1, kcol)), kv(prev(1, vcol)),
            kv(lambda p: (p, kcol)), kv(lambda p: (p, vcol)),
        ],
        out_specs=pl.BlockSpec((CHUNK, D_SWA_Q), lambda p: (p, 0)),
        out_shape=jax.ShapeDtypeStruct((N_PROMPT_ROWS, D_SWA_Q), BF16),
        compiler_params=pltpu.CompilerParams(dimension_semantics=("parallel",)),
        name="swa_prompt",
    )(sinks, h_p, h_m, h_m, h_p, h_p, h_p, h_p, h_p, h_p)


def _attn_sample(sinks, h_s, meta_k, meta_v, win_k, win_v):
    kcol = COL_K // D_SWA_KV
    vcol = COL_V // D_SWA_KV
    qcol = COL_Q // D_SWA_Q
    kv = lambda imap: pl.BlockSpec((CHUNK, D_SWA_KV), imap)
    return pl.pallas_call(
        functools.partial(_attn_body, prompt=False),
        grid=(DEC_BATCH,),
        in_specs=[
            pl.BlockSpec(memory_space=pltpu.SMEM),
            pl.BlockSpec((CHUNK, D_SWA_Q), lambda s: (s, qcol)),
            pl.BlockSpec((N_META, D_SWA_KV), lambda s: (s, 0)),
            pl.BlockSpec((N_META, D_SWA_KV), lambda s: (s, 0)),
            kv(lambda s: (2 * s, 0)), kv(lambda s: (2 * s, 0)),
            kv(lambda s: (2 * s + 1, 0)), kv(lambda s: (2 * s + 1, 0)),
            kv(lambda s: (s, kcol)), kv(lambda s: (s, vcol)),
        ],
        out_specs=pl.BlockSpec((CHUNK, D_SWA_Q), lambda s: (s, 0)),
        out_shape=jax.ShapeDtypeStruct((N_SAMPLE_ROWS, D_SWA_Q), BF16),
        compiler_params=pltpu.CompilerParams(dimension_semantics=("parallel",)),
        name="swa_sample",
    )(sinks, h_s, meta_k, meta_v, win_k, win_v, win_k, win_v, h_s, h_s)


def _unit_lower_inverse(a, n):
    r = lax.broadcasted_iota(jnp.int32, (n, n), 0)
    c = lax.broadcasted_iota(jnp.int32, (n, n), 1)
    eye = (r == c).astype(F32)
    a0 = jnp.where((r >> 3) == (c >> 3), a, 0.0)
    a2 = _hdot(a0, a0)
    a4 = _hdot(a2, a2)
    x = _hdot(_hdot(eye - a0, eye + a2), eye + a4)
    shift = 3
    while (1 << shift) < n:
        pair = jnp.logical_and((r >> (shift + 1)) == (c >> (shift + 1)), (r >> shift) != (c >> shift))
        ak = jnp.where(pair, a, 0.0)
        x = x - _hdot(x, _hdot(ak, x))
        shift += 1
    return x


def _gdn_body(x_ref, z_ref, t_ref, w_ref, alog_ref, dt_ref, nw_ref, s0_ref, b0_ref, o_ref, sout_ref, s_scr,
              xe_scr, *, C, n_chunks):
    c = pl.program_id(1)

    @pl.when(c == 0)
    def _():
        s_scr[...] = s0_ref[0]
        xe_scr[0:CARRY, :] = b0_ref[0]

    @pl.when(c > 0)
    def _():
        xe_scr[0:CARRY, :] = xe_scr[C:C + CARRY, :]

    xe_scr[CARRY:CARRY + C, :] = x_ref[...]

    t = t_ref[...]
    beta = jax.nn.sigmoid(t)
    ta = t + dt_ref[...]
    softplus = jnp.maximum(ta, 0.0) + jnp.log(1.0 + jnp.exp(-jnp.abs(ta)))
    g = -jnp.exp(alog_ref[...]) * softplus
    r = lax.broadcasted_iota(jnp.int32, (C, C), 0)
    cc = lax.broadcasted_iota(jnp.int32, (C, C), 1)
    incl = r >= cc
    strict = r > cc
    gc = _dot(incl.astype(F32), g, HI)
    gc_t = gc.T
    heads = range(N_HEADS_GDN)
    bh = jnp.stack([beta[:, h:h + 1] for h in heads])
    gcol = jnp.stack([gc[:, N_HEADS_GDN + h:N_HEADS_GDN + h + 1] for h in heads])
    grow = jnp.stack([gc_t[N_HEADS_GDN + h:N_HEADS_GDN + h + 1, :] for h in heads])
    glast = gcol[:, C - 1:C, :]
    decay = jnp.where(incl, jnp.exp(jnp.where(incl, gcol - grow, 0.0)), 0.0)
    e_g = jnp.exp(gcol)

    xe = xe_scr[...]
    acc = w_ref[CONV_WIDTH - 1:CONV_WIDTH, :] * xe[CARRY:, :]
    for lag in range(1, CONV_WIDTH):
        tap = CONV_WIDTH - 1 - lag
        acc = acc + w_ref[tap:tap + 1, :] * pltpu.roll(xe, lag, 0)[CARRY:, :]
    y = _silu(acc)
    split = lambda base: jnp.stack([y[:, base + h * DK_GDN:base + (h + 1) * DK_GDN] for h in heads])
    q, k, v = split(0), split(D_GDN), split(2 * D_GDN)
    q = q * lax.rsqrt(jnp.sum(q * q, axis=-1, keepdims=True) + RMS_EPS) * (DK_GDN ** -0.5)
    k = k * lax.rsqrt(jnp.sum(k * k, axis=-1, keepdims=True) + RMS_EPS)
    kb = k * bh

    kq = _hdot_nt(jnp.concatenate([kb, q], axis=1), k)
    a = jnp.where(strict, kq[:, :C] * decay, 0.0)
    qk = kq[:, C:] * decay
    t_inv = _unit_lower_inverse(a, C)

    s_prev = s_scr[...]
    ws = _hdot(jnp.concatenate([kb * e_g, q * e_g], axis=1), s_prev)
    v_new = _hdot(t_inv, v * bh - ws[:, :C])
    o = ws[:, C:] + _hdot(qk, v_new)
    s_scr[...] = s_prev * jnp.exp(glast) + _hdot_tn(k * jnp.exp(glast - gcol), v_new)

    o = o * lax.rsqrt(jnp.mean(o * o, axis=-1, keepdims=True) + RMS_EPS) * nw_ref[...]
    for h in heads:
        hs = slice(h * DV_GDN, (h + 1) * DV_GDN)
        o_ref[:, hs] = (o[h] * _silu(z_ref[:, hs])).astype(BF16)

    @pl.when(c == n_chunks - 1)
    def _():
        sout_ref[0] = s_scr[...]


def _gdn(h_src, tail_src, conv_w, alog_row, dt_row, nw_row, s0, buf0, *, n_seq, n_chunks, C, shared_init,
         first_chunk=0):
    init = (lambda s: 0) if shared_init else (lambda s: s)
    vec = pl.BlockSpec((1, TAIL_W), lambda s, c: (0, 0))
    src = lambda s, c: first_chunk + s * n_chunks + c
    return pl.pallas_call(
        functools.partial(_gdn_body, C=C, n_chunks=n_chunks),
        grid=(n_seq, n_chunks),
        in_specs=[
            pl.BlockSpec((C, D_CONV), lambda s, c: (src(s, c), COL_CONV // D_CONV)),
            pl.BlockSpec((C, D_GDN), lambda s, c: (src(s, c), COL_Z // D_GDN)),
            pl.BlockSpec((C, TAIL_W), lambda s, c: (src(s, c), 0)),
            pl.BlockSpec((CONV_WIDTH, D_CONV), lambda s, c: (0, 0)),
            vec, vec, vec,
            pl.BlockSpec((1, N_HEADS_GDN, DK_GDN, DV_GDN), lambda s, c: (init(s), 0, 0, 0)),
            pl.BlockSpec((1, CARRY, D_CONV), lambda s, c: (init(s), 0, 0)),
        ],
        out_specs=[
            pl.BlockSpec((C, D_GDN), lambda s, c: (s * n_chunks + c, 0)),
            pl.BlockSpec((1, N_HEADS_GDN, DK_GDN, DV_GDN), lambda s, c: (s, 0, 0, 0)),
        ],
        out_shape=[
            jax.ShapeDtypeStruct((n_seq * n_chunks * C, D_GDN), BF16),
            jax.ShapeDtypeStruct((n_seq, N_HEADS_GDN, DK_GDN, DV_GDN), F32),
        ],
        scratch_shapes=[
            pltpu.VMEM((N_HEADS_GDN, DK_GDN, DV_GDN), F32),
            pltpu.VMEM((CARRY + C, D_CONV), F32),
        ],
        compiler_params=pltpu.CompilerParams(dimension_semantics=("parallel", "arbitrary")),
        name="gdn",
    )(h_src, h_src, tail_src, conv_w, alog_row, dt_row, nw_row, s0, buf0)


def _out_body(a_ref, b_ref, w_ref, x_ref, g_ref, beta_ref, o_ref):
    mixed = _dot(a_ref[...], w_ref[0:D_SWA_Q, :]) + _dot(b_ref[...], w_ref[D_SWA_Q:, :])
    o_ref[...] = _layernorm_rows(ALPHA * x_ref[...] + mixed, g_ref[...], beta_ref[...])


def _out_proj(o_swa, o_gdn, w_out, x1, g, b, *, tm):
    rows = o_swa.shape[0]
    return pl.pallas_call(
        _out_body,
        grid=(rows // tm,),
        in_specs=[
            pl.BlockSpec((tm, D_SWA_Q), lambda i: (i, 0)),
            pl.BlockSpec((tm, D_GDN), lambda i: (i, 0)),
            pl.BlockSpec((D_MODEL, D_MODEL), lambda i: (0, 0), pipeline_mode=pl.Buffered(1)),
            pl.BlockSpec((tm, D_MODEL), lambda i: (i, 0)),
            pl.BlockSpec((1, D_MODEL), lambda i: (0, 0)),
            pl.BlockSpec((1, D_MODEL), lambda i: (0, 0)),
        ],
        out_specs=pl.BlockSpec((tm, D_MODEL), lambda i: (i, 0)),
        out_shape=jax.ShapeDtypeStruct((rows, D_MODEL), F32),
        compiler_params=pltpu.CompilerParams(
            dimension_semantics=("parallel",), vmem_limit_bytes=VMEM_LIMIT),
        name="proj_out",
    )(o_swa, o_gdn, w_out, x1, g, b)


def _rope_tables(pos):
    half = HEAD_DIM // 2
    inv = ROPE_THETA ** (-jnp.arange(half, dtype=F32) / half)
    ang = pos.astype(F32)[:, None] * inv[None, :]
    cos = jnp.cos(ang)
    sin = jnp.sin(ang)
    return jnp.concatenate([cos, cos], axis=1), jnp.concatenate([-sin, sin], axis=1)


def kernel(x_prompt, x_sample, cache_meta_k, cache_meta_v, cache_win_k, cache_win_v, state_conv, state_gdn,
           meta_tokens, ln_g, ln_b, ffn_w_gate, ffn_w_up, ffn_w_down, w_in, w_out, attn_sinks, conv_w,
           gdn_a_log, gdn_dt_bias, gdn_norm_w):
    l = 0
    wg1, wu1, wd1 = (w[l, 0].astype(BF16) for w in (ffn_w_gate, ffn_w_up, ffn_w_down))
    w_main = w_in.astype(BF16)
    w_tail = jnp.pad(w_in[l, :, D_MAIN:], ((0, 0), (0, TAIL_W - 2 * N_HEADS_GDN))).astype(BF16)
    wo = w_out[l].astype(BF16)
    g1, g2, g3 = (ln_g[l, i][None, :] for i in range(3))
    b1, b2, b3 = (ln_b[l, i][None, :] for i in range(3))
    pad_tail = lambda v, off: jnp.pad(v.astype(F32), (off, TAIL_W - off - N_HEADS_GDN))[None, :]
    alog_row = pad_tail(gdn_a_log[l], N_HEADS_GDN)
    dt_row = pad_tail(gdn_dt_bias[l], N_HEADS_GDN)
    nw_row = gdn_norm_w[l].astype(F32)[None, :]
    cw = conv_w[l]
    sinks = attn_sinks[l].astype(F32)

    n_pad = N_SM_ROWS - N_SAMPLE_ROWS - N_META
    xp = x_prompt.reshape(N_PROMPT_ROWS, D_MODEL)
    xs = jnp.concatenate([x_sample.reshape(N_SAMPLE_ROWS, D_MODEL), meta_tokens.astype(F32),
                          jnp.zeros((n_pad, D_MODEL), F32)], axis=0)
    cos_p, sin_p = _rope_tables(jnp.tile(N_META + jnp.arange(SEQ, dtype=jnp.int32), BATCH))
    cos_s, sin_s = _rope_tables(jnp.concatenate([
        jnp.tile(N_META + PAST_LEN + jnp.arange(DEC_SEQ, dtype=jnp.int32), DEC_BATCH),
        jnp.arange(N_META, dtype=jnp.int32), jnp.zeros((n_pad,), jnp.int32)]))

    ffn1 = functools.partial(_ffn, wg=wg1, wu=wu1, wd=wd1, g=g1, b=b1, emit_bf16=True)
    x1_s, x1b_s = ffn1(xs, tm=SM_TILE)
    x1_p, x1b_p, wg2, wu2, wd2 = ffn1(xp, tm=512, cast_next=(ffn_w_gate, ffn_w_up, ffn_w_down))
    h_p, t_p = _proj(x1b_p, w_main, w_tail, cos_p, sin_p, tm=1024)
    h_s, t_s = _proj(x1b_s, w_main, w_tail, cos_s, sin_s, tm=N_SM_ROWS)
    meta_rows = slice(META_ROW0, META_ROW0 + N_META)

    o_swa_p = _attn_prompt(sinks, h_p, h_s)
    o_swa_s = _attn_sample(
        sinks, h_s,
        cache_meta_k[l].reshape(DEC_BATCH * N_META, D_SWA_KV), cache_meta_v[l].reshape(DEC_BATCH * N_META, D_SWA_KV),
        cache_win_k[l].reshape(DEC_BATCH * WINDOW, D_SWA_KV), cache_win_v[l].reshape(DEC_BATCH * WINDOW, D_SWA_KV))

    zero_s = jnp.zeros((1, N_HEADS_GDN, DK_GDN, DV_GDN), F32)
    zero_buf = jnp.zeros((1, CARRY, D_CONV), F32)
    gdn = functools.partial(_gdn, conv_w=cw, alog_row=alog_row, dt_row=dt_row, nw_row=nw_row)
    _, s_meta = gdn(h_s, t_s, s0=zero_s, buf0=zero_buf, n_seq=1, n_chunks=1, C=N_META, shared_init=True,
                    first_chunk=META_ROW0 // N_META)
    buf_meta = h_s[None, META_ROW0 + N_META - CARRY:META_ROW0 + N_META, COL_CONV:COL_CONV + D_CONV]
    o_gdn_p, s_prompt = gdn(h_p, t_p, s0=s_meta, buf0=buf_meta, n_seq=BATCH, n_chunks=CHUNKS_PER_SEQ, C=CHUNK,
                            shared_init=True)
    buf_s = jnp.pad(state_conv[l].astype(F32), ((0, 0), (CARRY - (CONV_WIDTH - 1), 0), (0, 0)))
    o_gdn_s, s_sample = gdn(h_s, t_s, s0=state_gdn[l].astype(F32), buf0=buf_s, n_seq=DEC_BATCH, n_chunks=1,
                            C=CHUNK, shared_init=False)

    x2_p = _out_proj(o_swa_p, o_gdn_p, wo, x1_p, g2, b2, tm=256)
    x2_s = _out_proj(o_swa_s, o_gdn_s, wo, x1_s, g2, b2, tm=256)
    ffn2 = functools.partial(_ffn, wg=wg2, wu=wu2, wd=wd2, g=g3, b=b3, emit_bf16=False)
    (y_p,) = ffn2(x2_p, tm=512)
    (y_s,) = ffn2(x2_s, tm=512)

    y_prompt = y_p.reshape(BATCH, SEQ, D_MODEL)
    y_sample = y_s.reshape(DEC_BATCH, DEC_SEQ, D_MODEL)
    k_meta = h_s[meta_rows, COL_K:COL_K + D_SWA_KV].reshape(N_META, N_KV_SWA, HEAD_DIM)
    v_meta = h_s[meta_rows, COL_V:COL_V + D_SWA_KV].reshape(N_META, N_KV_SWA, HEAD_DIM)
    p_meta_k = jnp.broadcast_to(k_meta[None, None], (1, BATCH, N_META, N_KV_SWA, HEAD_DIM))
    p_meta_v = jnp.broadcast_to(v_meta[None, None], (1, BATCH, N_META, N_KV_SWA, HEAD_DIM))
    hp = h_p.reshape(BATCH, SEQ, D_MAIN)
    hs = h_s[:N_SAMPLE_ROWS].reshape(DEC_BATCH, DEC_SEQ, D_MAIN)
    p_win_k = hp[:, SEQ - WINDOW:, COL_K:COL_K + D_SWA_KV].reshape(1, BATCH, WINDOW, N_KV_SWA, HEAD_DIM)
    p_win_v = hp[:, SEQ - WINDOW:, COL_V:COL_V + D_SWA_KV].reshape(1, BATCH, WINDOW, N_KV_SWA, HEAD_DIM)
    p_conv = hp[:, SEQ - (CONV_WIDTH - 1):, COL_CONV:COL_CONV + D_CONV][None]
    p_gdn = s_prompt[None]
    s_win_k = hs[:, :, COL_K:COL_K + D_SWA_KV].reshape(1, DEC_BATCH, DEC_SEQ, N_KV_SWA, HEAD_DIM)
    s_win_v = hs[:, :, COL_V:COL_V + D_SWA_KV].reshape(1, DEC_BATCH, DEC_SEQ, N_KV_SWA, HEAD_DIM)
    s_conv = hs[:, DEC_SEQ - (CONV_WIDTH - 1):, COL_CONV:COL_CONV + D_CONV][None]
    s_gdn = s_sample[None]
    return (y_prompt, y_sample, p_meta_k, p_meta_v, p_win_k, p_win_v, p_conv, p_gdn, s_win_k, s_win_v, s_conv,
            s_gdn)
```

```python
import functools

import jax
import jax.numpy as jnp
from jax import lax
from jax.experimental import pallas as pl
from jax.experimental.pallas import tpu as pltpu

D_MODEL = 4096
BATCH = 4
SEQ = 2048
DEC_BATCH = 16
DEC_SEQ = 64
PAST_LEN = 2048
CHUNK = 64
N_META = 16
WINDOW = 128
HEAD_DIM = 128
N_HEADS_SWA = 16
N_KV_SWA = 4
GQA_GROUP = 4
N_HEADS_GDN = 16
DK_GDN = 128
DV_GDN = 128
CONV_WIDTH = 4
D_SWA_Q = N_HEADS_SWA * HEAD_DIM
D_SWA_KV = N_KV_SWA * HEAD_DIM
D_GDN = N_HEADS_GDN * DK_GDN
D_CONV = 3 * D_GDN
D_FF = 11008
ROPE_THETA = 10000.0
LN_EPS = 1e-5
RMS_EPS = 1e-6
ALPHA = 2.0 ** 0.25

N_PROMPT_ROWS = BATCH * SEQ
N_SAMPLE_ROWS = DEC_BATCH * DEC_SEQ
CHUNKS_PER_SEQ = SEQ // CHUNK
META_ROW0 = N_SAMPLE_ROWS
SM_TILE = 528
N_SM_ROWS = 2 * SM_TILE

COL_CONV = 0
COL_Z = D_CONV
COL_Q = COL_Z + D_GDN
COL_K = COL_Q + D_SWA_Q
COL_V = COL_K + D_SWA_KV
D_MAIN = COL_V + D_SWA_KV
TAIL_W = 128

FFN_TF = 256
PROJ_TN = 512
CARRY = 8

VMEM_LIMIT = 60 * 1024 * 1024

BF16 = jnp.bfloat16
F32 = jnp.float32
HI = lax.Precision.HIGHEST


def _dot(a, b, precision=None):
    return jnp.dot(a, b, preferred_element_type=F32, precision=precision)


def _dot_nt(a, b):
    return lax.dot_general(a, b, (((1,), (1,)), ((), ())), preferred_element_type=F32)


def _dot_tn(a, b):
    return lax.dot_general(a, b, (((0,), (0,)), ((), ())), preferred_element_type=F32)


def _head_dot(a, b, lhs_contract, rhs_contract):
    dims = (((lhs_contract,), (rhs_contract,)), ((0,), (0,)))
    return lax.dot_general(a.astype(BF16), b.astype(BF16), dims, preferred_element_type=F32)


_hdot = functools.partial(_head_dot, lhs_contract=2, rhs_contract=1)
_hdot_nt = functools.partial(_head_dot, lhs_contract=2, rhs_contract=2)
_hdot_tn = functools.partial(_head_dot, lhs_contract=1, rhs_contract=1)


def _layernorm_rows(y, g, b):
    mu = jnp.mean(y, axis=-1, keepdims=True)
    d = y - mu
    var = jnp.mean(d * d, axis=-1, keepdims=True)
    return d * lax.rsqrt(var + LN_EPS) * g + b


def _silu(x):
    return x * jax.nn.sigmoid(x)


def _ffn_body(x_ref, wg_ref, wu_ref, wd_ref, g_ref, b_ref, *rest, emit_bf16, n_cast):
    cast_in, rest = rest[:n_cast], rest[n_cast:]
    o_ref, rest = rest[0], rest[1:]
    if emit_bf16:
        ob_ref, rest = rest[0], rest[1:]
    cast_out, (xb_ref,) = rest[:n_cast], rest[n_cast:]
    f = pl.program_id(1)

    for src, dst in zip(cast_in, cast_out):
        dst[...] = src[...].astype(BF16)

    @pl.when(f == 0)
    def _():
        xb_ref[...] = x_ref[...].astype(BF16)
        o_ref[...] = jnp.zeros_like(o_ref)

    xb = xb_ref[...]
    hidden = _silu(_dot(xb, wg_ref[...])) * _dot(xb, wu_ref[...])
    o_ref[...] += _dot(hidden.astype(BF16), wd_ref[...])

    @pl.when(f == pl.num_programs(1) - 1)
    def _():
        y = _layernorm_rows(ALPHA * x_ref[...] + 0.5 * o_ref[...], g_ref[...], b_ref[...])
        o_ref[...] = y
        if emit_bf16:
            ob_ref[...] = y.astype(BF16)


def _ffn(x, wg, wu, wd, g, b, *, tm, emit_bf16, cast_next=None):
    rows = x.shape[0]
    grid = (rows // tm, D_FF // FFN_TF)
    row_spec = lambda: pl.BlockSpec((tm, D_MODEL), lambda i, f: (i, 0))
    in_specs = [
        pl.BlockSpec((tm, D_MODEL), lambda i, f: (i, 0), pipeline_mode=pl.Buffered(1)),
        pl.BlockSpec((D_MODEL, FFN_TF), lambda i, f: (0, f)),
        pl.BlockSpec((D_MODEL, FFN_TF), lambda i, f: (0, f)),
        pl.BlockSpec((FFN_TF, D_MODEL), lambda i, f: (f, 0)),
        pl.BlockSpec((1, D_MODEL), lambda i, f: (0, 0)),
        pl.BlockSpec((1, D_MODEL), lambda i, f: (0, 0)),
    ]
    args = [x, wg, wu, wd, g, b]
    out_shape = [jax.ShapeDtypeStruct((rows, D_MODEL), F32)]
    out_specs = [row_spec()]
    if emit_bf16:
        out_shape.append(jax.ShapeDtypeStruct((rows, D_MODEL), BF16))
        out_specs.append(row_spec())
    n_cast = 0
    if cast_next is not None:
        assert grid[0] * FFN_TF == D_MODEL
        n_cast = 3
        up_map = lambda i, f: (0, 1, i, f)
        down_map = lambda i, f: (0, 1, f, i)
        sq = (None, None, FFN_TF, FFN_TF)
        in_specs += [pl.BlockSpec(sq, up_map), pl.BlockSpec(sq, up_map), pl.BlockSpec(sq, down_map)]
        args += list(cast_next)
        out_shape += [jax.ShapeDtypeStruct((D_MODEL, D_FF), BF16), jax.ShapeDtypeStruct((D_MODEL, D_FF), BF16),
                      jax.ShapeDtypeStruct((D_FF, D_MODEL), BF16)]
        out_specs += [pl.BlockSpec((FFN_TF, FFN_TF), lambda i, f: (i, f)),
                      pl.BlockSpec((FFN_TF, FFN_TF), lambda i, f: (i, f)),
                      pl.BlockSpec((FFN_TF, FFN_TF), lambda i, f: (f, i))]
    return pl.pallas_call(
        functools.partial(_ffn_body, emit_bf16=emit_bf16, n_cast=n_cast),
        grid=grid,
        in_specs=in_specs,
        out_specs=out_specs,
        out_shape=out_shape,
        scratch_shapes=[pltpu.VMEM((tm, D_MODEL), BF16)],
        compiler_params=pltpu.CompilerParams(
            dimension_semantics=("parallel", "arbitrary"), vmem_limit_bytes=VMEM_LIMIT),
        name="ffn",
    )(*args)


def _proj_body(x_ref, w_ref, wt_ref, cos_ref, sin_ref, o_ref, t_ref):
    j = pl.program_id(1)
    acc = _dot(x_ref[...], w_ref[...].astype(BF16))
    n_full, n_part = divmod((D_SWA_Q + D_SWA_KV) // HEAD_DIM, PROJ_TN // HEAD_DIM)

    def store(n_rope_heads):
        cos = cos_ref[...]
        sin = sin_ref[...]
        for s in range(PROJ_TN // HEAD_DIM):
            cols = slice(s * HEAD_DIM, (s + 1) * HEAD_DIM)
            blk = acc[:, cols]
            o_ref[:, cols] = blk * cos + pltpu.roll(blk, HEAD_DIM // 2, 1) * sin if s < n_rope_heads else blk

    pl.when(j < n_full)(lambda: store(PROJ_TN // HEAD_DIM))
    pl.when(j == n_full)(lambda: store(n_part))

    @pl.when(j > n_full)
    def _():
        o_ref[...] = acc

    @pl.when(j == 0)
    def _():
        t_ref[...] = _dot(x_ref[...], wt_ref[...])


def _proj(xb, w_in, w_tail, cos, sin, *, tm):
    rows = xb.shape[0]
    n_swa = (D_SWA_Q + 2 * D_SWA_KV) // PROJ_TN

    def dest(i, j):
        return i, jnp.where(j < n_swa, j + COL_Q // PROJ_TN, j - n_swa)

    return pl.pallas_call(
        _proj_body,
        grid=(rows // tm, D_MAIN // PROJ_TN),
        in_specs=[
            pl.BlockSpec((tm, D_MODEL), lambda i, j: (i, 0)),
            pl.BlockSpec((None, D_MODEL, PROJ_TN), lambda i, j: (0, 0, j)),
            pl.BlockSpec((D_MODEL, TAIL_W), lambda i, j: (0, 0)),
            pl.BlockSpec((tm, HEAD_DIM), lambda i, j: (i, 0)),
            pl.BlockSpec((tm, HEAD_DIM), lambda i, j: (i, 0)),
        ],
        out_specs=[
            pl.BlockSpec((tm, PROJ_TN), dest),
            pl.BlockSpec((tm, TAIL_W), lambda i, j: (i, 0)),
        ],
        out_shape=[
            jax.ShapeDtypeStruct((rows, D_MAIN), F32),
            jax.ShapeDtypeStruct((rows, TAIL_W), F32),
        ],
        compiler_params=pltpu.CompilerParams(
            dimension_semantics=("parallel", "arbitrary"), vmem_limit_bytes=VMEM_LIMIT),
        name="proj_in",
    )(xb, w_in, w_tail, cos, sin)


N_KEYS = N_META + 3 * CHUNK


def _stack_heads(ref, rows, heads):
    return jnp.concatenate([ref[rows, h * HEAD_DIM:(h + 1) * HEAD_DIM] for h in heads], axis=0)


def _group_heads(g):
    return range(g * GQA_GROUP, (g + 1) * GQA_GROUP)


def _sink_rows(sink_ref, g):
    return jnp.concatenate([jnp.full((CHUNK, 1), sink_ref[h], F32) for h in _group_heads(g)], axis=0)


def _softmax_pv(q, k, v, sink, mask):
    s = _hdot_nt(q, k) * (HEAD_DIM ** -0.5)
    if mask is not None:
        s = jnp.where(mask, s, -jnp.inf)
    m = jnp.maximum(jnp.max(s, axis=-1, keepdims=True), sink)
    p = jnp.exp(s - m)
    den = jnp.sum(p, axis=-1, keepdims=True) + jnp.exp(sink - m)
    return _hdot(p, v) / den


def _attn_sample_body(sink_ref, q_ref, km_ref, vm_ref, k2_ref, v2_ref, k1_ref, v1_ref, k0_ref, v0_ref, o_ref):
    groups = range(N_KV_SWA)
    every = slice(None)
    k = jnp.stack([jnp.concatenate([_stack_heads(r, every, [g]) for r in (km_ref, k2_ref, k1_ref, k0_ref)], axis=0)
                   for g in groups])
    v = jnp.stack([jnp.concatenate([_stack_heads(r, every, [g]) for r in (vm_ref, v2_ref, v1_ref, v0_ref)], axis=0)
                   for g in groups])
    q = jnp.stack([_stack_heads(q_ref, every, _group_heads(g)) for g in groups])
    sink = jnp.stack([_sink_rows(sink_ref, g) for g in groups])
    o = _softmax_pv(q, k, v, sink, None)
    for g in groups:
        for j, h in enumerate(_group_heads(g)):
            o_ref[:, h * HEAD_DIM:(h + 1) * HEAD_DIM] = o[g, j * CHUNK:(j + 1) * CHUNK].astype(BF16)


def _attn_pair_body(sink_ref, q_ref, km_ref, vm_ref, kp_ref, vp_ref, kc_ref, vc_ref, o_ref):
    has_prev = pl.program_id(0) % (CHUNKS_PER_SEQ // 2) >= 1
    col = lax.broadcasted_iota(jnp.int32, (1, N_KEYS), 1)
    groups = range(N_KV_SWA)
    key_rows = ((slice(0, 2 * CHUNK), slice(0, CHUNK)), (slice(CHUNK, 2 * CHUNK), slice(0, 2 * CHUNK)))
    n_prev = (2 * CHUNK, CHUNK)
    q, k, v, sink, mask = [], [], [], [], []
    for sub in range(2):
        prev_rows, cur_rows = key_rows[sub]
        visible = jnp.logical_or(jnp.logical_or(col < N_META, col >= N_META + n_prev[sub]), has_prev)
        for g in groups:
            k.append(jnp.concatenate([_stack_heads(km_ref, slice(None), [g]), _stack_heads(kp_ref, prev_rows, [g]),
                                      _stack_heads(kc_ref, cur_rows, [g])], axis=0))
            v.append(jnp.concatenate([_stack_heads(vm_ref, slice(None), [g]), _stack_heads(vp_ref, prev_rows, [g]),
                                      _stack_heads(vc_ref, cur_rows, [g])], axis=0))
            q.append(_stack_heads(q_ref, slice(sub * CHUNK, (sub + 1) * CHUNK), _group_heads(g)))
            sink.append(_sink_rows(sink_ref, g))
            mask.append(visible)
    o = _softmax_pv(jnp.stack(q), jnp.stack(k), jnp.stack(v), jnp.stack(sink), jnp.stack(mask))
    for sub in range(2):
        for g in groups:
            for j, h in enumerate(_group_heads(g)):
                o_ref[sub * CHUNK:(sub + 1) * CHUNK, h * HEAD_DIM:(h + 1) * HEAD_DIM] = (
                    o[sub * N_KV_SWA + g, j * CHUNK:(j + 1) * CHUNK].astype(BF16))


def _attn_prompt(sinks, h_p, h_m):
    kcol = COL_K // D_SWA_KV
    vcol = COL_V // D_SWA_KV
    qcol = COL_Q // D_SWA_Q
    pairs_per_seq = CHUNKS_PER_SEQ // 2
    prev = lambda colblk: (lambda p: (p - jnp.where(p % pairs_per_seq >= 1, 1, 0), colblk))
    kv = lambda imap: pl.BlockSpec((2 * CHUNK, D_SWA_KV), imap)
    meta = lambda colblk: pl.BlockSpec((N_META, D_SWA_KV), lambda p: (META_ROW0 // N_META, colblk))
    return pl.pallas_call(
        _attn_pair_body,
        grid=(N_PROMPT_ROWS // (2 * CHUNK),),
        in_specs=[
            pl.BlockSpec(memory_space=pltpu.SMEM),
            pl.BlockSpec((2 * CHUNK, D_SWA_Q), lambda p: (p, qcol)),
            meta(kcol), meta(vcol),
            kv(prev(kcol)), kv(prev(vcol)),
            kv(lambda p: (p, kcol)), kv(lambda p: (p, vcol)),
        ],
        out_specs=pl.BlockSpec((2 * CHUNK, D_SWA_Q), lambda p: (p, 0)),
        out_shape=jax.ShapeDtypeStruct((N_PROMPT_ROWS, D_SWA_Q), BF16),
        compiler_params=pltpu.CompilerParams(dimension_semantics=("parallel",)),
        name="swa_prompt",
    )(sinks, h_p, h_m, h_m, h_p, h_p, h_p, h_p)


def _attn_sample(sinks, h_s, meta_k, meta_v, win_k, win_v):
    kcol = COL_K // D_SWA_KV
    vcol = COL_V // D_SWA_KV
    qcol = COL_Q // D_SWA_Q
    kv = lambda imap: pl.BlockSpec((CHUNK, D_SWA_KV), imap)
    return pl.pallas_call(
        _attn_sample_body,
        grid=(DEC_BATCH,),
        in_specs=[
            pl.BlockSpec(memory_space=pltpu.SMEM),
            pl.BlockSpec((CHUNK, D_SWA_Q), lambda s: (s, qcol)),
            pl.BlockSpec((N_META, D_SWA_KV), lambda s: (s, 0)),
            pl.BlockSpec((N_META, D_SWA_KV), lambda s: (s, 0)),
            kv(lambda s: (2 * s, 0)), kv(lambda s: (2 * s, 0)),
            kv(lambda s: (2 * s + 1, 0)), kv(lambda s: (2 * s + 1, 0)),
            kv(lambda s: (s, kcol)), kv(lambda s: (s, vcol)),
        ],
        out_specs=pl.BlockSpec((CHUNK, D_SWA_Q), lambda s: (s, 0)),
        out_shape=jax.ShapeDtypeStruct((N_SAMPLE_ROWS, D_SWA_Q), BF16),
        compiler_params=pltpu.CompilerParams(dimension_semantics=("parallel",)),
        name="swa_sample",
    )(sinks, h_s, meta_k, meta_v, win_k, win_v, win_k, win_v, h_s, h_s)


def _unit_lower_inverse(a, n):
    r = lax.broadcasted_iota(jnp.int32, (n, n), 0)
    c = lax.broadcasted_iota(jnp.int32, (n, n), 1)
    eye = (r == c).astype(F32)
    a0 = jnp.where((r >> 3) == (c >> 3), a, 0.0)
    a2 = _hdot(a0, a0)
    a4 = _hdot(a2, a2)
    x = _hdot(_hdot(eye - a0, eye + a2), eye + a4)
    shift = 3
    while (1 << shift) < n:
        pair = jnp.logical_and((r >> (shift + 1)) == (c >> (shift + 1)), (r >> shift) != (c >> shift))
        ak = jnp.where(pair, a, 0.0)
        x = x - _hdot(x, _hdot(ak, x))
        shift += 1
    return x


def _gdn_body(x_ref, z_ref, t_ref, w_ref, alog_ref, dt_ref, nw_ref, s0_ref, b0_ref, o_ref, sout_ref, s_scr,
              xe_scr, *, C, n_chunks):
    c = pl.program_id(1)

    @pl.when(c == 0)
    def _():
        s_scr[...] = s0_ref[0]
        xe_scr[0:CARRY, :] = b0_ref[0]

    @pl.when(c > 0)
    def _():
        xe_scr[0:CARRY, :] = xe_scr[C:C + CARRY, :]

    xe_scr[CARRY:CARRY + C, :] = x_ref[...]

    t = t_ref[...]
    beta = jax.nn.sigmoid(t)
    ta = t + dt_ref[...]
    softplus = jnp.maximum(ta, 0.0) + jnp.log(1.0 + jnp.exp(-jnp.abs(ta)))
    g = -jnp.exp(alog_ref[...]) * softplus
    r = lax.broadcasted_iota(jnp.int32, (C, C), 0)
    cc = lax.broadcasted_iota(jnp.int32, (C, C), 1)
    incl = r >= cc
    strict = r > cc
    gc = _dot(incl.astype(F32), g, HI)
    gc_t = gc.T
    heads = range(N_HEADS_GDN)
    bh = jnp.stack([beta[:, h:h + 1] for h in heads])
    gcol = jnp.stack([gc[:, N_HEADS_GDN + h:N_HEADS_GDN + h + 1] for h in heads])
    grow = jnp.stack([gc_t[N_HEADS_GDN + h:N_HEADS_GDN + h + 1, :] for h in heads])
    glast = gcol[:, C - 1:C, :]
    decay = jnp.where(incl, jnp.exp(jnp.where(incl, gcol - grow, 0.0)), 0.0)
    e_g = jnp.exp(gcol)

    xe = xe_scr[...]
    acc = w_ref[CONV_WIDTH - 1:CONV_WIDTH, :] * xe[CARRY:, :]
    for lag in range(1, CONV_WIDTH):
        tap = CONV_WIDTH - 1 - lag
        acc = acc + w_ref[tap:tap + 1, :] * pltpu.roll(xe, lag, 0)[CARRY:, :]
    y = _silu(acc)
    split = lambda base: jnp.stack([y[:, base + h * DK_GDN:base + (h + 1) * DK_GDN] for h in heads])
    q, k, v = split(0), split(D_GDN), split(2 * D_GDN)
    q = q * lax.rsqrt(jnp.sum(q * q, axis=-1, keepdims=True) + RMS_EPS) * (DK_GDN ** -0.5)
    k = k * lax.rsqrt(jnp.sum(k * k, axis=-1, keepdims=True) + RMS_EPS)
    kb = k * bh

    kq = _hdot_nt(jnp.concatenate([kb, q], axis=1), k)
    a = jnp.where(strict, kq[:, :C] * decay, 0.0)
    qk = kq[:, C:] * decay
    t_inv = _unit_lower_inverse(a, C)

    s_prev = s_scr[...]
    ws = _hdot(jnp.concatenate([kb * e_g, q * e_g], axis=1), s_prev)
    v_new = _hdot(t_inv, v * bh - ws[:, :C])
    o = ws[:, C:] + _hdot(qk, v_new)
    s_scr[...] = s_prev * jnp.exp(glast) + _hdot_tn(k * jnp.exp(glast - gcol), v_new)

    o = o * lax.rsqrt(jnp.mean(o * o, axis=-1, keepdims=True) + RMS_EPS) * nw_ref[...]
    for h in heads:
        hs = slice(h * DV_GDN, (h + 1) * DV_GDN)
        o_ref[:, hs] = (o[h] * _silu(z_ref[:, hs])).astype(BF16)

    @pl.when(c == n_chunks - 1)
    def _():
        sout_ref[0] = s_scr[...]


def _gdn(h_src, tail_src, conv_w, alog_row, dt_row, nw_row, s0, buf0, *, n_seq, n_chunks, C, shared_init,
         first_chunk=0):
    init = (lambda s: 0) if shared_init else (lambda s: s)
    vec = pl.BlockSpec((1, TAIL_W), lambda s, c: (0, 0))
    src = lambda s, c: first_chunk + s * n_chunks + c
    return pl.pallas_call(
        functools.partial(_gdn_body, C=C, n_chunks=n_chunks),
        grid=(n_seq, n_chunks),
        in_specs=[
            pl.BlockSpec((C, D_CONV), lambda s, c: (src(s, c), COL_CONV // D_CONV)),
            pl.BlockSpec((C, D_GDN), lambda s, c: (src(s, c), COL_Z // D_GDN)),
            pl.BlockSpec((C, TAIL_W), lambda s, c: (src(s, c), 0)),
            pl.BlockSpec((CONV_WIDTH, D_CONV), lambda s, c: (0, 0)),
            vec, vec, vec,
            pl.BlockSpec((1, N_HEADS_GDN, DK_GDN, DV_GDN), lambda s, c: (init(s), 0, 0, 0)),
            pl.BlockSpec((1, CARRY, D_CONV), lambda s, c: (init(s), 0, 0)),
        ],
        out_specs=[
            pl.BlockSpec((C, D_GDN), lambda s, c: (s * n_chunks + c, 0)),
            pl.BlockSpec((1, N_HEADS_GDN, DK_GDN, DV_GDN), lambda s, c: (s, 0, 0, 0)),
        ],
        out_shape=[
            jax.ShapeDtypeStruct((n_seq * n_chunks * C, D_GDN), BF16),
            jax.ShapeDtypeStruct((n_seq, N_HEADS_GDN, DK_GDN, DV_GDN), F32),
        ],
        scratch_shapes=[
            pltpu.VMEM((N_HEADS_GDN, DK_GDN, DV_GDN), F32),
            pltpu.VMEM((CARRY + C, D_CONV), F32),
        ],
        compiler_params=pltpu.CompilerParams(dimension_semantics=("parallel", "arbitrary")),
        name="gdn",
    )(h_src, h_src, tail_src, conv_w, alog_row, dt_row, nw_row, s0, buf0)


def _out_body(a_ref, b_ref, w_ref, x_ref, g_ref, beta_ref, o_ref):
    mixed = _dot(a_ref[...], w_ref[0:D_SWA_Q, :]) + _dot(b_ref[...], w_ref[D_SWA_Q:, :])
    o_ref[...] = _layernorm_rows(ALPHA * x_ref[...] + mixed, g_ref[...], beta_ref[...])


def _out_proj(o_swa, o_gdn, w_out, x1, g, b, *, tm):
    rows = o_swa.shape[0]
    return pl.pallas_call(
        _out_body,
        grid=(rows // tm,),
        in_specs=[
            pl.BlockSpec((tm, D_SWA_Q), lambda i: (i, 0)),
            pl.BlockSpec((tm, D_GDN), lambda i: (i, 0)),
            pl.BlockSpec((D_MODEL, D_MODEL), lambda i: (0, 0), pipeline_mode=pl.Buffered(1)),
            pl.BlockSpec((tm, D_MODEL), lambda i: (i, 0)),
            pl.BlockSpec((1, D_MODEL), lambda i: (0, 0)),
            pl.BlockSpec((1, D_MODEL), lambda i: (0, 0)),
        ],
        out_specs=pl.BlockSpec((tm, D_MODEL), lambda i: (i, 0)),
        out_shape=jax.ShapeDtypeStruct((rows, D_MODEL), F32),
        compiler_params=pltpu.CompilerParams(
            dimension_semantics=("parallel",), vmem_limit_bytes=VMEM_LIMIT),
        name="proj_out",
    )(o_swa, o_gdn, w_out, x1, g, b)


def _rope_tables(pos):
    half = HEAD_DIM // 2
    inv = ROPE_THETA ** (-jnp.arange(half, dtype=F32) / half)
    ang = pos.astype(F32)[:, None] * inv[None, :]
    cos = jnp.cos(ang)
    sin = jnp.sin(ang)
    return jnp.concatenate([cos, cos], axis=1), jnp.concatenate([-sin, sin], axis=1)


def kernel(x_prompt, x_sample, cache_meta_k, cache_meta_v, cache_win_k, cache_win_v, state_conv, state_gdn,
           meta_tokens, ln_g, ln_b, ffn_w_gate, ffn_w_up, ffn_w_down, w_in, w_out, attn_sinks, conv_w,
           gdn_a_log, gdn_dt_bias, gdn_norm_w):
    l = 0
    wg1, wu1, wd1 = (w[l, 0].astype(BF16) for w in (ffn_w_gate, ffn_w_up, ffn_w_down))
    w_main = w_in
    w_tail = jnp.pad(w_in[l, :, D_MAIN:], ((0, 0), (0, TAIL_W - 2 * N_HEADS_GDN))).astype(BF16)
    wo = w_out[l].astype(BF16)
    g1, g2, g3 = (ln_g[l, i][None, :] for i in range(3))
    b1, b2, b3 = (ln_b[l, i][None, :] for i in range(3))
    pad_tail = lambda v, off: jnp.pad(v.astype(F32), (off, TAIL_W - off - N_HEADS_GDN))[None, :]
    alog_row = pad_tail(gdn_a_log[l], N_HEADS_GDN)
    dt_row = pad_tail(gdn_dt_bias[l], N_HEADS_GDN)
    nw_row = gdn_norm_w[l].astype(F32)[None, :]
    cw = conv_w[l]
    sinks = attn_sinks[l].astype(F32)

    n_pad = N_SM_ROWS - N_SAMPLE_ROWS - N_META
    xp = x_prompt.reshape(N_PROMPT_ROWS, D_MODEL)
    xs = jnp.concatenate([x_sample.reshape(N_SAMPLE_ROWS, D_MODEL), meta_tokens.astype(F32),
                          jnp.zeros((n_pad, D_MODEL), F32)], axis=0)
    cos_p, sin_p = _rope_tables(jnp.tile(N_META + jnp.arange(SEQ, dtype=jnp.int32), BATCH))
    cos_s, sin_s = _rope_tables(jnp.concatenate([
        jnp.tile(N_META + PAST_LEN + jnp.arange(DEC_SEQ, dtype=jnp.int32), DEC_BATCH),
        jnp.arange(N_META, dtype=jnp.int32), jnp.zeros((n_pad,), jnp.int32)]))

    ffn1 = functools.partial(_ffn, wg=wg1, wu=wu1, wd=wd1, g=g1, b=b1, emit_bf16=True)
    x1_s, x1b_s = ffn1(xs, tm=SM_TILE)
    x1_p, x1b_p, wg2, wu2, wd2 = ffn1(xp, tm=512, cast_next=(ffn_w_gate, ffn_w_up, ffn_w_down))
    h_p, t_p = _proj(x1b_p, w_main, w_tail, cos_p, sin_p, tm=1024)
    h_s, t_s = _proj(x1b_s, w_main, w_tail, cos_s, sin_s, tm=N_SM_ROWS)
    meta_rows = slice(META_ROW0, META_ROW0 + N_META)

    o_swa_p = _attn_prompt(sinks, h_p, h_s)
    o_swa_s = _attn_sample(
        sinks, h_s,
        cache_meta_k[l].reshape(DEC_BATCH * N_META, D_SWA_KV), cache_meta_v[l].reshape(DEC_BATCH * N_META, D_SWA_KV),
        cache_win_k[l].reshape(DEC_BATCH * WINDOW, D_SWA_KV), cache_win_v[l].reshape(DEC_BATCH * WINDOW, D_SWA_KV))

    zero_s = jnp.zeros((1, N_HEADS_GDN, DK_GDN, DV_GDN), F32)
    zero_buf = jnp.zeros((1, CARRY, D_CONV), F32)
    gdn = functools.partial(_gdn, conv_w=cw, alog_row=alog_row, dt_row=dt_row, nw_row=nw_row)
    _, s_meta = gdn(h_s, t_s, s0=zero_s, buf0=zero_buf, n_seq=1, n_chunks=1, C=N_META, shared_init=True,
                    first_chunk=META_ROW0 // N_META)
    buf_meta = h_s[None, META_ROW0 + N_META - CARRY:META_ROW0 + N_META, COL_CONV:COL_CONV + D_CONV]
    o_gdn_p, s_prompt = gdn(h_p, t_p, s0=s_meta, buf0=buf_meta, n_seq=BATCH, n_chunks=CHUNKS_PER_SEQ, C=CHUNK,
                            shared_init=True)
    buf_s = jnp.pad(state_conv[l].astype(F32), ((0, 0), (CARRY - (CONV_WIDTH - 1), 0), (0, 0)))
    o_gdn_s, s_sample = gdn(h_s, t_s, s0=state_gdn[l].astype(F32), buf0=buf_s, n_seq=DEC_BATCH, n_chunks=1,
                            C=CHUNK, shared_init=False)

    x2_p = _out_proj(o_swa_p, o_gdn_p, wo, x1_p, g2, b2, tm=256)
    x2_s = _out_proj(o_swa_s, o_gdn_s, wo, x1_s, g2, b2, tm=256)
    ffn2 = functools.partial(_ffn, wg=wg2, wu=wu2, wd=wd2, g=g3, b=b3, emit_bf16=False)
    (y_p,) = ffn2(x2_p, tm=512)
    (y_s,) = ffn2(x2_s, tm=512)

    y_prompt = y_p.reshape(BATCH, SEQ, D_MODEL)
    y_sample = y_s.reshape(DEC_BATCH, DEC_SEQ, D_MODEL)
    k_meta = h_s[meta_rows, COL_K:COL_K + D_SWA_KV].reshape(N_META, N_KV_SWA, HEAD_DIM)
    v_meta = h_s[meta_rows, COL_V:COL_V + D_SWA_KV].reshape(N_META, N_KV_SWA, HEAD_DIM)
    p_meta_k = jnp.broadcast_to(k_meta[None, None], (1, BATCH, N_META, N_KV_SWA, HEAD_DIM))
    p_meta_v = jnp.broadcast_to(v_meta[None, None], (1, BATCH, N_META, N_KV_SWA, HEAD_DIM))
    hp = h_p.reshape(BATCH, SEQ, D_MAIN)
    sample_cols = lambda c0, width: h_s[:N_SAMPLE_ROWS, c0:c0 + width].reshape(DEC_BATCH, DEC_SEQ, width)
    p_win_k = hp[:, SEQ - WINDOW:, COL_K:COL_K + D_SWA_KV].reshape(1, BATCH, WINDOW, N_KV_SWA, HEAD_DIM)
    p_win_v = hp[:, SEQ - WINDOW:, COL_V:COL_V + D_SWA_KV].reshape(1, BATCH, WINDOW, N_KV_SWA, HEAD_DIM)
    p_conv = hp[:, SEQ - (CONV_WIDTH - 1):, COL_CONV:COL_CONV + D_CONV][None]
    p_gdn = s_prompt[None]
    s_win_k = sample_cols(COL_K, D_SWA_KV).reshape(1, DEC_BATCH, DEC_SEQ, N_KV_SWA, HEAD_DIM)
    s_win_v = sample_cols(COL_V, D_SWA_KV).reshape(1, DEC_BATCH, DEC_SEQ, N_KV_SWA, HEAD_DIM)
    s_conv = sample_cols(COL_CONV, D_CONV)[:, DEC_SEQ - (CONV_WIDTH - 1):][None]
    s_gdn = s_sample[None]
    return (y_prompt, y_sample, p_meta_k, p_meta_v, p_win_k, p_win_v, p_conv, p_gdn, s_win_k, s_win_v, s_conv,
            s_gdn)
```

```python
import functools

import jax
import jax.numpy as jnp
from jax import lax
from jax.experimental import pallas as pl
from jax.experimental.pallas import tpu as pltpu

D_MODEL = 4096
BATCH = 4
SEQ = 2048
DEC_BATCH = 16
DEC_SEQ = 64
PAST_LEN = 2048
CHUNK = 64
N_META = 16
WINDOW = 128
HEAD_DIM = 128
N_HEADS_SWA = 16
N_KV_SWA = 4
GQA_GROUP = 4
N_HEADS_GDN = 16
DK_GDN = 128
DV_GDN = 128
CONV_WIDTH = 4
D_SWA_Q = N_HEADS_SWA * HEAD_DIM
D_SWA_KV = N_KV_SWA * HEAD_DIM
D_GDN = N_HEADS_GDN * DK_GDN
D_CONV = 3 * D_GDN
D_FF = 11008
ROPE_THETA = 10000.0
LN_EPS = 1e-5
RMS_EPS = 1e-6
ALPHA = 2.0 ** 0.25

N_PROMPT_ROWS = BATCH * SEQ
N_SAMPLE_ROWS = DEC_BATCH * DEC_SEQ
CHUNKS_PER_SEQ = SEQ // CHUNK
META_ROW0 = N_SAMPLE_ROWS
SM_TILE = 528
N_SM_ROWS = 2 * SM_TILE

COL_CONV = 0
COL_Z = D_CONV
COL_Q = COL_Z + D_GDN
COL_K = COL_Q + D_SWA_Q
COL_V = COL_K + D_SWA_KV
D_MAIN = COL_V + D_SWA_KV
TAIL_W = 128

FFN_TF = 256
PROJ_TN = 512
CARRY = 8

VMEM_LIMIT = 60 * 1024 * 1024

BF16 = jnp.bfloat16
F32 = jnp.float32
HI = lax.Precision.HIGHEST


def _dot(a, b, precision=None):
    return jnp.dot(a, b, preferred_element_type=F32, precision=precision)


def _dot_nt(a, b):
    return lax.dot_general(a, b, (((1,), (1,)), ((), ())), preferred_element_type=F32)


def _dot_tn(a, b):
    return lax.dot_general(a, b, (((0,), (0,)), ((), ())), preferred_element_type=F32)


def _head_dot(a, b, lhs_contract, rhs_contract):
    dims = (((lhs_contract,), (rhs_contract,)), ((0,), (0,)))
    return lax.dot_general(a.astype(BF16), b.astype(BF16), dims, preferred_element_type=F32)


_hdot = functools.partial(_head_dot, lhs_contract=2, rhs_contract=1)
_hdot_nt = functools.partial(_head_dot, lhs_contract=2, rhs_contract=2)
_hdot_tn = functools.partial(_head_dot, lhs_contract=1, rhs_contract=1)


def _layernorm_rows(y, g, b):
    mu = jnp.mean(y, axis=-1, keepdims=True)
    d = y - mu
    var = jnp.mean(d * d, axis=-1, keepdims=True)
    return d * lax.rsqrt(var + LN_EPS) * g + b


def _silu(x):
    return x * jax.nn.sigmoid(x)


def _ffn_body(x_ref, wg_ref, wu_ref, wd_ref, g_ref, b_ref, *rest, emit_bf16, n_cast):
    cast_in, rest = rest[:n_cast], rest[n_cast:]
    o_ref, rest = rest[0], rest[1:]
    if emit_bf16:
        ob_ref, rest = rest[0], rest[1:]
    cast_out, (xb_ref, h_scr) = rest[:n_cast], rest[n_cast:]
    f = pl.program_id(1)
    n_f = pl.num_programs(1) - 1
    slot = f % 2

    for src, dst in zip(cast_in, cast_out):
        dst[...] = src[...].astype(BF16)

    def hidden_tile(dst_slot):
        xb = xb_ref[...]
        h_scr[dst_slot] = (_silu(_dot(xb, wg_ref[...])) * _dot(xb, wu_ref[...])).astype(BF16)

    def down_project(src_slot):
        o_ref[...] += _dot(h_scr[src_slot], wd_ref[...])

    @pl.when(f == 0)
    def _():
        xb_ref[...] = x_ref[...].astype(BF16)
        o_ref[...] = jnp.zeros_like(o_ref)
        hidden_tile(0)

    @pl.when(jnp.logical_and(f > 0, f < n_f))
    def _():
        down_project(1 - slot)
        hidden_tile(slot)

    @pl.when(f == n_f)
    def _():
        down_project(1 - slot)
        y = _layernorm_rows(ALPHA * x_ref[...] + 0.5 * o_ref[...], g_ref[...], b_ref[...])
        o_ref[...] = y
        if emit_bf16:
            ob_ref[...] = y.astype(BF16)


def _ffn(x, wg, wu, wd, g, b, *, tm, emit_bf16, cast_next=None):
    rows = x.shape[0]
    n_f = D_FF // FFN_TF
    grid = (rows // tm, n_f + 1)
    tile_a = lambda f: jnp.minimum(f, n_f - 1)
    tile_b = lambda f: jnp.maximum(f - 1, 0)
    row_spec = lambda: pl.BlockSpec((tm, D_MODEL), lambda i, f: (i, 0))
    in_specs = [
        pl.BlockSpec((tm, D_MODEL), lambda i, f: (i, 0), pipeline_mode=pl.Buffered(1)),
        pl.BlockSpec((D_MODEL, FFN_TF), lambda i, f: (0, tile_a(f))),
        pl.BlockSpec((D_MODEL, FFN_TF), lambda i, f: (0, tile_a(f))),
        pl.BlockSpec((FFN_TF, D_MODEL), lambda i, f: (tile_b(f), 0)),
        pl.BlockSpec((1, D_MODEL), lambda i, f: (0, 0)),
        pl.BlockSpec((1, D_MODEL), lambda i, f: (0, 0)),
    ]
    args = [x, wg, wu, wd, g, b]
    out_shape = [jax.ShapeDtypeStruct((rows, D_MODEL), F32)]
    out_specs = [row_spec()]
    if emit_bf16:
        out_shape.append(jax.ShapeDtypeStruct((rows, D_MODEL), BF16))
        out_specs.append(row_spec())
    n_cast = 0
    if cast_next is not None:
        assert grid[0] * FFN_TF == D_MODEL
        n_cast = 3
        up_map = lambda i, f: (0, 1, i, tile_a(f))
        down_map = lambda i, f: (0, 1, tile_a(f), i)
        sq = (None, None, FFN_TF, FFN_TF)
        in_specs += [pl.BlockSpec(sq, up_map), pl.BlockSpec(sq, up_map), pl.BlockSpec(sq, down_map)]
        args += list(cast_next)
        out_shape += [jax.ShapeDtypeStruct((D_MODEL, D_FF), BF16), jax.ShapeDtypeStruct((D_MODEL, D_FF), BF16),
                      jax.ShapeDtypeStruct((D_FF, D_MODEL), BF16)]
        out_specs += [pl.BlockSpec((FFN_TF, FFN_TF), lambda i, f: (i, tile_a(f))),
                      pl.BlockSpec((FFN_TF, FFN_TF), lambda i, f: (i, tile_a(f))),
                      pl.BlockSpec((FFN_TF, FFN_TF), lambda i, f: (tile_a(f), i))]
    return pl.pallas_call(
        functools.partial(_ffn_body, emit_bf16=emit_bf16, n_cast=n_cast),
        grid=grid,
        in_specs=in_specs,
        out_specs=out_specs,
        out_shape=out_shape,
        scratch_shapes=[pltpu.VMEM((tm, D_MODEL), BF16), pltpu.VMEM((2, tm, FFN_TF), BF16)],
        compiler_params=pltpu.CompilerParams(
            dimension_semantics=("parallel", "arbitrary"), vmem_limit_bytes=VMEM_LIMIT),
        name="ffn",
    )(*args)


def _proj_body(x_ref, w_ref, wt_ref, cos_ref, sin_ref, o_ref, t_ref):
    j = pl.program_id(1)
    acc = _dot_nt(x_ref[...], w_ref[...].astype(BF16))
    n_full, n_part = divmod((D_SWA_Q + D_SWA_KV) // HEAD_DIM, PROJ_TN // HEAD_DIM)

    def store(n_rope_heads):
        cos = cos_ref[...]
        sin = sin_ref[...]
        for s in range(PROJ_TN // HEAD_DIM):
            cols = slice(s * HEAD_DIM, (s + 1) * HEAD_DIM)
            blk = acc[:, cols]
            o_ref[:, cols] = blk * cos + pltpu.roll(blk, HEAD_DIM // 2, 1) * sin if s < n_rope_heads else blk

    pl.when(j < n_full)(lambda: store(PROJ_TN // HEAD_DIM))
    pl.when(j == n_full)(lambda: store(n_part))

    @pl.when(j > n_full)
    def _():
        o_ref[...] = acc

    @pl.when(j == 0)
    def _():
        t_ref[...] = _dot_nt(x_ref[...], wt_ref[...])


def _proj(xb, w_in, w_tail, cos, sin, *, tm):
    rows = xb.shape[0]
    n_swa = (D_SWA_Q + 2 * D_SWA_KV) // PROJ_TN

    def dest(i, j):
        return i, jnp.where(j < n_swa, j + COL_Q // PROJ_TN, j - n_swa)

    return pl.pallas_call(
        _proj_body,
        grid=(rows // tm, D_MAIN // PROJ_TN),
        in_specs=[
            pl.BlockSpec((tm, D_MODEL), lambda i, j: (i, 0)),
            pl.BlockSpec((PROJ_TN, D_MODEL), lambda i, j: (j, 0)),
            pl.BlockSpec((TAIL_W, D_MODEL), lambda i, j: (0, 0)),
            pl.BlockSpec((tm, HEAD_DIM), lambda i, j: (i, 0)),
            pl.BlockSpec((tm, HEAD_DIM), lambda i, j: (i, 0)),
        ],
        out_specs=[
            pl.BlockSpec((tm, PROJ_TN), dest),
            pl.BlockSpec((tm, TAIL_W), lambda i, j: (i, 0)),
        ],
        out_shape=[
            jax.ShapeDtypeStruct((rows, D_MAIN), F32),
            jax.ShapeDtypeStruct((rows, TAIL_W), F32),
        ],
        compiler_params=pltpu.CompilerParams(
            dimension_semantics=("parallel", "arbitrary"), vmem_limit_bytes=VMEM_LIMIT),
        name="proj_in",
    )(xb, w_in, w_tail, cos, sin)


N_KEYS = N_META + 3 * CHUNK


def _stack_heads(ref, rows, heads):
    return jnp.concatenate([ref[rows, h * HEAD_DIM:(h + 1) * HEAD_DIM] for h in heads], axis=0)


def _group_heads(g):
    return range(g * GQA_GROUP, (g + 1) * GQA_GROUP)


def _sink_rows(sink_ref, g):
    return jnp.concatenate([jnp.full((CHUNK, 1), sink_ref[h], F32) for h in _group_heads(g)], axis=0)


def _softmax_pv(q, k, v, sink, mask):
    s = _hdot_nt(q, k) * (HEAD_DIM ** -0.5)
    if mask is not None:
        s = jnp.where(mask, s, -jnp.inf)
    m = jnp.maximum(jnp.max(s, axis=-1, keepdims=True), sink)
    p = jnp.exp(s - m)
    den = jnp.sum(p, axis=-1, keepdims=True) + jnp.exp(sink - m)
    return _hdot(p, v) / den


def _attn_sample_body(sink_ref, q_ref, km_ref, vm_ref, k2_ref, v2_ref, k1_ref, v1_ref, k0_ref, v0_ref, o_ref):
    groups = range(N_KV_SWA)
    every = slice(None)
    k = jnp.stack([jnp.concatenate([_stack_heads(r, every, [g]) for r in (km_ref, k2_ref, k1_ref, k0_ref)], axis=0)
                   for g in groups])
    v = jnp.stack([jnp.concatenate([_stack_heads(r, every, [g]) for r in (vm_ref, v2_ref, v1_ref, v0_ref)], axis=0)
                   for g in groups])
    q = jnp.stack([_stack_heads(q_ref, every, _group_heads(g)) for g in groups])
    sink = jnp.stack([_sink_rows(sink_ref, g) for g in groups])
    o = _softmax_pv(q, k, v, sink, None)
    for g in groups:
        for j, h in enumerate(_group_heads(g)):
            o_ref[:, h * HEAD_DIM:(h + 1) * HEAD_DIM] = o[g, j * CHUNK:(j + 1) * CHUNK].astype(BF16)


def _attn_pair_body(sink_ref, q_ref, km_ref, vm_ref, kp_ref, vp_ref, kc_ref, vc_ref, o_ref):
    has_prev = pl.program_id(0) % (CHUNKS_PER_SEQ // 2) >= 1
    col = lax.broadcasted_iota(jnp.int32, (1, N_KEYS), 1)
    groups = range(N_KV_SWA)
    key_rows = ((slice(0, 2 * CHUNK), slice(0, CHUNK)), (slice(CHUNK, 2 * CHUNK), slice(0, 2 * CHUNK)))
    n_prev = (2 * CHUNK, CHUNK)
    q, k, v, sink, mask = [], [], [], [], []
    for sub in range(2):
        prev_rows, cur_rows = key_rows[sub]
        visible = jnp.logical_or(jnp.logical_or(col < N_META, col >= N_META + n_prev[sub]), has_prev)
        for g in groups:
            k.append(jnp.concatenate([_stack_heads(km_ref, slice(None), [g]), _stack_heads(kp_ref, prev_rows, [g]),
                                      _stack_heads(kc_ref, cur_rows, [g])], axis=0))
            v.append(jnp.concatenate([_stack_heads(vm_ref, slice(None), [g]), _stack_heads(vp_ref, prev_rows, [g]),
                                      _stack_heads(vc_ref, cur_rows, [g])], axis=0))
            q.append(_stack_heads(q_ref, slice(sub * CHUNK, (sub + 1) * CHUNK), _group_heads(g)))
            sink.append(_sink_rows(sink_ref, g))
            mask.append(visible)
    o = _softmax_pv(jnp.stack(q), jnp.stack(k), jnp.stack(v), jnp.stack(sink), jnp.stack(mask))
    for sub in range(2):
        for g in groups:
            for j, h in enumerate(_group_heads(g)):
                o_ref[sub * CHUNK:(sub + 1) * CHUNK, h * HEAD_DIM:(h + 1) * HEAD_DIM] = (
                    o[sub * N_KV_SWA + g, j * CHUNK:(j + 1) * CHUNK].astype(BF16))


def _attn_prompt(sinks, h_p, h_m):
    kcol = COL_K // D_SWA_KV
    vcol = COL_V // D_SWA_KV
    qcol = COL_Q // D_SWA_Q
    pairs_per_seq = CHUNKS_PER_SEQ // 2
    prev = lambda colblk: (lambda p: (p - jnp.where(p % pairs_per_seq >= 1, 1, 0), colblk))
    kv = lambda imap: pl.BlockSpec((2 * CHUNK, D_SWA_KV), imap)
    meta = lambda colblk: pl.BlockSpec((N_META, D_SWA_KV), lambda p: (META_ROW0 // N_META, colblk))
    return pl.pallas_call(
        _attn_pair_body,
        grid=(N_PROMPT_ROWS // (2 * CHUNK),),
        in_specs=[
            pl.BlockSpec(memory_space=pltpu.SMEM),
            pl.BlockSpec((2 * CHUNK, D_SWA_Q), lambda p: (p, qcol)),
            meta(kcol), meta(vcol),
            kv(prev(kcol)), kv(prev(vcol)),
            kv(lambda p: (p, kcol)), kv(lambda p: (p, vcol)),
        ],
        out_specs=pl.BlockSpec((2 * CHUNK, D_SWA_Q), lambda p: (p, 0)),
        out_shape=jax.ShapeDtypeStruct((N_PROMPT_ROWS, D_SWA_Q), BF16),
        compiler_params=pltpu.CompilerParams(dimension_semantics=("parallel",)),
        name="swa_prompt",
    )(sinks, h_p, h_m, h_m, h_p, h_p, h_p, h_p)


def _attn_sample(sinks, h_s, meta_k, meta_v, win_k, win_v):
    kcol = COL_K // D_SWA_KV
    vcol = COL_V // D_SWA_KV
    qcol = COL_Q // D_SWA_Q
    kv = lambda imap: pl.BlockSpec((CHUNK, D_SWA_KV), imap)
    return pl.pallas_call(
        _attn_sample_body,
        grid=(DEC_BATCH,),
        in_specs=[
            pl.BlockSpec(memory_space=pltpu.SMEM),
            pl.BlockSpec((CHUNK, D_SWA_Q), lambda s: (s, qcol)),
            pl.BlockSpec((N_META, D_SWA_KV), lambda s: (s, 0)),
            pl.BlockSpec((N_META, D_SWA_KV), lambda s: (s, 0)),
            kv(lambda s: (2 * s, 0)), kv(lambda s: (2 * s, 0)),
            kv(lambda s: (2 * s + 1, 0)), kv(lambda s: (2 * s + 1, 0)),
            kv(lambda s: (s, kcol)), kv(lambda s: (s, vcol)),
        ],
        out_specs=pl.BlockSpec((CHUNK, D_SWA_Q), lambda s: (s, 0)),
        out_shape=jax.ShapeDtypeStruct((N_SAMPLE_ROWS, D_SWA_Q), BF16),
        compiler_params=pltpu.CompilerParams(dimension_semantics=("parallel",)),
        name="swa_sample",
    )(sinks, h_s, meta_k, meta_v, win_k, win_v, win_k, win_v, h_s, h_s)


def _unit_lower_inverse(a, n):
    r = lax.broadcasted_iota(jnp.int32, (n, n), 0)
    c = lax.broadcasted_iota(jnp.int32, (n, n), 1)
    eye = (r == c).astype(F32)
    a0 = jnp.where((r >> 3) == (c >> 3), a, 0.0)
    a2 = _hdot(a0, a0)
    a4 = _hdot(a2, a2)
    x = _hdot(_hdot(eye - a0, eye + a2), eye + a4)
    shift = 3
    while (1 << shift) < n:
        pair = jnp.logical_and((r >> (shift + 1)) == (c >> (shift + 1)), (r >> shift) != (c >> shift))
        ak = jnp.where(pair, a, 0.0)
        x = x - _hdot(x, _hdot(ak, x))
        shift += 1
    return x


def _gdn_body(x_ref, z_ref, t_ref, w_ref, alog_ref, dt_ref, nw_ref, s0_ref, b0_ref, o_ref, sout_ref, s_scr,
              xe_scr, *, C, n_chunks):
    c = pl.program_id(1)

    @pl.when(c == 0)
    def _():
        s_scr[...] = s0_ref[0]
        xe_scr[0:CARRY, :] = b0_ref[0]

    @pl.when(c > 0)
    def _():
        xe_scr[0:CARRY, :] = xe_scr[C:C + CARRY, :]

    xe_scr[CARRY:CARRY + C, :] = x_ref[...]

    t = t_ref[...]
    beta = jax.nn.sigmoid(t)
    ta = t + dt_ref[...]
    softplus = jnp.maximum(ta, 0.0) + jnp.log(1.0 + jnp.exp(-jnp.abs(ta)))
    g = -jnp.exp(alog_ref[...]) * softplus
    r = lax.broadcasted_iota(jnp.int32, (C, C), 0)
    cc = lax.broadcasted_iota(jnp.int32, (C, C), 1)
    incl = r >= cc
    strict = r > cc
    gc = _dot(incl.astype(F32), g, HI)
    gc_t = gc.T
    heads = range(N_HEADS_GDN)
    bh = jnp.stack([beta[:, h:h + 1] for h in heads])
    gcol = jnp.stack([gc[:, N_HEADS_GDN + h:N_HEADS_GDN + h + 1] for h in heads])
    grow = jnp.stack([gc_t[N_HEADS_GDN + h:N_HEADS_GDN + h + 1, :] for h in heads])
    glast = gcol[:, C - 1:C, :]
    decay = jnp.where(incl, jnp.exp(jnp.where(incl, gcol - grow, 0.0)), 0.0)
    e_g = jnp.exp(gcol)

    xe = xe_scr[...]
    acc = w_ref[CONV_WIDTH - 1:CONV_WIDTH, :] * xe[CARRY:, :]
    for lag in range(1, CONV_WIDTH):
        tap = CONV_WIDTH - 1 - lag
        acc = acc + w_ref[tap:tap + 1, :] * pltpu.roll(xe, lag, 0)[CARRY:, :]
    y = _silu(acc)
    split = lambda base: jnp.stack([y[:, base + h * DK_GDN:base + (h + 1) * DK_GDN] for h in heads])
    q, k, v = split(0), split(D_GDN), split(2 * D_GDN)
    q = q * lax.rsqrt(jnp.sum(q * q, axis=-1, keepdims=True) + RMS_EPS) * (DK_GDN ** -0.5)
    k = k * lax.rsqrt(jnp.sum(k * k, axis=-1, keepdims=True) + RMS_EPS)
    kb = k * bh

    kq = _hdot_nt(jnp.concatenate([kb, q], axis=1), k)
    a = jnp.where(strict, kq[:, :C] * decay, 0.0)
    qk = kq[:, C:] * decay
    t_inv = _unit_lower_inverse(a, C)

    s_prev = s_scr[...]
    ws = _hdot(jnp.concatenate([kb * e_g, q * e_g], axis=1), s_prev)
    v_new = _hdot(t_inv, v * bh - ws[:, :C])
    o = ws[:, C:] + _hdot(qk, v_new)
    s_scr[...] = s_prev * jnp.exp(glast) + _hdot_tn(k * jnp.exp(glast - gcol), v_new)

    o = o * lax.rsqrt(jnp.mean(o * o, axis=-1, keepdims=True) + RMS_EPS) * nw_ref[...]
    for h in heads:
        hs = slice(h * DV_GDN, (h + 1) * DV_GDN)
        o_ref[:, hs] = (o[h] * _silu(z_ref[:, hs])).astype(BF16)

    @pl.when(c == n_chunks - 1)
    def _():
        sout_ref[0] = s_scr[...]


def _gdn(h_src, tail_src, conv_w, alog_row, dt_row, nw_row, s0, buf0, *, n_seq, n_chunks, C, shared_init,
         first_chunk=0):
    init = (lambda s: 0) if shared_init else (lambda s: s)
    vec = pl.BlockSpec((1, TAIL_W), lambda s, c: (0, 0))
    src = lambda s, c: first_chunk + s * n_chunks + c
    return pl.pallas_call(
        functools.partial(_gdn_body, C=C, n_chunks=n_chunks),
        grid=(n_seq, n_chunks),
        in_specs=[
            pl.BlockSpec((C, D_CONV), lambda s, c: (src(s, c), COL_CONV // D_CONV)),
            pl.BlockSpec((C, D_GDN), lambda s, c: (src(s, c), COL_Z // D_GDN)),
            pl.BlockSpec((C, TAIL_W), lambda s, c: (src(s, c), 0)),
            pl.BlockSpec((CONV_WIDTH, D_CONV), lambda s, c: (0, 0)),
            vec, vec, vec,
            pl.BlockSpec((1, N_HEADS_GDN, DK_GDN, DV_GDN), lambda s, c: (init(s), 0, 0, 0)),
            pl.BlockSpec((1, CARRY, D_CONV), lambda s, c: (init(s), 0, 0)),
        ],
        out_specs=[
            pl.BlockSpec((C, D_GDN), lambda s, c: (s * n_chunks + c, 0)),
            pl.BlockSpec((1, N_HEADS_GDN, DK_GDN, DV_GDN), lambda s, c: (s, 0, 0, 0)),
        ],
        out_shape=[
            jax.ShapeDtypeStruct((n_seq * n_chunks * C, D_GDN), BF16),
            jax.ShapeDtypeStruct((n_seq, N_HEADS_GDN, DK_GDN, DV_GDN), F32),
        ],
        scratch_shapes=[
            pltpu.VMEM((N_HEADS_GDN, DK_GDN, DV_GDN), F32),
            pltpu.VMEM((CARRY + C, D_CONV), F32),
        ],
        compiler_params=pltpu.CompilerParams(dimension_semantics=("parallel", "arbitrary")),
        name="gdn",
    )(h_src, h_src, tail_src, conv_w, alog_row, dt_row, nw_row, s0, buf0)


def _out_body(a_ref, b_ref, w_ref, x_ref, g_ref, beta_ref, o_ref):
    mixed = _dot(a_ref[...], w_ref[0:D_SWA_Q, :]) + _dot(b_ref[...], w_ref[D_SWA_Q:, :])
    o_ref[...] = _layernorm_rows(ALPHA * x_ref[...] + mixed, g_ref[...], beta_ref[...])


def _out_proj(o_swa, o_gdn, w_out, x1, g, b, *, tm):
    rows = o_swa.shape[0]
    return pl.pallas_call(
        _out_body,
        grid=(rows // tm,),
        in_specs=[
            pl.BlockSpec((tm, D_SWA_Q), lambda i: (i, 0)),
            pl.BlockSpec((tm, D_GDN), lambda i: (i, 0)),
            pl.BlockSpec((D_MODEL, D_MODEL), lambda i: (0, 0), pipeline_mode=pl.Buffered(1)),
            pl.BlockSpec((tm, D_MODEL), lambda i: (i, 0)),
            pl.BlockSpec((1, D_MODEL), lambda i: (0, 0)),
            pl.BlockSpec((1, D_MODEL), lambda i: (0, 0)),
        ],
        out_specs=pl.BlockSpec((tm, D_MODEL), lambda i: (i, 0)),
        out_shape=jax.ShapeDtypeStruct((rows, D_MODEL), F32),
        compiler_params=pltpu.CompilerParams(
            dimension_semantics=("parallel",), vmem_limit_bytes=VMEM_LIMIT),
        name="proj_out",
    )(o_swa, o_gdn, w_out, x1, g, b)


def _rope_tables(pos):
    half = HEAD_DIM // 2
    inv = ROPE_THETA ** (-jnp.arange(half, dtype=F32) / half)
    ang = pos.astype(F32)[:, None] * inv[None, :]
    cos = jnp.cos(ang)
    sin = jnp.sin(ang)
    return jnp.concatenate([cos, cos], axis=1), jnp.concatenate([-sin, sin], axis=1)


def kernel(x_prompt, x_sample, cache_meta_k, cache_meta_v, cache_win_k, cache_win_v, state_conv, state_gdn,
           meta_tokens, ln_g, ln_b, ffn_w_gate, ffn_w_up, ffn_w_down, w_in, w_out, attn_sinks, conv_w,
           gdn_a_log, gdn_dt_bias, gdn_norm_w):
    l = 0
    wg1, wu1, wd1 = (w[l, 0].astype(BF16) for w in (ffn_w_gate, ffn_w_up, ffn_w_down))
    w_main = jnp.swapaxes(w_in[l], 0, 1)
    w_tail = jnp.pad(w_main[D_MAIN:], ((0, TAIL_W - 2 * N_HEADS_GDN), (0, 0))).astype(BF16)
    wo = w_out[l].astype(BF16)
    g1, g2, g3 = (ln_g[l, i][None, :] for i in range(3))
    b1, b2, b3 = (ln_b[l, i][None, :] for i in range(3))
    pad_tail = lambda v, off: jnp.pad(v.astype(F32), (off, TAIL_W - off - N_HEADS_GDN))[None, :]
    alog_row = pad_tail(gdn_a_log[l], N_HEADS_GDN)
    dt_row = pad_tail(gdn_dt_bias[l], N_HEADS_GDN)
    nw_row = gdn_norm_w[l].astype(F32)[None, :]
    cw = conv_w[l]
    sinks = attn_sinks[l].astype(F32)

    n_pad = N_SM_ROWS - N_SAMPLE_ROWS - N_META
    xp = x_prompt.reshape(N_PROMPT_ROWS, D_MODEL)
    xs = jnp.concatenate([x_sample.reshape(N_SAMPLE_ROWS, D_MODEL), meta_tokens.astype(F32),
                          jnp.zeros((n_pad, D_MODEL), F32)], axis=0)
    cos_p, sin_p = _rope_tables(jnp.tile(N_META + jnp.arange(SEQ, dtype=jnp.int32), BATCH))
    cos_s, sin_s = _rope_tables(jnp.concatenate([
        jnp.tile(N_META + PAST_LEN + jnp.arange(DEC_SEQ, dtype=jnp.int32), DEC_BATCH),
        jnp.arange(N_META, dtype=jnp.int32), jnp.zeros((n_pad,), jnp.int32)]))

    ffn1 = functools.partial(_ffn, wg=wg1, wu=wu1, wd=wd1, g=g1, b=b1, emit_bf16=True)
    x1_s, x1b_s = ffn1(xs, tm=SM_TILE)
    x1_p, x1b_p, wg2, wu2, wd2 = ffn1(xp, tm=512, cast_next=(ffn_w_gate, ffn_w_up, ffn_w_down))
    h_p, t_p = _proj(x1b_p, w_main, w_tail, cos_p, sin_p, tm=1024)
    h_s, t_s = _proj(x1b_s, w_main, w_tail, cos_s, sin_s, tm=N_SM_ROWS)
    meta_rows = slice(META_ROW0, META_ROW0 + N_META)

    o_swa_p = _attn_prompt(sinks, h_p, h_s)
    o_swa_s = _attn_sample(
        sinks, h_s,
        cache_meta_k[l].reshape(DEC_BATCH * N_META, D_SWA_KV), cache_meta_v[l].reshape(DEC_BATCH * N_META, D_SWA_KV),
        cache_win_k[l].reshape(DEC_BATCH * WINDOW, D_SWA_KV), cache_win_v[l].reshape(DEC_BATCH * WINDOW, D_SWA_KV))

    zero_s = jnp.zeros((1, N_HEADS_GDN, DK_GDN, DV_GDN), F32)
    zero_buf = jnp.zeros((1, CARRY, D_CONV), F32)
    gdn = functools.partial(_gdn, conv_w=cw, alog_row=alog_row, dt_row=dt_row, nw_row=nw_row)
    _, s_meta = gdn(h_s, t_s, s0=zero_s, buf0=zero_buf, n_seq=1, n_chunks=1, C=N_META, shared_init=True,
                    first_chunk=META_ROW0 // N_META)
    buf_meta = h_s[None, META_ROW0 + N_META - CARRY:META_ROW0 + N_META, COL_CONV:COL_CONV + D_CONV]
    o_gdn_p, s_prompt = gdn(h_p, t_p, s0=s_meta, buf0=buf_meta, n_seq=BATCH, n_chunks=CHUNKS_PER_SEQ, C=CHUNK,
                            shared_init=True)
    buf_s = jnp.pad(state_conv[l].astype(F32), ((0, 0), (CARRY - (CONV_WIDTH - 1), 0), (0, 0)))
    o_gdn_s, s_sample = gdn(h_s, t_s, s0=state_gdn[l].astype(F32), buf0=buf_s, n_seq=DEC_BATCH, n_chunks=1,
                            C=CHUNK, shared_init=False)

    x2_p = _out_proj(o_swa_p, o_gdn_p, wo, x1_p, g2, b2, tm=256)
    x2_s = _out_proj(o_swa_s, o_gdn_s, wo, x1_s, g2, b2, tm=256)
    ffn2 = functools.partial(_ffn, wg=wg2, wu=wu2, wd=wd2, g=g3, b=b3, emit_bf16=False)
    (y_p,) = ffn2(x2_p, tm=512)
    (y_s,) = ffn2(x2_s, tm=512)

    y_prompt = y_p.reshape(BATCH, SEQ, D_MODEL)
    y_sample = y_s.reshape(DEC_BATCH, DEC_SEQ, D_MODEL)
    k_meta = h_s[meta_rows, COL_K:COL_K + D_SWA_KV].reshape(N_META, N_KV_SWA, HEAD_DIM)
    v_meta = h_s[meta_rows, COL_V:COL_V + D_SWA_KV].reshape(N_META, N_KV_SWA, HEAD_DIM)
    p_meta_k = jnp.broadcast_to(k_meta[None, None], (1, BATCH, N_META, N_KV_SWA, HEAD_DIM))
    p_meta_v = jnp.broadcast_to(v_meta[None, None], (1, BATCH, N_META, N_KV_SWA, HEAD_DIM))
    hp = h_p.reshape(BATCH, SEQ, D_MAIN)
    sample_cols = lambda c0, width: h_s[:N_SAMPLE_ROWS, c0:c0 + width].reshape(DEC_BATCH, DEC_SEQ, width)
    p_win_k = hp[:, SEQ - WINDOW:, COL_K:COL_K + D_SWA_KV].reshape(1, BATCH, WINDOW, N_KV_SWA, HEAD_DIM)
    p_win_v = hp[:, SEQ - WINDOW:, COL_V:COL_V + D_SWA_KV].reshape(1, BATCH, WINDOW, N_KV_SWA, HEAD_DIM)
    p_conv = hp[:, SEQ - (CONV_WIDTH - 1):, COL_CONV:COL_CONV + D_CONV][None]
    p_gdn = s_prompt[None]
    s_win_k = sample_cols(COL_K, D_SWA_KV).reshape(1, DEC_BATCH, DEC_SEQ, N_KV_SWA, HEAD_DIM)
    s_win_v = sample_cols(COL_V, D_SWA_KV).reshape(1, DEC_BATCH, DEC_SEQ, N_KV_SWA, HEAD_DIM)
    s_conv = sample_cols(COL_CONV, D_CONV)[:, DEC_SEQ - (CONV_WIDTH - 1):][None]
    s_gdn = s_sample[None]
    return (y_prompt, y_sample, p_meta_k, p_meta_v, p_win_k, p_win_v, p_conv, p_gdn, s_win_k, s_win_v, s_conv,
            s_gdn)
```

```python
import functools

import jax
import jax.numpy as jnp
from jax import lax
from jax.experimental import pallas as pl
from jax.experimental.pallas import tpu as pltpu

D_MODEL = 4096
BATCH = 4
SEQ = 2048
DEC_BATCH = 16
DEC_SEQ = 64
PAST_LEN = 2048
CHUNK = 64
N_META = 16
WINDOW = 128
HEAD_DIM = 128
N_HEADS_SWA = 16
N_KV_SWA = 4
GQA_GROUP = 4
N_HEADS_GDN = 16
DK_GDN = 128
DV_GDN = 128
CONV_WIDTH = 4
D_SWA_Q = N_HEADS_SWA * HEAD_DIM
D_SWA_KV = N_KV_SWA * HEAD_DIM
D_GDN = N_HEADS_GDN * DK_GDN
D_CONV = 3 * D_GDN
D_FF = 11008
ROPE_THETA = 10000.0
LN_EPS = 1e-5
RMS_EPS = 1e-6
ALPHA = 2.0 ** 0.25

N_PROMPT_ROWS = BATCH * SEQ
N_SAMPLE_ROWS = DEC_BATCH * DEC_SEQ
CHUNKS_PER_SEQ = SEQ // CHUNK
META_ROW0 = N_SAMPLE_ROWS
SM_TILE = 528
N_SM_ROWS = 2 * SM_TILE

COL_CONV = 0
COL_Z = D_CONV
COL_Q = COL_Z + D_GDN
COL_K = COL_Q + D_SWA_Q
COL_V = COL_K + D_SWA_KV
D_MAIN = COL_V + D_SWA_KV
TAIL_W = 128

FFN_TF = 256
PROJ_TN_F32 = 512
PROJ_TN_BF16 = 1024
CARRY = 8

VMEM_LIMIT = 60 * 1024 * 1024

BF16 = jnp.bfloat16
F32 = jnp.float32
HI = lax.Precision.HIGHEST


def _dot(a, b, precision=None):
    return jnp.dot(a, b, preferred_element_type=F32, precision=precision)


def _dot_nt(a, b):
    return lax.dot_general(a, b, (((1,), (1,)), ((), ())), preferred_element_type=F32)


def _dot_tn(a, b):
    return lax.dot_general(a, b, (((0,), (0,)), ((), ())), preferred_element_type=F32)


def _head_dot(a, b, lhs_contract, rhs_contract):
    dims = (((lhs_contract,), (rhs_contract,)), ((0,), (0,)))
    return lax.dot_general(a.astype(BF16), b.astype(BF16), dims, preferred_element_type=F32)


_hdot = functools.partial(_head_dot, lhs_contract=2, rhs_contract=1)
_hdot_nt = functools.partial(_head_dot, lhs_contract=2, rhs_contract=2)
_hdot_tn = functools.partial(_head_dot, lhs_contract=1, rhs_contract=1)


def _layernorm_rows(y, g, b):
    mu = jnp.mean(y, axis=-1, keepdims=True)
    d = y - mu
    var = jnp.mean(d * d, axis=-1, keepdims=True)
    return d * lax.rsqrt(var + LN_EPS) * g + b


def _silu(x):
    return x * jax.nn.sigmoid(x)


def _ffn_body(x_ref, wg_ref, wu_ref, wd_ref, g_ref, b_ref, *rest, emit_bf16, n_cast):
    cast_in, rest = rest[:n_cast], rest[n_cast:]
    o_ref, rest = rest[0], rest[1:]
    if emit_bf16:
        ob_ref, rest = rest[0], rest[1:]
    cast_out, (xb_ref,) = rest[:n_cast], rest[n_cast:]
    f = pl.program_id(1)

    for src, dst in zip(cast_in, cast_out):
        dst[...] = src[...].astype(BF16)

    @pl.when(f == 0)
    def _():
        xb_ref[...] = x_ref[...].astype(BF16)
        o_ref[...] = jnp.zeros_like(o_ref)

    xb = xb_ref[...]
    hidden = _silu(_dot(xb, wg_ref[...])) * _dot(xb, wu_ref[...])
    o_ref[...] += _dot(hidden.astype(BF16), wd_ref[...])

    @pl.when(f == pl.num_programs(1) - 1)
    def _():
        y = _layernorm_rows(ALPHA * x_ref[...] + 0.5 * o_ref[...], g_ref[...], b_ref[...])
        o_ref[...] = y
        if emit_bf16:
            ob_ref[...] = y.astype(BF16)


def _ffn(x, wg, wu, wd, g, b, *, tm, emit_bf16, cast_next=None):
    rows = x.shape[0]
    grid = (rows // tm, D_FF // FFN_TF)
    row_spec = lambda: pl.BlockSpec((tm, D_MODEL), lambda i, f: (i, 0))
    in_specs = [
        pl.BlockSpec((tm, D_MODEL), lambda i, f: (i, 0), pipeline_mode=pl.Buffered(1)),
        pl.BlockSpec((D_MODEL, FFN_TF), lambda i, f: (0, f)),
        pl.BlockSpec((D_MODEL, FFN_TF), lambda i, f: (0, f)),
        pl.BlockSpec((FFN_TF, D_MODEL), lambda i, f: (f, 0)),
        pl.BlockSpec((1, D_MODEL), lambda i, f: (0, 0)),
        pl.BlockSpec((1, D_MODEL), lambda i, f: (0, 0)),
    ]
    args = [x, wg, wu, wd, g, b]
    out_shape = [jax.ShapeDtypeStruct((rows, D_MODEL), F32)]
    out_specs = [row_spec()]
    if emit_bf16:
        out_shape.append(jax.ShapeDtypeStruct((rows, D_MODEL), BF16))
        out_specs.append(row_spec())
    n_cast = 0
    if cast_next is not None:
        assert grid[0] * FFN_TF == D_MODEL
        n_cast = 3
        up_map = lambda i, f: (0, 1, i, f)
        down_map = lambda i, f: (0, 1, f, i)
        sq = (None, None, FFN_TF, FFN_TF)
        in_specs += [pl.BlockSpec(sq, up_map), pl.BlockSpec(sq, up_map), pl.BlockSpec(sq, down_map)]
        args += list(cast_next)
        out_shape += [jax.ShapeDtypeStruct((D_MODEL, D_FF), BF16), jax.ShapeDtypeStruct((D_MODEL, D_FF), BF16),
                      jax.ShapeDtypeStruct((D_FF, D_MODEL), BF16)]
        out_specs += [pl.BlockSpec((FFN_TF, FFN_TF), lambda i, f: (i, f)),
                      pl.BlockSpec((FFN_TF, FFN_TF), lambda i, f: (i, f)),
                      pl.BlockSpec((FFN_TF, FFN_TF), lambda i, f: (f, i))]
    return pl.pallas_call(
        functools.partial(_ffn_body, emit_bf16=emit_bf16, n_cast=n_cast),
        grid=grid,
        in_specs=in_specs,
        out_specs=out_specs,
        out_shape=out_shape,
        scratch_shapes=[pltpu.VMEM((tm, D_MODEL), BF16)],
        compiler_params=pltpu.CompilerParams(
            dimension_semantics=("parallel", "arbitrary"), vmem_limit_bytes=VMEM_LIMIT),
        name="ffn",
    )(*args)


def _proj_body(x_ref, w_ref, wt_ref, cos_ref, sin_ref, o_ref, t_ref, *wb_ref, tn):
    j = pl.program_id(1)
    wb = w_ref[...].astype(BF16)
    if wb_ref:
        wb_ref[0][...] = wb
    acc = _dot_nt(x_ref[...], wb)
    n_full, n_part = divmod((D_SWA_Q + D_SWA_KV) // HEAD_DIM, tn // HEAD_DIM)

    def store(n_rope_heads):
        cos = cos_ref[...]
        sin = sin_ref[...]
        for s in range(tn // HEAD_DIM):
            cols = slice(s * HEAD_DIM, (s + 1) * HEAD_DIM)
            blk = acc[:, cols]
            o_ref[:, cols] = blk * cos + pltpu.roll(blk, HEAD_DIM // 2, 1) * sin if s < n_rope_heads else blk

    pl.when(j < n_full)(lambda: store(tn // HEAD_DIM))
    pl.when(j == n_full)(lambda: store(n_part))

    @pl.when(j > n_full)
    def _():
        o_ref[...] = acc

    @pl.when(j == 0)
    def _():
        t_ref[...] = _dot_nt(x_ref[...], wt_ref[...])


def _proj(xb, w_t, w_tail, cos, sin, *, tm, tn, emit_weights):
    rows = xb.shape[0]
    grid = (rows // tm, D_MAIN // tn)
    n_swa = (D_SWA_Q + 2 * D_SWA_KV) // tn

    def dest(i, j):
        return i, jnp.where(j < n_swa, j + COL_Q // tn, j - n_swa)

    out_specs = [pl.BlockSpec((tm, tn), dest), pl.BlockSpec((tm, TAIL_W), lambda i, j: (i, 0))]
    out_shape = [jax.ShapeDtypeStruct((rows, D_MAIN), F32), jax.ShapeDtypeStruct((rows, TAIL_W), F32)]
    if emit_weights:
        assert grid[0] == 1
        out_specs.append(pl.BlockSpec((tn, D_MODEL), lambda i, j: (j, 0)))
        out_shape.append(jax.ShapeDtypeStruct((D_MAIN, D_MODEL), BF16))
    return pl.pallas_call(
        functools.partial(_proj_body, tn=tn),
        grid=grid,
        in_specs=[
            pl.BlockSpec((tm, D_MODEL), lambda i, j: (i, 0)),
            pl.BlockSpec((tn, D_MODEL), lambda i, j: (j, 0)),
            pl.BlockSpec((TAIL_W, D_MODEL), lambda i, j: (0, 0)),
            pl.BlockSpec((tm, HEAD_DIM), lambda i, j: (i, 0)),
            pl.BlockSpec((tm, HEAD_DIM), lambda i, j: (i, 0)),
        ],
        out_specs=out_specs,
        out_shape=out_shape,
        compiler_params=pltpu.CompilerParams(
            dimension_semantics=("parallel", "arbitrary"), vmem_limit_bytes=VMEM_LIMIT),
        name="proj_in",
    )(xb, w_t, w_tail, cos, sin)


N_KEYS = N_META + 3 * CHUNK


def _stack_heads(ref, rows, heads):
    return jnp.concatenate([ref[rows, h * HEAD_DIM:(h + 1) * HEAD_DIM] for h in heads], axis=0)


def _group_heads(g):
    return range(g * GQA_GROUP, (g + 1) * GQA_GROUP)


def _sink_rows(sink_ref, g):
    return jnp.concatenate([jnp.full((CHUNK, 1), sink_ref[h], F32) for h in _group_heads(g)], axis=0)


def _softmax_pv(q, k, v, sink, mask):
    s = _hdot_nt(q, k) * (HEAD_DIM ** -0.5)
    if mask is not None:
        s = jnp.where(mask, s, -jnp.inf)
    m = jnp.maximum(jnp.max(s, axis=-1, keepdims=True), sink)
    p = jnp.exp(s - m)
    den = jnp.sum(p, axis=-1, keepdims=True) + jnp.exp(sink - m)
    return _hdot(p, v) / den


def _attn_sample_body(sink_ref, q_ref, km_ref, vm_ref, k2_ref, v2_ref, k1_ref, v1_ref, k0_ref, v0_ref, o_ref):
    groups = range(N_KV_SWA)
    every = slice(None)
    k = jnp.stack([jnp.concatenate([_stack_heads(r, every, [g]) for r in (km_ref, k2_ref, k1_ref, k0_ref)], axis=0)
                   for g in groups])
    v = jnp.stack([jnp.concatenate([_stack_heads(r, every, [g]) for r in (vm_ref, v2_ref, v1_ref, v0_ref)], axis=0)
                   for g in groups])
    q = jnp.stack([_stack_heads(q_ref, every, _group_heads(g)) for g in groups])
    sink = jnp.stack([_sink_rows(sink_ref, g) for g in groups])
    o = _softmax_pv(q, k, v, sink, None)
    for g in groups:
        for j, h in enumerate(_group_heads(g)):
            o_ref[:, h * HEAD_DIM:(h + 1) * HEAD_DIM] = o[g, j * CHUNK:(j + 1) * CHUNK].astype(BF16)


def _attn_pair_body(sink_ref, q_ref, km_ref, vm_ref, kp_ref, vp_ref, kc_ref, vc_ref, o_ref):
    has_prev = pl.program_id(0) % (CHUNKS_PER_SEQ // 2) >= 1
    col = lax.broadcasted_iota(jnp.int32, (1, N_KEYS), 1)
    groups = range(N_KV_SWA)
    key_rows = ((slice(0, 2 * CHUNK), slice(0, CHUNK)), (slice(CHUNK, 2 * CHUNK), slice(0, 2 * CHUNK)))
    n_prev = (2 * CHUNK, CHUNK)
    q, k, v, sink, mask = [], [], [], [], []
    for sub in range(2):
        prev_rows, cur_rows = key_rows[sub]
        visible = jnp.logical_or(jnp.logical_or(col < N_META, col >= N_META + n_prev[sub]), has_prev)
        for g in groups:
            k.append(jnp.concatenate([_stack_heads(km_ref, slice(None), [g]), _stack_heads(kp_ref, prev_rows, [g]),
                                      _stack_heads(kc_ref, cur_rows, [g])], axis=0))
            v.append(jnp.concatenate([_stack_heads(vm_ref, slice(None), [g]), _stack_heads(vp_ref, prev_rows, [g]),
                                      _stack_heads(vc_ref, cur_rows, [g])], axis=0))
            q.append(_stack_heads(q_ref, slice(sub * CHUNK, (sub + 1) * CHUNK), _group_heads(g)))
            sink.append(_sink_rows(sink_ref, g))
            mask.append(visible)
    o = _softmax_pv(jnp.stack(q), jnp.stack(k), jnp.stack(v), jnp.stack(sink), jnp.stack(mask))
    for sub in range(2):
        for g in groups:
            for j, h in enumerate(_group_heads(g)):
                o_ref[sub * CHUNK:(sub + 1) * CHUNK, h * HEAD_DIM:(h + 1) * HEAD_DIM] = (
                    o[sub * N_KV_SWA + g, j * CHUNK:(j + 1) * CHUNK].astype(BF16))


def _attn_prompt(sinks, h_p, h_m):
    kcol = COL_K // D_SWA_KV
    vcol = COL_V // D_SWA_KV
    qcol = COL_Q // D_SWA_Q
    pairs_per_seq = CHUNKS_PER_SEQ // 2
    prev = lambda colblk: (lambda p: (p - jnp.where(p % pairs_per_seq >= 1, 1, 0), colblk))
    kv = lambda imap: pl.BlockSpec((2 * CHUNK, D_SWA_KV), imap)
    meta = lambda colblk: pl.BlockSpec((N_META, D_SWA_KV), lambda p: (META_ROW0 // N_META, colblk))
    return pl.pallas_call(
        _attn_pair_body,
        grid=(N_PROMPT_ROWS // (2 * CHUNK),),
        in_specs=[
            pl.BlockSpec(memory_space=pltpu.SMEM),
            pl.BlockSpec((2 * CHUNK, D_SWA_Q), lambda p: (p, qcol)),
            meta(kcol), meta(vcol),
            kv(prev(kcol)), kv(prev(vcol)),
            kv(lambda p: (p, kcol)), kv(lambda p: (p, vcol)),
        ],
        out_specs=pl.BlockSpec((2 * CHUNK, D_SWA_Q), lambda p: (p, 0)),
        out_shape=jax.ShapeDtypeStruct((N_PROMPT_ROWS, D_SWA_Q), BF16),
        compiler_params=pltpu.CompilerParams(dimension_semantics=("parallel",)),
        name="swa_prompt",
    )(sinks, h_p, h_m, h_m, h_p, h_p, h_p, h_p)


def _attn_sample(sinks, h_s, meta_k, meta_v, win_k, win_v):
    kcol = COL_K // D_SWA_KV
    vcol = COL_V // D_SWA_KV
    qcol = COL_Q // D_SWA_Q
    kv = lambda imap: pl.BlockSpec((CHUNK, D_SWA_KV), imap)
    return pl.pallas_call(
        _attn_sample_body,
        grid=(DEC_BATCH,),
        in_specs=[
            pl.BlockSpec(memory_space=pltpu.SMEM),
            pl.BlockSpec((CHUNK, D_SWA_Q), lambda s: (s, qcol)),
            pl.BlockSpec((N_META, D_SWA_KV), lambda s: (s, 0)),
            pl.BlockSpec((N_META, D_SWA_KV), lambda s: (s, 0)),
            kv(lambda s: (2 * s, 0)), kv(lambda s: (2 * s, 0)),
            kv(lambda s: (2 * s + 1, 0)), kv(lambda s: (2 * s + 1, 0)),
            kv(lambda s: (s, kcol)), kv(lambda s: (s, vcol)),
        ],
        out_specs=pl.BlockSpec((CHUNK, D_SWA_Q), lambda s: (s, 0)),
        out_shape=jax.ShapeDtypeStruct((N_SAMPLE_ROWS, D_SWA_Q), BF16),
        compiler_params=pltpu.CompilerParams(dimension_semantics=("parallel",)),
        name="swa_sample",
    )(sinks, h_s, meta_k, meta_v, win_k, win_v, win_k, win_v, h_s, h_s)


def _unit_lower_inverse(a, n):
    r = lax.broadcasted_iota(jnp.int32, (n, n), 0)
    c = lax.broadcasted_iota(jnp.int32, (n, n), 1)
    eye = (r == c).astype(F32)
    a0 = jnp.where((r >> 3) == (c >> 3), a, 0.0)
    a2 = _hdot(a0, a0)
    a4 = _hdot(a2, a2)
    x = _hdot(_hdot(eye - a0, eye + a2), eye + a4)
    shift = 3
    while (1 << shift) < n:
        pair = jnp.logical_and((r >> (shift + 1)) == (c >> (shift + 1)), (r >> shift) != (c >> shift))
        ak = jnp.where(pair, a, 0.0)
        x = x - _hdot(x, _hdot(ak, x))
        shift += 1
    return x


def _gdn_body(x_ref, z_ref, t_ref, w_ref, alog_ref, dt_ref, nw_ref, s0_ref, b0_ref, o_ref, sout_ref, s_scr,
              xe_scr, *, C, n_chunks):
    c = pl.program_id(1)

    @pl.when(c == 0)
    def _():
        s_scr[...] = s0_ref[0]
        xe_scr[0:CARRY, :] = b0_ref[0]

    @pl.when(c > 0)
    def _():
        xe_scr[0:CARRY, :] = xe_scr[C:C + CARRY, :]

    xe_scr[CARRY:CARRY + C, :] = x_ref[...]

    t = t_ref[...]
    beta = jax.nn.sigmoid(t)
    ta = t + dt_ref[...]
    softplus = jnp.maximum(ta, 0.0) + jnp.log(1.0 + jnp.exp(-jnp.abs(ta)))
    g = -jnp.exp(alog_ref[...]) * softplus
    r = lax.broadcasted_iota(jnp.int32, (C, C), 0)
    cc = lax.broadcasted_iota(jnp.int32, (C, C), 1)
    incl = r >= cc
    strict = r > cc
    gc = _dot(incl.astype(F32), g, HI)
    gc_t = gc.T
    heads = range(N_HEADS_GDN)
    bh = jnp.stack([beta[:, h:h + 1] for h in heads])
    gcol = jnp.stack([gc[:, N_HEADS_GDN + h:N_HEADS_GDN + h + 1] for h in heads])
    grow = jnp.stack([gc_t[N_HEADS_GDN + h:N_HEADS_GDN + h + 1, :] for h in heads])
    glast = gcol[:, C - 1:C, :]
    decay = jnp.where(incl, jnp.exp(jnp.where(incl, gcol - grow, 0.0)), 0.0)
    e_g = jnp.exp(gcol)

    xe = xe_scr[...]
    acc = w_ref[CONV_WIDTH - 1:CONV_WIDTH, :] * xe[CARRY:, :]
    for lag in range(1, CONV_WIDTH):
        tap = CONV_WIDTH - 1 - lag
        acc = acc + w_ref[tap:tap + 1, :] * pltpu.roll(xe, lag, 0)[CARRY:, :]
    y = _silu(acc)
    split = lambda base: jnp.stack([y[:, base + h * DK_GDN:base + (h + 1) * DK_GDN] for h in heads])
    q, k, v = split(0), split(D_GDN), split(2 * D_GDN)
    q = q * lax.rsqrt(jnp.sum(q * q, axis=-1, keepdims=True) + RMS_EPS) * (DK_GDN ** -0.5)
    k = k * lax.rsqrt(jnp.sum(k * k, axis=-1, keepdims=True) + RMS_EPS)
    kb = k * bh

    kq = _hdot_nt(jnp.concatenate([kb, q], axis=1), k)
    a = jnp.where(strict, kq[:, :C] * decay, 0.0)
    qk = kq[:, C:] * decay
    t_inv = _unit_lower_inverse(a, C)

    s_prev = s_scr[...]
    ws = _hdot(jnp.concatenate([kb * e_g, q * e_g], axis=1), s_prev)
    v_new = _hdot(t_inv, v * bh - ws[:, :C])
    o = ws[:, C:] + _hdot(qk, v_new)
    s_scr[...] = s_prev * jnp.exp(glast) + _hdot_tn(k * jnp.exp(glast - gcol), v_new)

    o = o * lax.rsqrt(jnp.mean(o * o, axis=-1, keepdims=True) + RMS_EPS) * nw_ref[...]
    for h in heads:
        hs = slice(h * DV_GDN, (h + 1) * DV_GDN)
        o_ref[:, hs] = (o[h] * _silu(z_ref[:, hs])).astype(BF16)

    @pl.when(c == n_chunks - 1)
    def _():
        sout_ref[0] = s_scr[...]


def _gdn(h_src, tail_src, conv_w, alog_row, dt_row, nw_row, s0, buf0, *, n_seq, n_chunks, C, shared_init,
         first_chunk=0):
    init = (lambda s: 0) if shared_init else (lambda s: s)
    vec = pl.BlockSpec((1, TAIL_W), lambda s, c: (0, 0))
    src = lambda s, c: first_chunk + s * n_chunks + c
    return pl.pallas_call(
        functools.partial(_gdn_body, C=C, n_chunks=n_chunks),
        grid=(n_seq, n_chunks),
        in_specs=[
            pl.BlockSpec((C, D_CONV), lambda s, c: (src(s, c), COL_CONV // D_CONV)),
            pl.BlockSpec((C, D_GDN), lambda s, c: (src(s, c), COL_Z // D_GDN)),
            pl.BlockSpec((C, TAIL_W), lambda s, c: (src(s, c), 0)),
            pl.BlockSpec((CONV_WIDTH, D_CONV), lambda s, c: (0, 0)),
            vec, vec, vec,
            pl.BlockSpec((1, N_HEADS_GDN, DK_GDN, DV_GDN), lambda s, c: (init(s), 0, 0, 0)),
            pl.BlockSpec((1, CARRY, D_CONV), lambda s, c: (init(s), 0, 0)),
        ],
        out_specs=[
            pl.BlockSpec((C, D_GDN), lambda s, c: (s * n_chunks + c, 0)),
            pl.BlockSpec((1, N_HEADS_GDN, DK_GDN, DV_GDN), lambda s, c: (s, 0, 0, 0)),
        ],
        out_shape=[
            jax.ShapeDtypeStruct((n_seq * n_chunks * C, D_GDN), BF16),
            jax.ShapeDtypeStruct((n_seq, N_HEADS_GDN, DK_GDN, DV_GDN), F32),
        ],
        scratch_shapes=[
            pltpu.VMEM((N_HEADS_GDN, DK_GDN, DV_GDN), F32),
            pltpu.VMEM((CARRY + C, D_CONV), F32),
        ],
        compiler_params=pltpu.CompilerParams(dimension_semantics=("parallel", "arbitrary")),
        name="gdn",
    )(h_src, h_src, tail_src, conv_w, alog_row, dt_row, nw_row, s0, buf0)


def _out_body(a_ref, b_ref, w_ref, x_ref, g_ref, beta_ref, o_ref):
    mixed = _dot(a_ref[...], w_ref[0:D_SWA_Q, :]) + _dot(b_ref[...], w_ref[D_SWA_Q:, :])
    o_ref[...] = _layernorm_rows(ALPHA * x_ref[...] + mixed, g_ref[...], beta_ref[...])


def _out_proj(o_swa, o_gdn, w_out, x1, g, b, *, tm):
    rows = o_swa.shape[0]
    return pl.pallas_call(
        _out_body,
        grid=(rows // tm,),
        in_specs=[
            pl.BlockSpec((tm, D_SWA_Q), lambda i: (i, 0)),
            pl.BlockSpec((tm, D_GDN), lambda i: (i, 0)),
            pl.BlockSpec((D_MODEL, D_MODEL), lambda i: (0, 0), pipeline_mode=pl.Buffered(1)),
            pl.BlockSpec((tm, D_MODEL), lambda i: (i, 0)),
            pl.BlockSpec((1, D_MODEL), lambda i: (0, 0)),
            pl.BlockSpec((1, D_MODEL), lambda i: (0, 0)),
        ],
        out_specs=pl.BlockSpec((tm, D_MODEL), lambda i: (i, 0)),
        out_shape=jax.ShapeDtypeStruct((rows, D_MODEL), F32),
        compiler_params=pltpu.CompilerParams(
            dimension_semantics=("parallel",), vmem_limit_bytes=VMEM_LIMIT),
        name="proj_out",
    )(o_swa, o_gdn, w_out, x1, g, b)


def _rope_tables(pos):
    half = HEAD_DIM // 2
    inv = ROPE_THETA ** (-jnp.arange(half, dtype=F32) / half)
    ang = pos.astype(F32)[:, None] * inv[None, :]
    cos = jnp.cos(ang)
    sin = jnp.sin(ang)
    return jnp.concatenate([cos, cos], axis=1), jnp.concatenate([-sin, sin], axis=1)


def kernel(x_prompt, x_sample, cache_meta_k, cache_meta_v, cache_win_k, cache_win_v, state_conv, state_gdn,
           meta_tokens, ln_g, ln_b, ffn_w_gate, ffn_w_up, ffn_w_down, w_in, w_out, attn_sinks, conv_w,
           gdn_a_log, gdn_dt_bias, gdn_norm_w):
    l = 0
    wg1, wu1, wd1 = (w[l, 0].astype(BF16) for w in (ffn_w_gate, ffn_w_up, ffn_w_down))
    w_main = jnp.swapaxes(w_in[l], 0, 1)
    w_tail = jnp.pad(w_main[D_MAIN:], ((0, TAIL_W - 2 * N_HEADS_GDN), (0, 0))).astype(BF16)
    wo = w_out[l].astype(BF16)
    g1, g2, g3 = (ln_g[l, i][None, :] for i in range(3))
    b1, b2, b3 = (ln_b[l, i][None, :] for i in range(3))
    pad_tail = lambda v, off: jnp.pad(v.astype(F32), (off, TAIL_W - off - N_HEADS_GDN))[None, :]
    alog_row = pad_tail(gdn_a_log[l], N_HEADS_GDN)
    dt_row = pad_tail(gdn_dt_bias[l], N_HEADS_GDN)
    nw_row = gdn_norm_w[l].astype(F32)[None, :]
    cw = conv_w[l]
    sinks = attn_sinks[l].astype(F32)

    n_pad = N_SM_ROWS - N_SAMPLE_ROWS - N_META
    xp = x_prompt.reshape(N_PROMPT_ROWS, D_MODEL)
    xs = jnp.concatenate([x_sample.reshape(N_SAMPLE_ROWS, D_MODEL), meta_tokens.astype(F32),
                          jnp.zeros((n_pad, D_MODEL), F32)], axis=0)
    cos_p, sin_p = _rope_tables(jnp.tile(N_META + jnp.arange(SEQ, dtype=jnp.int32), BATCH))
    cos_s, sin_s = _rope_tables(jnp.concatenate([
        jnp.tile(N_META + PAST_LEN + jnp.arange(DEC_SEQ, dtype=jnp.int32), DEC_BATCH),
        jnp.arange(N_META, dtype=jnp.int32), jnp.zeros((n_pad,), jnp.int32)]))

    ffn1 = functools.partial(_ffn, wg=wg1, wu=wu1, wd=wd1, g=g1, b=b1, emit_bf16=True)
    x1_s, x1b_s = ffn1(xs, tm=SM_TILE)
    x1_p, x1b_p, wg2, wu2, wd2 = ffn1(xp, tm=512, cast_next=(ffn_w_gate, ffn_w_up, ffn_w_down))
    h_s, t_s, w_main_b = _proj(x1b_s, w_main, w_tail, cos_s, sin_s, tm=N_SM_ROWS, tn=PROJ_TN_F32,
                               emit_weights=True)
    h_p, t_p = _proj(x1b_p, w_main_b, w_tail, cos_p, sin_p, tm=1024, tn=PROJ_TN_BF16, emit_weights=False)
    meta_rows = slice(META_ROW0, META_ROW0 + N_META)

    o_swa_p = _attn_prompt(sinks, h_p, h_s)
    o_swa_s = _attn_sample(
        sinks, h_s,
        cache_meta_k[l].reshape(DEC_BATCH * N_META, D_SWA_KV), cache_meta_v[l].reshape(DEC_BATCH * N_META, D_SWA_KV),
        cache_win_k[l].reshape(DEC_BATCH * WINDOW, D_SWA_KV), cache_win_v[l].reshape(DEC_BATCH * WINDOW, D_SWA_KV))

    zero_s = jnp.zeros((1, N_HEADS_GDN, DK_GDN, DV_GDN), F32)
    zero_buf = jnp.zeros((1, CARRY, D_CONV), F32)
    gdn = functools.partial(_gdn, conv_w=cw, alog_row=alog_row, dt_row=dt_row, nw_row=nw_row)
    _, s_meta = gdn(h_s, t_s, s0=zero_s, buf0=zero_buf, n_seq=1, n_chunks=1, C=N_META, shared_init=True,
                    first_chunk=META_ROW0 // N_META)
    buf_meta = h_s[None, META_ROW0 + N_META - CARRY:META_ROW0 + N_META, COL_CONV:COL_CONV + D_CONV]
    o_gdn_p, s_prompt = gdn(h_p, t_p, s0=s_meta, buf0=buf_meta, n_seq=BATCH, n_chunks=CHUNKS_PER_SEQ, C=CHUNK,
                            shared_init=True)
    buf_s = jnp.pad(state_conv[l].astype(F32), ((0, 0), (CARRY - (CONV_WIDTH - 1), 0), (0, 0)))
    o_gdn_s, s_sample = gdn(h_s, t_s, s0=state_gdn[l].astype(F32), buf0=buf_s, n_seq=DEC_BATCH, n_chunks=1,
                            C=CHUNK, shared_init=False)

    x2_p = _out_proj(o_swa_p, o_gdn_p, wo, x1_p, g2, b2, tm=256)
    x2_s = _out_proj(o_swa_s, o_gdn_s, wo, x1_s, g2, b2, tm=256)
    ffn2 = functools.partial(_ffn, wg=wg2, wu=wu2, wd=wd2, g=g3, b=b3, emit_bf16=False)
    (y_p,) = ffn2(x2_p, tm=512)
    (y_s,) = ffn2(x2_s, tm=512)

    y_prompt = y_p.reshape(BATCH, SEQ, D_MODEL)
    y_sample = y_s.reshape(DEC_BATCH, DEC_SEQ, D_MODEL)
    k_meta = h_s[meta_rows, COL_K:COL_K + D_SWA_KV].reshape(N_META, N_KV_SWA, HEAD_DIM)
    v_meta = h_s[meta_rows, COL_V:COL_V + D_SWA_KV].reshape(N_META, N_KV_SWA, HEAD_DIM)
    p_meta_k = jnp.broadcast_to(k_meta[None, None], (1, BATCH, N_META, N_KV_SWA, HEAD_DIM))
    p_meta_v = jnp.broadcast_to(v_meta[None, None], (1, BATCH, N_META, N_KV_SWA, HEAD_DIM))
    hp = h_p.reshape(BATCH, SEQ, D_MAIN)
    sample_cols = lambda c0, width: h_s[:N_SAMPLE_ROWS, c0:c0 + width].reshape(DEC_BATCH, DEC_SEQ, width)
    p_win_k = hp[:, SEQ - WINDOW:, COL_K:COL_K + D_SWA_KV].reshape(1, BATCH, WINDOW, N_KV_SWA, HEAD_DIM)
    p_win_v = hp[:, SEQ - WINDOW:, COL_V:COL_V + D_SWA_KV].reshape(1, BATCH, WINDOW, N_KV_SWA, HEAD_DIM)
    p_conv = hp[:, SEQ - (CONV_WIDTH - 1):, COL_CONV:COL_CONV + D_CONV][None]
    p_gdn = s_prompt[None]
    s_win_k = sample_cols(COL_K, D_SWA_KV).reshape(1, DEC_BATCH, DEC_SEQ, N_KV_SWA, HEAD_DIM)
    s_win_v = sample_cols(COL_V, D_SWA_KV).reshape(1, DEC_BATCH, DEC_SEQ, N_KV_SWA, HEAD_DIM)
    s_conv = sample_cols(COL_CONV, D_CONV)[:, DEC_SEQ - (CONV_WIDTH - 1):][None]
    s_gdn = s_sample[None]
    return (y_prompt, y_sample, p_meta_k, p_meta_v, p_win_k, p_win_v, p_conv, p_gdn, s_win_k, s_win_v, s_conv,
            s_gdn)
```

```python
import functools

import jax
import jax.numpy as jnp
from jax import lax
from jax.experimental import pallas as pl
from jax.experimental.pallas import tpu as pltpu

D_MODEL = 4096
BATCH = 4
SEQ = 2048
DEC_BATCH = 16
DEC_SEQ = 64
PAST_LEN = 2048
CHUNK = 64
N_META = 16
WINDOW = 128
HEAD_DIM = 128
N_HEADS_SWA = 16
N_KV_SWA = 4
GQA_GROUP = 4
N_HEADS_GDN = 16
DK_GDN = 128
DV_GDN = 128
CONV_WIDTH = 4
D_SWA_Q = N_HEADS_SWA * HEAD_DIM
D_SWA_KV = N_KV_SWA * HEAD_DIM
D_GDN = N_HEADS_GDN * DK_GDN
D_CONV = 3 * D_GDN
D_FF = 11008
ROPE_THETA = 10000.0
LN_EPS = 1e-5
RMS_EPS = 1e-6
ALPHA = 2.0 ** 0.25

N_PROMPT_ROWS = BATCH * SEQ
N_SAMPLE_ROWS = DEC_BATCH * DEC_SEQ
CHUNKS_PER_SEQ = SEQ // CHUNK
META_ROW0 = N_SAMPLE_ROWS
SM_TILE = 528
N_SM_ROWS = 2 * SM_TILE

COL_CONV = 0
COL_Z = D_CONV
COL_Q = COL_Z + D_GDN
COL_K = COL_Q + D_SWA_Q
COL_V = COL_K + D_SWA_KV
D_MAIN = COL_V + D_SWA_KV
TAIL_W = 128

FFN_TF = 256
PROJ_TN_F32 = 512
PROJ_TN_BF16 = 1024
CARRY = 8

VMEM_LIMIT = 60 * 1024 * 1024

BF16 = jnp.bfloat16
F32 = jnp.float32
HI = lax.Precision.HIGHEST


def _dot(a, b, precision=None):
    return jnp.dot(a, b, preferred_element_type=F32, precision=precision)


def _dot_nt(a, b):
    return lax.dot_general(a, b, (((1,), (1,)), ((), ())), preferred_element_type=F32)


def _dot_tn(a, b):
    return lax.dot_general(a, b, (((0,), (0,)), ((), ())), preferred_element_type=F32)


def _head_dot(a, b, lhs_contract, rhs_contract):
    dims = (((lhs_contract,), (rhs_contract,)), ((0,), (0,)))
    return lax.dot_general(a.astype(BF16), b.astype(BF16), dims, preferred_element_type=F32)


_hdot = functools.partial(_head_dot, lhs_contract=2, rhs_contract=1)
_hdot_nt = functools.partial(_head_dot, lhs_contract=2, rhs_contract=2)
_hdot_tn = functools.partial(_head_dot, lhs_contract=1, rhs_contract=1)


def _layernorm_rows(y, g, b):
    mu = jnp.mean(y, axis=-1, keepdims=True)
    d = y - mu
    var = jnp.mean(d * d, axis=-1, keepdims=True)
    return d * lax.rsqrt(var + LN_EPS) * g + b


def _silu(x):
    return x * jax.nn.sigmoid(x)


def _ffn_body(x_ref, wg_ref, wu_ref, wd_ref, g_ref, b_ref, *rest, emit_bf16, n_cast):
    cast_in, rest = rest[:n_cast], rest[n_cast:]
    o_ref, rest = rest[0], rest[1:]
    if emit_bf16:
        ob_ref, rest = rest[0], rest[1:]
    cast_out, (xb_ref,) = rest[:n_cast], rest[n_cast:]
    f = pl.program_id(1)

    for src, dst in zip(cast_in, cast_out):
        dst[...] = src[...].astype(BF16)

    @pl.when(f == 0)
    def _():
        xb_ref[...] = x_ref[...].astype(BF16)
        o_ref[...] = jnp.zeros_like(o_ref)

    xb = xb_ref[...]
    hidden = _silu(_dot(xb, wg_ref[...])) * _dot(xb, wu_ref[...])
    o_ref[...] += _dot(hidden.astype(BF16), wd_ref[...])

    @pl.when(f == pl.num_programs(1) - 1)
    def _():
        y = _layernorm_rows(ALPHA * x_ref[...] + 0.5 * o_ref[...], g_ref[...], b_ref[...])
        o_ref[...] = y
        if emit_bf16:
            ob_ref[...] = y.astype(BF16)


def _ffn(x, wg, wu, wd, g, b, *, tm, emit_bf16, cast_next=None):
    rows = x.shape[0]
    grid = (rows // tm, D_FF // FFN_TF)
    row_spec = lambda: pl.BlockSpec((tm, D_MODEL), lambda i, f: (i, 0))
    x_mode = dict(pipeline_mode=pl.Buffered(1)) if emit_bf16 else {}
    in_specs = [
        pl.BlockSpec((tm, D_MODEL), lambda i, f: (i, 0), **x_mode),
        pl.BlockSpec((D_MODEL, FFN_TF), lambda i, f: (0, f)),
        pl.BlockSpec((D_MODEL, FFN_TF), lambda i, f: (0, f)),
        pl.BlockSpec((FFN_TF, D_MODEL), lambda i, f: (f, 0)),
        pl.BlockSpec((1, D_MODEL), lambda i, f: (0, 0)),
        pl.BlockSpec((1, D_MODEL), lambda i, f: (0, 0)),
    ]
    args = [x, wg, wu, wd, g, b]
    out_shape = [jax.ShapeDtypeStruct((rows, D_MODEL), F32)]
    out_specs = [row_spec()]
    if emit_bf16:
        out_shape.append(jax.ShapeDtypeStruct((rows, D_MODEL), BF16))
        out_specs.append(row_spec())
    n_cast = 0
    if cast_next is not None:
        assert grid[0] * FFN_TF == D_MODEL and FFN_TF % grid[0] == 0
        n_cast = 3
        up_map = lambda i, f: (0, 1, i, f)
        sq = (None, None, FFN_TF, FFN_TF)
        strip = FFN_TF // grid[0]
        down_map = lambda i, f: (0, 1, f * grid[0] + i, 0)
        in_specs += [pl.BlockSpec(sq, up_map), pl.BlockSpec(sq, up_map),
                     pl.BlockSpec((None, None, strip, D_MODEL), down_map)]
        args += list(cast_next)
        out_shape += [jax.ShapeDtypeStruct((D_MODEL, D_FF), BF16), jax.ShapeDtypeStruct((D_MODEL, D_FF), BF16),
                      jax.ShapeDtypeStruct((D_FF, D_MODEL), BF16)]
        out_specs += [pl.BlockSpec((FFN_TF, FFN_TF), lambda i, f: (i, f)),
                      pl.BlockSpec((FFN_TF, FFN_TF), lambda i, f: (i, f)),
                      pl.BlockSpec((strip, D_MODEL), lambda i, f: (f * grid[0] + i, 0))]
    return pl.pallas_call(
        functools.partial(_ffn_body, emit_bf16=emit_bf16, n_cast=n_cast),
        grid=grid,
        in_specs=in_specs,
        out_specs=out_specs,
        out_shape=out_shape,
        scratch_shapes=[pltpu.VMEM((tm, D_MODEL), BF16)],
        compiler_params=pltpu.CompilerParams(
            dimension_semantics=("parallel", "arbitrary"), vmem_limit_bytes=VMEM_LIMIT),
        name="ffn",
    )(*args)


def _proj_body(x_ref, w_ref, wt_ref, cos_ref, sin_ref, o_ref, t_ref, *wb_ref, tn):
    j = pl.program_id(1)
    wb = w_ref[...].astype(BF16)
    if wb_ref:
        wb_ref[0][...] = wb
    acc = _dot_nt(x_ref[...], wb)
    n_full, n_part = divmod((D_SWA_Q + D_SWA_KV) // HEAD_DIM, tn // HEAD_DIM)

    def store(n_rope_heads):
        cos = cos_ref[...]
        sin = sin_ref[...]
        for s in range(tn // HEAD_DIM):
            cols = slice(s * HEAD_DIM, (s + 1) * HEAD_DIM)
            blk = acc[:, cols]
            o_ref[:, cols] = blk * cos + pltpu.roll(blk, HEAD_DIM // 2, 1) * sin if s < n_rope_heads else blk

    pl.when(j < n_full)(lambda: store(tn // HEAD_DIM))
    pl.when(j == n_full)(lambda: store(n_part))

    @pl.when(j > n_full)
    def _():
        o_ref[...] = acc

    @pl.when(j == 0)
    def _():
        t_ref[...] = _dot_nt(x_ref[...], wt_ref[...])


def _proj(xb, w_t, w_tail, cos, sin, *, tm, tn, emit_weights):
    rows = xb.shape[0]
    grid = (rows // tm, D_MAIN // tn)
    n_swa = (D_SWA_Q + 2 * D_SWA_KV) // tn

    def dest(i, j):
        return i, jnp.where(j < n_swa, j + COL_Q // tn, j - n_swa)

    out_specs = [pl.BlockSpec((tm, tn), dest), pl.BlockSpec((tm, TAIL_W), lambda i, j: (i, 0))]
    out_shape = [jax.ShapeDtypeStruct((rows, D_MAIN), F32), jax.ShapeDtypeStruct((rows, TAIL_W), F32)]
    if emit_weights:
        assert grid[0] == 1
        out_specs.append(pl.BlockSpec((tn, D_MODEL), lambda i, j: (j, 0)))
        out_shape.append(jax.ShapeDtypeStruct((D_MAIN, D_MODEL), BF16))
    return pl.pallas_call(
        functools.partial(_proj_body, tn=tn),
        grid=grid,
        in_specs=[
            pl.BlockSpec((tm, D_MODEL), lambda i, j: (i, 0)),
            pl.BlockSpec((tn, D_MODEL), lambda i, j: (j, 0)),
            pl.BlockSpec((TAIL_W, D_MODEL), lambda i, j: (0, 0)),
            pl.BlockSpec((tm, HEAD_DIM), lambda i, j: (i, 0)),
            pl.BlockSpec((tm, HEAD_DIM), lambda i, j: (i, 0)),
        ],
        out_specs=out_specs,
        out_shape=out_shape,
        compiler_params=pltpu.CompilerParams(
            dimension_semantics=("parallel", "arbitrary"), vmem_limit_bytes=VMEM_LIMIT),
        name="proj_in",
    )(xb, w_t, w_tail, cos, sin)


N_KEYS = N_META + 3 * CHUNK


def _stack_heads(ref, rows, heads):
    return jnp.concatenate([ref[rows, h * HEAD_DIM:(h + 1) * HEAD_DIM] for h in heads], axis=0)


def _group_heads(g):
    return range(g * GQA_GROUP, (g + 1) * GQA_GROUP)


def _sink_rows(sink_ref, g):
    return jnp.concatenate([jnp.full((CHUNK, 1), sink_ref[h], F32) for h in _group_heads(g)], axis=0)


def _softmax_pv(q, k, v, sink, mask):
    s = _hdot_nt(q, k) * (HEAD_DIM ** -0.5)
    if mask is not None:
        s = jnp.where(mask, s, -jnp.inf)
    m = jnp.maximum(jnp.max(s, axis=-1, keepdims=True), sink)
    p = jnp.exp(s - m)
    den = jnp.sum(p, axis=-1, keepdims=True) + jnp.exp(sink - m)
    return _hdot(p, v) / den


def _attn_sample_body(sink_ref, q_ref, km_ref, vm_ref, k2_ref, v2_ref, k1_ref, v1_ref, k0_ref, v0_ref, o_ref):
    groups = range(N_KV_SWA)
    every = slice(None)
    k = jnp.stack([jnp.concatenate([_stack_heads(r, every, [g]) for r in (km_ref, k2_ref, k1_ref, k0_ref)], axis=0)
                   for g in groups])
    v = jnp.stack([jnp.concatenate([_stack_heads(r, every, [g]) for r in (vm_ref, v2_ref, v1_ref, v0_ref)], axis=0)
                   for g in groups])
    q = jnp.stack([_stack_heads(q_ref, every, _group_heads(g)) for g in groups])
    sink = jnp.stack([_sink_rows(sink_ref, g) for g in groups])
    o = _softmax_pv(q, k, v, sink, None)
    for g in groups:
        for j, h in enumerate(_group_heads(g)):
            o_ref[:, h * HEAD_DIM:(h + 1) * HEAD_DIM] = o[g, j * CHUNK:(j + 1) * CHUNK].astype(BF16)


def _attn_pair_body(sink_ref, q_ref, km_ref, vm_ref, kp_ref, vp_ref, kc_ref, vc_ref, o_ref):
    has_prev = pl.program_id(0) % (CHUNKS_PER_SEQ // 2) >= 1
    col = lax.broadcasted_iota(jnp.int32, (1, N_KEYS), 1)
    groups = range(N_KV_SWA)
    key_rows = ((slice(0, 2 * CHUNK), slice(0, CHUNK)), (slice(CHUNK, 2 * CHUNK), slice(0, 2 * CHUNK)))
    n_prev = (2 * CHUNK, CHUNK)
    q, k, v, sink, mask = [], [], [], [], []
    for sub in range(2):
        prev_rows, cur_rows = key_rows[sub]
        visible = jnp.logical_or(jnp.logical_or(col < N_META, col >= N_META + n_prev[sub]), has_prev)
        for g in groups:
            k.append(jnp.concatenate([_stack_heads(km_ref, slice(None), [g]), _stack_heads(kp_ref, prev_rows, [g]),
                                      _stack_heads(kc_ref, cur_rows, [g])], axis=0))
            v.append(jnp.concatenate([_stack_heads(vm_ref, slice(None), [g]), _stack_heads(vp_ref, prev_rows, [g]),
                                      _stack_heads(vc_ref, cur_rows, [g])], axis=0))
            q.append(_stack_heads(q_ref, slice(sub * CHUNK, (sub + 1) * CHUNK), _group_heads(g)))
            sink.append(_sink_rows(sink_ref, g))
            mask.append(visible)
    o = _softmax_pv(jnp.stack(q), jnp.stack(k), jnp.stack(v), jnp.stack(sink), jnp.stack(mask))
    for sub in range(2):
        for g in groups:
            for j, h in enumerate(_group_heads(g)):
                o_ref[sub * CHUNK:(sub + 1) * CHUNK, h * HEAD_DIM:(h + 1) * HEAD_DIM] = (
                    o[sub * N_KV_SWA + g, j * CHUNK:(j + 1) * CHUNK].astype(BF16))


def _attn_prompt(sinks, h_p, h_m):
    kcol = COL_K // D_SWA_KV
    vcol = COL_V // D_SWA_KV
    qcol = COL_Q // D_SWA_Q
    pairs_per_seq = CHUNKS_PER_SEQ // 2
    prev = lambda colblk: (lambda p: (p - jnp.where(p % pairs_per_seq >= 1, 1, 0), colblk))
    kv = lambda imap: pl.BlockSpec((2 * CHUNK, D_SWA_KV), imap)
    meta = lambda colblk: pl.BlockSpec((N_META, D_SWA_KV), lambda p: (META_ROW0 // N_META, colblk))
    return pl.pallas_call(
        _attn_pair_body,
        grid=(N_PROMPT_ROWS // (2 * CHUNK),),
        in_specs=[
            pl.BlockSpec(memory_space=pltpu.SMEM),
            pl.BlockSpec((2 * CHUNK, D_SWA_Q), lambda p: (p, qcol)),
            meta(kcol), meta(vcol),
            kv(prev(kcol)), kv(prev(vcol)),
            kv(lambda p: (p, kcol)), kv(lambda p: (p, vcol)),
        ],
        out_specs=pl.BlockSpec((2 * CHUNK, D_SWA_Q), lambda p: (p, 0)),
        out_shape=jax.ShapeDtypeStruct((N_PROMPT_ROWS, D_SWA_Q), BF16),
        compiler_params=pltpu.CompilerParams(dimension_semantics=("parallel",)),
        name="swa_prompt",
    )(sinks, h_p, h_m, h_m, h_p, h_p, h_p, h_p)


def _attn_sample(sinks, h_s, meta_k, meta_v, win_k, win_v):
    kcol = COL_K // D_SWA_KV
    vcol = COL_V // D_SWA_KV
    qcol = COL_Q // D_SWA_Q
    kv = lambda imap: pl.BlockSpec((CHUNK, D_SWA_KV), imap)
    return pl.pallas_call(
        _attn_sample_body,
        grid=(DEC_BATCH,),
        in_specs=[
            pl.BlockSpec(memory_space=pltpu.SMEM),
            pl.BlockSpec((CHUNK, D_SWA_Q), lambda s: (s, qcol)),
            pl.BlockSpec((N_META, D_SWA_KV), lambda s: (s, 0)),
            pl.BlockSpec((N_META, D_SWA_KV), lambda s: (s, 0)),
            kv(lambda s: (2 * s, 0)), kv(lambda s: (2 * s, 0)),
            kv(lambda s: (2 * s + 1, 0)), kv(lambda s: (2 * s + 1, 0)),
            kv(lambda s: (s, kcol)), kv(lambda s: (s, vcol)),
        ],
        out_specs=pl.BlockSpec((CHUNK, D_SWA_Q), lambda s: (s, 0)),
        out_shape=jax.ShapeDtypeStruct((N_SAMPLE_ROWS, D_SWA_Q), BF16),
        compiler_params=pltpu.CompilerParams(dimension_semantics=("parallel",)),
        name="swa_sample",
    )(sinks, h_s, meta_k, meta_v, win_k, win_v, win_k, win_v, h_s, h_s)


def _unit_lower_inverse(a, n):
    r = lax.broadcasted_iota(jnp.int32, (n, n), 0)
    c = lax.broadcasted_iota(jnp.int32, (n, n), 1)
    eye = (r == c).astype(F32)
    a0 = jnp.where((r >> 3) == (c >> 3), a, 0.0)
    a2 = _hdot(a0, a0)
    a4 = _hdot(a2, a2)
    x = _hdot(_hdot(eye - a0, eye + a2), eye + a4)
    shift = 3
    while (1 << shift) < n:
        pair = jnp.logical_and((r >> (shift + 1)) == (c >> (shift + 1)), (r >> shift) != (c >> shift))
        ak = jnp.where(pair, a, 0.0)
        x = x - _hdot(x, _hdot(ak, x))
        shift += 1
    return x


def _gdn_body(x_ref, z_ref, t_ref, w_ref, alog_ref, dt_ref, nw_ref, s0_ref, b0_ref, o_ref, sout_ref, s_scr,
              xe_scr, *, C, n_chunks):
    c = pl.program_id(1)

    @pl.when(c == 0)
    def _():
        s_scr[...] = s0_ref[0]
        xe_scr[0:CARRY, :] = b0_ref[0]

    @pl.when(c > 0)
    def _():
        xe_scr[0:CARRY, :] = xe_scr[C:C + CARRY, :]

    xe_scr[CARRY:CARRY + C, :] = x_ref[...]

    t = t_ref[...]
    beta = jax.nn.sigmoid(t)
    ta = t + dt_ref[...]
    softplus = jnp.maximum(ta, 0.0) + jnp.log(1.0 + jnp.exp(-jnp.abs(ta)))
    g = -jnp.exp(alog_ref[...]) * softplus
    r = lax.broadcasted_iota(jnp.int32, (C, C), 0)
    cc = lax.broadcasted_iota(jnp.int32, (C, C), 1)
    incl = r >= cc
    strict = r > cc
    gc = _dot(incl.astype(F32), g, HI)
    gc_t = gc.T
    heads = range(N_HEADS_GDN)
    bh = jnp.stack([beta[:, h:h + 1] for h in heads])
    gcol = jnp.stack([gc[:, N_HEADS_GDN + h:N_HEADS_GDN + h + 1] for h in heads])
    grow = jnp.stack([gc_t[N_HEADS_GDN + h:N_HEADS_GDN + h + 1, :] for h in heads])
    glast = gcol[:, C - 1:C, :]
    decay = jnp.where(incl, jnp.exp(jnp.where(incl, gcol - grow, 0.0)), 0.0)
    e_g = jnp.exp(gcol)

    xe = xe_scr[...]
    acc = w_ref[CONV_WIDTH - 1:CONV_WIDTH, :] * xe[CARRY:, :]
    for lag in range(1, CONV_WIDTH):
        tap = CONV_WIDTH - 1 - lag
        acc = acc + w_ref[tap:tap + 1, :] * pltpu.roll(xe, lag, 0)[CARRY:, :]
    y = _silu(acc)
    split = lambda base: jnp.stack([y[:, base + h * DK_GDN:base + (h + 1) * DK_GDN] for h in heads])
    q, k, v = split(0), split(D_GDN), split(2 * D_GDN)
    q = q * lax.rsqrt(jnp.sum(q * q, axis=-1, keepdims=True) + RMS_EPS) * (DK_GDN ** -0.5)
    k = k * lax.rsqrt(jnp.sum(k * k, axis=-1, keepdims=True) + RMS_EPS)
    kb = k * bh

    kq = _hdot_nt(jnp.concatenate([kb, q], axis=1), k)
    a = jnp.where(strict, kq[:, :C] * decay, 0.0)
    qk = kq[:, C:] * decay
    t_inv = _unit_lower_inverse(a, C)

    s_prev = s_scr[...]
    ws = _hdot(jnp.concatenate([kb * e_g, q * e_g], axis=1), s_prev)
    v_new = _hdot(t_inv, v * bh - ws[:, :C])
    o = ws[:, C:] + _hdot(qk, v_new)
    s_scr[...] = s_prev * jnp.exp(glast) + _hdot_tn(k * jnp.exp(glast - gcol), v_new)

    o = o * lax.rsqrt(jnp.mean(o * o, axis=-1, keepdims=True) + RMS_EPS) * nw_ref[...]
    for h in heads:
        hs = slice(h * DV_GDN, (h + 1) * DV_GDN)
        o_ref[:, hs] = (o[h] * _silu(z_ref[:, hs])).astype(BF16)

    @pl.when(c == n_chunks - 1)
    def _():
        sout_ref[0] = s_scr[...]


def _gdn(h_src, tail_src, conv_w, alog_row, dt_row, nw_row, s0, buf0, *, n_seq, n_chunks, C, shared_init,
         first_chunk=0):
    init = (lambda s: 0) if shared_init else (lambda s: s)
    vec = pl.BlockSpec((1, TAIL_W), lambda s, c: (0, 0))
    src = lambda s, c: first_chunk + s * n_chunks + c
    return pl.pallas_call(
        functools.partial(_gdn_body, C=C, n_chunks=n_chunks),
        grid=(n_seq, n_chunks),
        in_specs=[
            pl.BlockSpec((C, D_CONV), lambda s, c: (src(s, c), COL_CONV // D_CONV)),
            pl.BlockSpec((C, D_GDN), lambda s, c: (src(s, c), COL_Z // D_GDN)),
            pl.BlockSpec((C, TAIL_W), lambda s, c: (src(s, c), 0)),
            pl.BlockSpec((CONV_WIDTH, D_CONV), lambda s, c: (0, 0)),
            vec, vec, vec,
            pl.BlockSpec((1, N_HEADS_GDN, DK_GDN, DV_GDN), lambda s, c: (init(s), 0, 0, 0)),
            pl.BlockSpec((1, CARRY, D_CONV), lambda s, c: (init(s), 0, 0)),
        ],
        out_specs=[
            pl.BlockSpec((C, D_GDN), lambda s, c: (s * n_chunks + c, 0)),
            pl.BlockSpec((1, N_HEADS_GDN, DK_GDN, DV_GDN), lambda s, c: (s, 0, 0, 0)),
        ],
        out_shape=[
            jax.ShapeDtypeStruct((n_seq * n_chunks * C, D_GDN), BF16),
            jax.ShapeDtypeStruct((n_seq, N_HEADS_GDN, DK_GDN, DV_GDN), F32),
        ],
        scratch_shapes=[
            pltpu.VMEM((N_HEADS_GDN, DK_GDN, DV_GDN), F32),
            pltpu.VMEM((CARRY + C, D_CONV), F32),
        ],
        compiler_params=pltpu.CompilerParams(dimension_semantics=("parallel", "arbitrary")),
        name="gdn",
    )(h_src, h_src, tail_src, conv_w, alog_row, dt_row, nw_row, s0, buf0)


def _out_body(a_ref, b_ref, w_ref, x_ref, g_ref, beta_ref, o_ref):
    mixed = _dot(a_ref[...], w_ref[0:D_SWA_Q, :]) + _dot(b_ref[...], w_ref[D_SWA_Q:, :])
    o_ref[...] = _layernorm_rows(ALPHA * x_ref[...] + mixed, g_ref[...], beta_ref[...])


def _out_proj(o_swa, o_gdn, w_out, x1, g, b, *, tm):
    rows = o_swa.shape[0]
    return pl.pallas_call(
        _out_body,
        grid=(rows // tm,),
        in_specs=[
            pl.BlockSpec((tm, D_SWA_Q), lambda i: (i, 0)),
            pl.BlockSpec((tm, D_GDN), lambda i: (i, 0)),
            pl.BlockSpec((D_MODEL, D_MODEL), lambda i: (0, 0), pipeline_mode=pl.Buffered(1)),
            pl.BlockSpec((tm, D_MODEL), lambda i: (i, 0)),
            pl.BlockSpec((1, D_MODEL), lambda i: (0, 0)),
            pl.BlockSpec((1, D_MODEL), lambda i: (0, 0)),
        ],
        out_specs=pl.BlockSpec((tm, D_MODEL), lambda i: (i, 0)),
        out_shape=jax.ShapeDtypeStruct((rows, D_MODEL), F32),
        compiler_params=pltpu.CompilerParams(
            dimension_semantics=("parallel",), vmem_limit_bytes=VMEM_LIMIT),
        name="proj_out",
    )(o_swa, o_gdn, w_out, x1, g, b)


def _rope_tables(pos):
    half = HEAD_DIM // 2
    inv = ROPE_THETA ** (-jnp.arange(half, dtype=F32) / half)
    ang = pos.astype(F32)[:, None] * inv[None, :]
    cos = jnp.cos(ang)
    sin = jnp.sin(ang)
    return jnp.concatenate([cos, cos], axis=1), jnp.concatenate([-sin, sin], axis=1)


def kernel(x_prompt, x_sample, cache_meta_k, cache_meta_v, cache_win_k, cache_win_v, state_conv, state_gdn,
           meta_tokens, ln_g, ln_b, ffn_w_gate, ffn_w_up, ffn_w_down, w_in, w_out, attn_sinks, conv_w,
           gdn_a_log, gdn_dt_bias, gdn_norm_w):
    l = 0
    wg1, wu1, wd1 = (w[l, 0].astype(BF16) for w in (ffn_w_gate, ffn_w_up, ffn_w_down))
    w_main = jnp.swapaxes(w_in[l], 0, 1)
    w_tail = jnp.pad(w_main[D_MAIN:], ((0, TAIL_W - 2 * N_HEADS_GDN), (0, 0))).astype(BF16)
    wo = w_out[l].astype(BF16)
    g1, g2, g3 = (ln_g[l, i][None, :] for i in range(3))
    b1, b2, b3 = (ln_b[l, i][None, :] for i in range(3))
    pad_tail = lambda v, off: jnp.pad(v.astype(F32), (off, TAIL_W - off - N_HEADS_GDN))[None, :]
    alog_row = pad_tail(gdn_a_log[l], N_HEADS_GDN)
    dt_row = pad_tail(gdn_dt_bias[l], N_HEADS_GDN)
    nw_row = gdn_norm_w[l].astype(F32)[None, :]
    cw = conv_w[l]
    sinks = attn_sinks[l].astype(F32)

    n_pad = N_SM_ROWS - N_SAMPLE_ROWS - N_META
    xp = x_prompt.reshape(N_PROMPT_ROWS, D_MODEL)
    xs = jnp.concatenate([x_sample.reshape(N_SAMPLE_ROWS, D_MODEL), meta_tokens.astype(F32),
                          jnp.zeros((n_pad, D_MODEL), F32)], axis=0)
    cos_p, sin_p = _rope_tables(jnp.tile(N_META + jnp.arange(SEQ, dtype=jnp.int32), BATCH))
    cos_s, sin_s = _rope_tables(jnp.concatenate([
        jnp.tile(N_META + PAST_LEN + jnp.arange(DEC_SEQ, dtype=jnp.int32), DEC_BATCH),
        jnp.arange(N_META, dtype=jnp.int32), jnp.zeros((n_pad,), jnp.int32)]))

    ffn1 = functools.partial(_ffn, wg=wg1, wu=wu1, wd=wd1, g=g1, b=b1, emit_bf16=True)
    x1_s, x1b_s = ffn1(xs, tm=SM_TILE)
    x1_p, x1b_p, wg2, wu2, wd2 = ffn1(xp, tm=512, cast_next=(ffn_w_gate, ffn_w_up, ffn_w_down))
    h_s, t_s, w_main_b = _proj(x1b_s, w_main, w_tail, cos_s, sin_s, tm=N_SM_ROWS, tn=PROJ_TN_F32,
                               emit_weights=True)
    h_p, t_p = _proj(x1b_p, w_main_b, w_tail, cos_p, sin_p, tm=1024, tn=PROJ_TN_BF16, emit_weights=False)
    meta_rows = slice(META_ROW0, META_ROW0 + N_META)

    o_swa_p = _attn_prompt(sinks, h_p, h_s)
    o_swa_s = _attn_sample(
        sinks, h_s,
        cache_meta_k[l].reshape(DEC_BATCH * N_META, D_SWA_KV), cache_meta_v[l].reshape(DEC_BATCH * N_META, D_SWA_KV),
        cache_win_k[l].reshape(DEC_BATCH * WINDOW, D_SWA_KV), cache_win_v[l].reshape(DEC_BATCH * WINDOW, D_SWA_KV))

    zero_s = jnp.zeros((1, N_HEADS_GDN, DK_GDN, DV_GDN), F32)
    zero_buf = jnp.zeros((1, CARRY, D_CONV), F32)
    gdn = functools.partial(_gdn, conv_w=cw, alog_row=alog_row, dt_row=dt_row, nw_row=nw_row)
    _, s_meta = gdn(h_s, t_s, s0=zero_s, buf0=zero_buf, n_seq=1, n_chunks=1, C=N_META, shared_init=True,
                    first_chunk=META_ROW0 // N_META)
    buf_meta = h_s[None, META_ROW0 + N_META - CARRY:META_ROW0 + N_META, COL_CONV:COL_CONV + D_CONV]
    o_gdn_p, s_prompt = gdn(h_p, t_p, s0=s_meta, buf0=buf_meta, n_seq=BATCH, n_chunks=CHUNKS_PER_SEQ, C=CHUNK,
                            shared_init=True)
    buf_s = jnp.pad(state_conv[l].astype(F32), ((0, 0), (CARRY - (CONV_WIDTH - 1), 0), (0, 0)))
    o_gdn_s, s_sample = gdn(h_s, t_s, s0=state_gdn[l].astype(F32), buf0=buf_s, n_seq=DEC_BATCH, n_chunks=1,
                            C=CHUNK, shared_init=False)

    x2_p = _out_proj(o_swa_p, o_gdn_p, wo, x1_p, g2, b2, tm=256)
    x2_s = _out_proj(o_swa_s, o_gdn_s, wo, x1_s, g2, b2, tm=256)
    ffn2 = functools.partial(_ffn, wg=wg2, wu=wu2, wd=wd2, g=g3, b=b3, emit_bf16=False)
    (y_p,) = ffn2(x2_p, tm=512)
    (y_s,) = ffn2(x2_s, tm=512)

    y_prompt = y_p.reshape(BATCH, SEQ, D_MODEL)
    y_sample = y_s.reshape(DEC_BATCH, DEC_SEQ, D_MODEL)
    k_meta = h_s[meta_rows, COL_K:COL_K + D_SWA_KV].reshape(N_META, N_KV_SWA, HEAD_DIM)
    v_meta = h_s[meta_rows, COL_V:COL_V + D_SWA_KV].reshape(N_META, N_KV_SWA, HEAD_DIM)
    p_meta_k = jnp.broadcast_to(k_meta[None, None], (1, BATCH, N_META, N_KV_SWA, HEAD_DIM))
    p_meta_v = jnp.broadcast_to(v_meta[None, None], (1, BATCH, N_META, N_KV_SWA, HEAD_DIM))
    hp = h_p.reshape(BATCH, SEQ, D_MAIN)
    sample_cols = lambda c0, width: h_s[:N_SAMPLE_ROWS, c0:c0 + width].reshape(DEC_BATCH, DEC_SEQ, width)
    p_win_k = hp[:, SEQ - WINDOW:, COL_K:COL_K + D_SWA_KV].reshape(1, BATCH, WINDOW, N_KV_SWA, HEAD_DIM)
    p_win_v = hp[:, SEQ - WINDOW:, COL_V:COL_V + D_SWA_KV].reshape(1, BATCH, WINDOW, N_KV_SWA, HEAD_DIM)
    p_conv = hp[:, SEQ - (CONV_WIDTH - 1):, COL_CONV:COL_CONV + D_CONV][None]
    p_gdn = s_prompt[None]
    s_win_k = sample_cols(COL_K, D_SWA_KV).reshape(1, DEC_BATCH, DEC_SEQ, N_KV_SWA, HEAD_DIM)
    s_win_v = sample_cols(COL_V, D_SWA_KV).reshape(1, DEC_BATCH, DEC_SEQ, N_KV_SWA, HEAD_DIM)
    s_conv = sample_cols(COL_CONV, D_CONV)[:, DEC_SEQ - (CONV_WIDTH - 1):][None]
    s_gdn = s_sample[None]
    return (y_prompt, y_sample, p_meta_k, p_meta_v, p_win_k, p_win_v, p_conv, p_gdn, s_win_k, s_win_v, s_conv,
            s_gdn)
```

```python
import functools

import jax
import jax.numpy as jnp
from jax import lax
from jax.experimental import pallas as pl
from jax.experimental.pallas import tpu as pltpu

D_MODEL = 4096
BATCH = 4
SEQ = 2048
DEC_BATCH = 16
DEC_SEQ = 64
PAST_LEN = 2048
CHUNK = 64
N_META = 16
WINDOW = 128
HEAD_DIM = 128
N_HEADS_SWA = 16
N_KV_SWA = 4
GQA_GROUP = 4
N_HEADS_GDN = 16
DK_GDN = 128
DV_GDN = 128
CONV_WIDTH = 4
D_SWA_Q = N_HEADS_SWA * HEAD_DIM
D_SWA_KV = N_KV_SWA * HEAD_DIM
D_GDN = N_HEADS_GDN * DK_GDN
D_CONV = 3 * D_GDN
D_FF = 11008
ROPE_THETA = 10000.0
LN_EPS = 1e-5
RMS_EPS = 1e-6
ALPHA = 2.0 ** 0.25

N_PROMPT_ROWS = BATCH * SEQ
N_SAMPLE_ROWS = DEC_BATCH * DEC_SEQ
CHUNKS_PER_SEQ = SEQ // CHUNK
META_ROW0 = N_SAMPLE_ROWS
SM_TILE = 528
N_SM_ROWS = 2 * SM_TILE

COL_CONV = 0
COL_Z = D_CONV
COL_Q = COL_Z + D_GDN
COL_K = COL_Q + D_SWA_Q
COL_V = COL_K + D_SWA_KV
D_MAIN = COL_V + D_SWA_KV
TAIL_W = 128

FFN_TF = 256
LN_ROWS_CHOICES = (64, 48, 16, 8)
LN_UNROLL = 2
PROJ_TN_F32 = 512
PROJ_TN_BF16 = 1024
CARRY = 8

VMEM_LIMIT = 60 * 1024 * 1024

BF16 = jnp.bfloat16
F32 = jnp.float32
HI = lax.Precision.HIGHEST


def _dot(a, b, precision=None):
    return jnp.dot(a, b, preferred_element_type=F32, precision=precision)


def _dot_nt(a, b):
    return lax.dot_general(a, b, (((1,), (1,)), ((), ())), preferred_element_type=F32)


def _dot_tn(a, b):
    return lax.dot_general(a, b, (((0,), (0,)), ((), ())), preferred_element_type=F32)


def _head_dot(a, b, lhs_contract, rhs_contract):
    dims = (((lhs_contract,), (rhs_contract,)), ((0,), (0,)))
    return lax.dot_general(a.astype(BF16), b.astype(BF16), dims, preferred_element_type=F32)


_hdot = functools.partial(_head_dot, lhs_contract=2, rhs_contract=1)
_hdot_nt = functools.partial(_head_dot, lhs_contract=2, rhs_contract=2)
_hdot_tn = functools.partial(_head_dot, lhs_contract=1, rhs_contract=1)


def _layernorm_rows(y, g, b):
    mu = jnp.mean(y, axis=-1, keepdims=True)
    d = y - mu
    var = jnp.mean(d * d, axis=-1, keepdims=True)
    return d * lax.rsqrt(var + LN_EPS) * g + b


def _layernorm_into(o_ref, pre_norm_rows, g_ref, b_ref):
    n_rows = o_ref.shape[0]
    per_chunk = next(r for r in LN_ROWS_CHOICES if n_rows % r == 0)

    def chunk(c, carry):
        rows = pl.ds(pl.multiple_of(c * per_chunk, per_chunk), per_chunk)
        o_ref[rows, :] = _layernorm_rows(pre_norm_rows(rows), g_ref[...], b_ref[...])
        return carry

    lax.fori_loop(0, n_rows // per_chunk, chunk, 0, unroll=LN_UNROLL)


def _silu(x):
    return x * jax.nn.sigmoid(x)


def _ffn_body(x_ref, wg_ref, wu_ref, wd_ref, g_ref, b_ref, *rest, emit_bf16, n_cast):
    cast_in, rest = rest[:n_cast], rest[n_cast:]
    o_ref, rest = rest[0], rest[1:]
    if emit_bf16:
        ob_ref, rest = rest[0], rest[1:]
    cast_out, (xb_ref,) = rest[:n_cast], rest[n_cast:]
    f = pl.program_id(1)

    for src, dst in zip(cast_in, cast_out):
        dst[...] = src[...].astype(BF16)

    @pl.when(f == 0)
    def _():
        xb_ref[...] = x_ref[...].astype(BF16)
        o_ref[...] = jnp.zeros_like(o_ref)

    xb = xb_ref[...]
    hidden = _silu(_dot(xb, wg_ref[...])) * _dot(xb, wu_ref[...])
    o_ref[...] += _dot(hidden.astype(BF16), wd_ref[...])

    @pl.when(f == pl.num_programs(1) - 1)
    def _():
        _layernorm_into(o_ref, lambda rows: ALPHA * x_ref[rows, :] + 0.5 * o_ref[rows, :], g_ref, b_ref)
        if emit_bf16:
            ob_ref[...] = o_ref[...].astype(BF16)


def _ffn(x, wg, wu, wd, g, b, *, tm, emit_bf16, cast_next=None):
    rows = x.shape[0]
    grid = (rows // tm, D_FF // FFN_TF)
    row_spec = lambda: pl.BlockSpec((tm, D_MODEL), lambda i, f: (i, 0))
    in_specs = [
        pl.BlockSpec((tm, D_MODEL), lambda i, f: (i, 0)),
        pl.BlockSpec((D_MODEL, FFN_TF), lambda i, f: (0, f)),
        pl.BlockSpec((D_MODEL, FFN_TF), lambda i, f: (0, f)),
        pl.BlockSpec((FFN_TF, D_MODEL), lambda i, f: (f, 0)),
        pl.BlockSpec((1, D_MODEL), lambda i, f: (0, 0)),
        pl.BlockSpec((1, D_MODEL), lambda i, f: (0, 0)),
    ]
    args = [x, wg, wu, wd, g, b]
    out_shape = [jax.ShapeDtypeStruct((rows, D_MODEL), F32)]
    out_specs = [row_spec()]
    if emit_bf16:
        out_shape.append(jax.ShapeDtypeStruct((rows, D_MODEL), BF16))
        out_specs.append(row_spec())
    n_cast = 0
    if cast_next is not None:
        assert grid[0] * FFN_TF == D_MODEL and FFN_TF % grid[0] == 0
        n_cast = 3
        up_map = lambda i, f: (0, 1, i, f)
        sq = (None, None, FFN_TF, FFN_TF)
        strip = FFN_TF // grid[0]
        down_map = lambda i, f: (0, 1, f * grid[0] + i, 0)
        in_specs += [pl.BlockSpec(sq, up_map), pl.BlockSpec(sq, up_map),
                     pl.BlockSpec((None, None, strip, D_MODEL), down_map)]
        args += list(cast_next)
        out_shape += [jax.ShapeDtypeStruct((D_MODEL, D_FF), BF16), jax.ShapeDtypeStruct((D_MODEL, D_FF), BF16),
                      jax.ShapeDtypeStruct((D_FF, D_MODEL), BF16)]
        out_specs += [pl.BlockSpec((FFN_TF, FFN_TF), lambda i, f: (i, f)),
                      pl.BlockSpec((FFN_TF, FFN_TF), lambda i, f: (i, f)),
                      pl.BlockSpec((strip, D_MODEL), lambda i, f: (f * grid[0] + i, 0))]
    return pl.pallas_call(
        functools.partial(_ffn_body, emit_bf16=emit_bf16, n_cast=n_cast),
        grid=grid,
        in_specs=in_specs,
        out_specs=out_specs,
        out_shape=out_shape,
        scratch_shapes=[pltpu.VMEM((tm, D_MODEL), BF16)],
        compiler_params=pltpu.CompilerParams(
            dimension_semantics=("parallel", "arbitrary"), vmem_limit_bytes=VMEM_LIMIT),
        name="ffn",
    )(*args)


def _proj_body(x_ref, w_ref, wt_ref, cos_ref, sin_ref, o_ref, t_ref, *wb_ref, tn):
    j = pl.program_id(1)
    wb = w_ref[...].astype(BF16)
    if wb_ref:
        wb_ref[0][...] = wb
    acc = _dot_nt(x_ref[...], wb)
    n_full, n_part = divmod((D_SWA_Q + D_SWA_KV) // HEAD_DIM, tn // HEAD_DIM)

    def store(n_rope_heads):
        cos = cos_ref[...]
        sin = sin_ref[...]
        for s in range(tn // HEAD_DIM):
            cols = slice(s * HEAD_DIM, (s + 1) * HEAD_DIM)
            blk = acc[:, cols]
            o_ref[:, cols] = blk * cos + pltpu.roll(blk, HEAD_DIM // 2, 1) * sin if s < n_rope_heads else blk

    pl.when(j < n_full)(lambda: store(tn // HEAD_DIM))
    pl.when(j == n_full)(lambda: store(n_part))

    @pl.when(j > n_full)
    def _():
        o_ref[...] = acc

    @pl.when(j == 0)
    def _():
        t_ref[...] = _dot_nt(x_ref[...], wt_ref[...])


def _proj(xb, w_t, w_tail, cos, sin, *, tm, tn, emit_weights):
    rows = xb.shape[0]
    grid = (rows // tm, D_MAIN // tn)
    n_swa = (D_SWA_Q + 2 * D_SWA_KV) // tn

    def dest(i, j):
        return i, jnp.where(j < n_swa, j + COL_Q // tn, j - n_swa)

    out_specs = [pl.BlockSpec((tm, tn), dest), pl.BlockSpec((tm, TAIL_W), lambda i, j: (i, 0))]
    out_shape = [jax.ShapeDtypeStruct((rows, D_MAIN), F32), jax.ShapeDtypeStruct((rows, TAIL_W), F32)]
    if emit_weights:
        assert grid[0] == 1
        out_specs.append(pl.BlockSpec((tn, D_MODEL), lambda i, j: (j, 0)))
        out_shape.append(jax.ShapeDtypeStruct((D_MAIN, D_MODEL), BF16))
    return pl.pallas_call(
        functools.partial(_proj_body, tn=tn),
        grid=grid,
        in_specs=[
            pl.BlockSpec((tm, D_MODEL), lambda i, j: (i, 0)),
            pl.BlockSpec((tn, D_MODEL), lambda i, j: (j, 0)),
            pl.BlockSpec((TAIL_W, D_MODEL), lambda i, j: (0, 0)),
            pl.BlockSpec((tm, HEAD_DIM), lambda i, j: (i, 0)),
            pl.BlockSpec((tm, HEAD_DIM), lambda i, j: (i, 0)),
        ],
        out_specs=out_specs,
        out_shape=out_shape,
        compiler_params=pltpu.CompilerParams(
            dimension_semantics=("parallel", "arbitrary"), vmem_limit_bytes=VMEM_LIMIT),
        name="proj_in",
    )(xb, w_t, w_tail, cos, sin)


N_KEYS = N_META + 3 * CHUNK


def _stack_heads(ref, rows, heads):
    return jnp.concatenate([ref[rows, h * HEAD_DIM:(h + 1) * HEAD_DIM] for h in heads], axis=0)


def _group_heads(g):
    return range(g * GQA_GROUP, (g + 1) * GQA_GROUP)


def _sink_rows(sink_ref, g):
    return jnp.concatenate([jnp.full((CHUNK, 1), sink_ref[h], F32) for h in _group_heads(g)], axis=0)


def _softmax_pv(q, k, v, sink, mask):
    s = _hdot_nt(q, k) * (HEAD_DIM ** -0.5)
    if mask is not None:
        s = jnp.where(mask, s, -jnp.inf)
    m = jnp.maximum(jnp.max(s, axis=-1, keepdims=True), sink)
    p = jnp.exp(s - m)
    den = jnp.sum(p, axis=-1, keepdims=True) + jnp.exp(sink - m)
    return _hdot(p, v) / den


def _attn_sample_body(sink_ref, q_ref, km_ref, vm_ref, k2_ref, v2_ref, k1_ref, v1_ref, k0_ref, v0_ref, o_ref):
    groups = range(N_KV_SWA)
    every = slice(None)
    k = jnp.stack([jnp.concatenate([_stack_heads(r, every, [g]) for r in (km_ref, k2_ref, k1_ref, k0_ref)], axis=0)
                   for g in groups])
    v = jnp.stack([jnp.concatenate([_stack_heads(r, every, [g]) for r in (vm_ref, v2_ref, v1_ref, v0_ref)], axis=0)
                   for g in groups])
    q = jnp.stack([_stack_heads(q_ref, every, _group_heads(g)) for g in groups])
    sink = jnp.stack([_sink_rows(sink_ref, g) for g in groups])
    o = _softmax_pv(q, k, v, sink, None)
    for g in groups:
        for j, h in enumerate(_group_heads(g)):
            o_ref[:, h * HEAD_DIM:(h + 1) * HEAD_DIM] = o[g, j * CHUNK:(j + 1) * CHUNK].astype(BF16)


def _attn_pair_body(sink_ref, q_ref, km_ref, vm_ref, kp_ref, vp_ref, kc_ref, vc_ref, o_ref):
    has_prev = pl.program_id(0) % (CHUNKS_PER_SEQ // 2) >= 1
    col = lax.broadcasted_iota(jnp.int32, (1, N_KEYS), 1)
    groups = range(N_KV_SWA)
    key_rows = ((slice(0, 2 * CHUNK), slice(0, CHUNK)), (slice(CHUNK, 2 * CHUNK), slice(0, 2 * CHUNK)))
    n_prev = (2 * CHUNK, CHUNK)
    q, k, v, sink, mask = [], [], [], [], []
    for sub in range(2):
        prev_rows, cur_rows = key_rows[sub]
        visible = jnp.logical_or(jnp.logical_or(col < N_META, col >= N_META + n_prev[sub]), has_prev)
        for g in groups:
            k.append(jnp.concatenate([_stack_heads(km_ref, slice(None), [g]), _stack_heads(kp_ref, prev_rows, [g]),
                                      _stack_heads(kc_ref, cur_rows, [g])], axis=0))
            v.append(jnp.concatenate([_stack_heads(vm_ref, slice(None), [g]), _stack_heads(vp_ref, prev_rows, [g]),
                                      _stack_heads(vc_ref, cur_rows, [g])], axis=0))
            q.append(_stack_heads(q_ref, slice(sub * CHUNK, (sub + 1) * CHUNK), _group_heads(g)))
            sink.append(_sink_rows(sink_ref, g))
            mask.append(visible)
    o = _softmax_pv(jnp.stack(q), jnp.stack(k), jnp.stack(v), jnp.stack(sink), jnp.stack(mask))
    for sub in range(2):
        for g in groups:
            for j, h in enumerate(_group_heads(g)):
                o_ref[sub * CHUNK:(sub + 1) * CHUNK, h * HEAD_DIM:(h + 1) * HEAD_DIM] = (
                    o[sub * N_KV_SWA + g, j * CHUNK:(j + 1) * CHUNK].astype(BF16))


def _attn_prompt(sinks, h_p, h_m):
    kcol = COL_K // D_SWA_KV
    vcol = COL_V // D_SWA_KV
    qcol = COL_Q // D_SWA_Q
    pairs_per_seq = CHUNKS_PER_SEQ // 2
    prev = lambda colblk: (lambda p: (p - jnp.where(p % pairs_per_seq >= 1, 1, 0), colblk))
    kv = lambda imap: pl.BlockSpec((2 * CHUNK, D_SWA_KV), imap)
    meta = lambda colblk: pl.BlockSpec((N_META, D_SWA_KV), lambda p: (META_ROW0 // N_META, colblk))
    return pl.pallas_call(
        _attn_pair_body,
        grid=(N_PROMPT_ROWS // (2 * CHUNK),),
        in_specs=[
            pl.BlockSpec(memory_space=pltpu.SMEM),
            pl.BlockSpec((2 * CHUNK, D_SWA_Q), lambda p: (p, qcol)),
            meta(kcol), meta(vcol),
            kv(prev(kcol)), kv(prev(vcol)),
            kv(lambda p: (p, kcol)), kv(lambda p: (p, vcol)),
        ],
        out_specs=pl.BlockSpec((2 * CHUNK, D_SWA_Q), lambda p: (p, 0)),
        out_shape=jax.ShapeDtypeStruct((N_PROMPT_ROWS, D_SWA_Q), BF16),
        compiler_params=pltpu.CompilerParams(dimension_semantics=("parallel",)),
        name="swa_prompt",
    )(sinks, h_p, h_m, h_m, h_p, h_p, h_p, h_p)


def _attn_sample(sinks, h_s, meta_k, meta_v, win_k, win_v):
    kcol = COL_K // D_SWA_KV
    vcol = COL_V // D_SWA_KV
    qcol = COL_Q // D_SWA_Q
    kv = lambda imap: pl.BlockSpec((CHUNK, D_SWA_KV), imap)
    return pl.pallas_call(
        _attn_sample_body,
        grid=(DEC_BATCH,),
        in_specs=[
            pl.BlockSpec(memory_space=pltpu.SMEM),
            pl.BlockSpec((CHUNK, D_SWA_Q), lambda s: (s, qcol)),
            pl.BlockSpec((N_META, D_SWA_KV), lambda s: (s, 0)),
            pl.BlockSpec((N_META, D_SWA_KV), lambda s: (s, 0)),
            kv(lambda s: (2 * s, 0)), kv(lambda s: (2 * s, 0)),
            kv(lambda s: (2 * s + 1, 0)), kv(lambda s: (2 * s + 1, 0)),
            kv(lambda s: (s, kcol)), kv(lambda s: (s, vcol)),
        ],
        out_specs=pl.BlockSpec((CHUNK, D_SWA_Q), lambda s: (s, 0)),
        out_shape=jax.ShapeDtypeStruct((N_SAMPLE_ROWS, D_SWA_Q), BF16),
        compiler_params=pltpu.CompilerParams(dimension_semantics=("parallel",)),
        name="swa_sample",
    )(sinks, h_s, meta_k, meta_v, win_k, win_v, win_k, win_v, h_s, h_s)


def _unit_lower_inverse(a, n):
    r = lax.broadcasted_iota(jnp.int32, (n, n), 0)
    c = lax.broadcasted_iota(jnp.int32, (n, n), 1)
    eye = (r == c).astype(F32)
    a0 = jnp.where((r >> 3) == (c >> 3), a, 0.0)
    a2 = _hdot(a0, a0)
    a4 = _hdot(a2, a2)
    x = _hdot(_hdot(eye - a0, eye + a2), eye + a4)
    shift = 3
    while (1 << shift) < n:
        pair = jnp.logical_and((r >> (shift + 1)) == (c >> (shift + 1)), (r >> shift) != (c >> shift))
        ak = jnp.where(pair, a, 0.0)
        x = x - _hdot(x, _hdot(ak, x))
        shift += 1
    return x


def _gdn_body(x_ref, z_ref, t_ref, w_ref, alog_ref, dt_ref, nw_ref, s0_ref, b0_ref, o_ref, sout_ref, s_scr,
              xe_scr, *, C, n_chunks):
    c = pl.program_id(1)

    @pl.when(c == 0)
    def _():
        s_scr[...] = s0_ref[0]
        xe_scr[0:CARRY, :] = b0_ref[0]

    @pl.when(c > 0)
    def _():
        xe_scr[0:CARRY, :] = xe_scr[C:C + CARRY, :]

    xe_scr[CARRY:CARRY + C, :] = x_ref[...]

    t = t_ref[...]
    beta = jax.nn.sigmoid(t)
    ta = t + dt_ref[...]
    softplus = jnp.maximum(ta, 0.0) + jnp.log(1.0 + jnp.exp(-jnp.abs(ta)))
    g = -jnp.exp(alog_ref[...]) * softplus
    r = lax.broadcasted_iota(jnp.int32, (C, C), 0)
    cc = lax.broadcasted_iota(jnp.int32, (C, C), 1)
    incl = r >= cc
    strict = r > cc
    gc = _dot(incl.astype(F32), g, HI)
    gc_t = gc.T
    heads = range(N_HEADS_GDN)
    bh = jnp.stack([beta[:, h:h + 1] for h in heads])
    gcol = jnp.stack([gc[:, N_HEADS_GDN + h:N_HEADS_GDN + h + 1] for h in heads])
    grow = jnp.stack([gc_t[N_HEADS_GDN + h:N_HEADS_GDN + h + 1, :] for h in heads])
    glast = gcol[:, C - 1:C, :]
    decay = jnp.where(incl, jnp.exp(jnp.where(incl, gcol - grow, 0.0)), 0.0)
    e_g = jnp.exp(gcol)

    xe = xe_scr[...]
    acc = w_ref[CONV_WIDTH - 1:CONV_WIDTH, :] * xe[CARRY:, :]
    for lag in range(1, CONV_WIDTH):
        tap = CONV_WIDTH - 1 - lag
        acc = acc + w_ref[tap:tap + 1, :] * pltpu.roll(xe, lag, 0)[CARRY:, :]
    y = _silu(acc)
    split = lambda base: jnp.stack([y[:, base + h * DK_GDN:base + (h + 1) * DK_GDN] for h in heads])
    q, k, v = split(0), split(D_GDN), split(2 * D_GDN)
    q = q * lax.rsqrt(jnp.sum(q * q, axis=-1, keepdims=True) + RMS_EPS) * (DK_GDN ** -0.5)
    k = k * lax.rsqrt(jnp.sum(k * k, axis=-1, keepdims=True) + RMS_EPS)
    kb = k * bh

    kq = _hdot_nt(jnp.concatenate([kb, q], axis=1), k)
    a = jnp.where(strict, kq[:, :C] * decay, 0.0)
    qk = kq[:, C:] * decay
    t_inv = _unit_lower_inverse(a, C)

    s_prev = s_scr[...]
    ws = _hdot(jnp.concatenate([kb * e_g, q * e_g], axis=1), s_prev)
    v_new = _hdot(t_inv, v * bh - ws[:, :C])
    o = ws[:, C:] + _hdot(qk, v_new)
    s_scr[...] = s_prev * jnp.exp(glast) + _hdot_tn(k * jnp.exp(glast - gcol), v_new)

    o = o * lax.rsqrt(jnp.mean(o * o, axis=-1, keepdims=True) + RMS_EPS) * nw_ref[...]
    for h in heads:
        hs = slice(h * DV_GDN, (h + 1) * DV_GDN)
        o_ref[:, hs] = (o[h] * _silu(z_ref[:, hs])).astype(BF16)

    @pl.when(c == n_chunks - 1)
    def _():
        sout_ref[0] = s_scr[...]


def _gdn(h_src, tail_src, conv_w, alog_row, dt_row, nw_row, s0, buf0, *, n_seq, n_chunks, C, shared_init,
         first_chunk=0):
    init = (lambda s: 0) if shared_init else (lambda s: s)
    vec = pl.BlockSpec((1, TAIL_W), lambda s, c: (0, 0))
    src = lambda s, c: first_chunk + s * n_chunks + c
    return pl.pallas_call(
        functools.partial(_gdn_body, C=C, n_chunks=n_chunks),
        grid=(n_seq, n_chunks),
        in_specs=[
            pl.BlockSpec((C, D_CONV), lambda s, c: (src(s, c), COL_CONV // D_CONV)),
            pl.BlockSpec((C, D_GDN), lambda s, c: (src(s, c), COL_Z // D_GDN)),
            pl.BlockSpec((C, TAIL_W), lambda s, c: (src(s, c), 0)),
            pl.BlockSpec((CONV_WIDTH, D_CONV), lambda s, c: (0, 0)),
            vec, vec, vec,
            pl.BlockSpec((1, N_HEADS_GDN, DK_GDN, DV_GDN), lambda s, c: (init(s), 0, 0, 0)),
            pl.BlockSpec((1, CARRY, D_CONV), lambda s, c: (init(s), 0, 0)),
        ],
        out_specs=[
            pl.BlockSpec((C, D_GDN), lambda s, c: (s * n_chunks + c, 0)),
            pl.BlockSpec((1, N_HEADS_GDN, DK_GDN, DV_GDN), lambda s, c: (s, 0, 0, 0)),
        ],
        out_shape=[
            jax.ShapeDtypeStruct((n_seq * n_chunks * C, D_GDN), BF16),
            jax.ShapeDtypeStruct((n_seq, N_HEADS_GDN, DK_GDN, DV_GDN), F32),
        ],
        scratch_shapes=[
            pltpu.VMEM((N_HEADS_GDN, DK_GDN, DV_GDN), F32),
            pltpu.VMEM((CARRY + C, D_CONV), F32),
        ],
        compiler_params=pltpu.CompilerParams(dimension_semantics=("parallel", "arbitrary")),
        name="gdn",
    )(h_src, h_src, tail_src, conv_w, alog_row, dt_row, nw_row, s0, buf0)


def _out_body(a_ref, b_ref, w_ref, x_ref, g_ref, beta_ref, o_ref):
    o_ref[...] = ALPHA * x_ref[...] + _dot(a_ref[...], w_ref[0:D_SWA_Q, :]) + _dot(b_ref[...], w_ref[D_SWA_Q:, :])
    _layernorm_into(o_ref, lambda rows: o_ref[rows, :], g_ref, beta_ref)


def _out_proj(o_swa, o_gdn, w_out, x1, g, b, *, tm):
    rows = o_swa.shape[0]
    return pl.pallas_call(
        _out_body,
        grid=(rows // tm,),
        in_specs=[
            pl.BlockSpec((tm, D_SWA_Q), lambda i: (i, 0)),
            pl.BlockSpec((tm, D_GDN), lambda i: (i, 0)),
            pl.BlockSpec((D_MODEL, D_MODEL), lambda i: (0, 0), pipeline_mode=pl.Buffered(1)),
            pl.BlockSpec((tm, D_MODEL), lambda i: (i, 0)),
            pl.BlockSpec((1, D_MODEL), lambda i: (0, 0)),
            pl.BlockSpec((1, D_MODEL), lambda i: (0, 0)),
        ],
        out_specs=pl.BlockSpec((tm, D_MODEL), lambda i: (i, 0)),
        out_shape=jax.ShapeDtypeStruct((rows, D_MODEL), F32),
        compiler_params=pltpu.CompilerParams(
            dimension_semantics=("parallel",), vmem_limit_bytes=VMEM_LIMIT),
        name="proj_out",
    )(o_swa, o_gdn, w_out, x1, g, b)


def _rope_tables(pos):
    half = HEAD_DIM // 2
    inv = ROPE_THETA ** (-jnp.arange(half, dtype=F32) / half)
    ang = pos.astype(F32)[:, None] * inv[None, :]
    cos = jnp.cos(ang)
    sin = jnp.sin(ang)
    return jnp.concatenate([cos, cos], axis=1), jnp.concatenate([-sin, sin], axis=1)


def kernel(x_prompt, x_sample, cache_meta_k, cache_meta_v, cache_win_k, cache_win_v, state_conv, state_gdn,
           meta_tokens, ln_g, ln_b, ffn_w_gate, ffn_w_up, ffn_w_down, w_in, w_out, attn_sinks, conv_w,
           gdn_a_log, gdn_dt_bias, gdn_norm_w):
    l = 0
    wg1, wu1, wd1 = (w[l, 0].astype(BF16) for w in (ffn_w_gate, ffn_w_up, ffn_w_down))
    w_main = jnp.swapaxes(w_in[l], 0, 1)
    w_tail = jnp.pad(w_main[D_MAIN:], ((0, TAIL_W - 2 * N_HEADS_GDN), (0, 0))).astype(BF16)
    wo = w_out[l].astype(BF16)
    g1, g2, g3 = (ln_g[l, i][None, :] for i in range(3))
    b1, b2, b3 = (ln_b[l, i][None, :] for i in range(3))
    pad_tail = lambda v, off: jnp.pad(v.astype(F32), (off, TAIL_W - off - N_HEADS_GDN))[None, :]
    alog_row = pad_tail(gdn_a_log[l], N_HEADS_GDN)
    dt_row = pad_tail(gdn_dt_bias[l], N_HEADS_GDN)
    nw_row = gdn_norm_w[l].astype(F32)[None, :]
    cw = conv_w[l]
    sinks = attn_sinks[l].astype(F32)

    n_pad = N_SM_ROWS - N_SAMPLE_ROWS - N_META
    xp = x_prompt.reshape(N_PROMPT_ROWS, D_MODEL)
    xs = jnp.concatenate([x_sample.reshape(N_SAMPLE_ROWS, D_MODEL), meta_tokens.astype(F32),
                          jnp.zeros((n_pad, D_MODEL), F32)], axis=0)
    cos_p, sin_p = _rope_tables(jnp.tile(N_META + jnp.arange(SEQ, dtype=jnp.int32), BATCH))
    cos_s, sin_s = _rope_tables(jnp.concatenate([
        jnp.tile(N_META + PAST_LEN + jnp.arange(DEC_SEQ, dtype=jnp.int32), DEC_BATCH),
        jnp.arange(N_META, dtype=jnp.int32), jnp.zeros((n_pad,), jnp.int32)]))

    ffn1 = functools.partial(_ffn, wg=wg1, wu=wu1, wd=wd1, g=g1, b=b1, emit_bf16=True)
    x1_s, x1b_s = ffn1(xs, tm=SM_TILE)
    x1_p, x1b_p, wg2, wu2, wd2 = ffn1(xp, tm=512, cast_next=(ffn_w_gate, ffn_w_up, ffn_w_down))
    h_s, t_s, w_main_b = _proj(x1b_s, w_main, w_tail, cos_s, sin_s, tm=N_SM_ROWS, tn=PROJ_TN_F32,
                               emit_weights=True)
    h_p, t_p = _proj(x1b_p, w_main_b, w_tail, cos_p, sin_p, tm=1024, tn=PROJ_TN_BF16, emit_weights=False)
    meta_rows = slice(META_ROW0, META_ROW0 + N_META)

    o_swa_p = _attn_prompt(sinks, h_p, h_s)
    o_swa_s = _attn_sample(
        sinks, h_s,
        cache_meta_k[l].reshape(DEC_BATCH * N_META, D_SWA_KV), cache_meta_v[l].reshape(DEC_BATCH * N_META, D_SWA_KV),
        cache_win_k[l].reshape(DEC_BATCH * WINDOW, D_SWA_KV), cache_win_v[l].reshape(DEC_BATCH * WINDOW, D_SWA_KV))

    zero_s = jnp.zeros((1, N_HEADS_GDN, DK_GDN, DV_GDN), F32)
    zero_buf = jnp.zeros((1, CARRY, D_CONV), F32)
    gdn = functools.partial(_gdn, conv_w=cw, alog_row=alog_row, dt_row=dt_row, nw_row=nw_row)
    _, s_meta = gdn(h_s, t_s, s0=zero_s, buf0=zero_buf, n_seq=1, n_chunks=1, C=N_META, shared_init=True,
                    first_chunk=META_ROW0 // N_META)
    buf_meta = h_s[None, META_ROW0 + N_META - CARRY:META_ROW0 + N_META, COL_CONV:COL_CONV + D_CONV]
    o_gdn_p, s_prompt = gdn(h_p, t_p, s0=s_meta, buf0=buf_meta, n_seq=BATCH, n_chunks=CHUNKS_PER_SEQ, C=CHUNK,
                            shared_init=True)
    buf_s = jnp.pad(state_conv[l].astype(F32), ((0, 0), (CARRY - (CONV_WIDTH - 1), 0), (0, 0)))
    o_gdn_s, s_sample = gdn(h_s, t_s, s0=state_gdn[l].astype(F32), buf0=buf_s, n_seq=DEC_BATCH, n_chunks=1,
                            C=CHUNK, shared_init=False)

    x2_p = _out_proj(o_swa_p, o_gdn_p, wo, x1_p, g2, b2, tm=256)
    x2_s = _out_proj(o_swa_s, o_gdn_s, wo, x1_s, g2, b2, tm=256)
    ffn2 = functools.partial(_ffn, wg=wg2, wu=wu2, wd=wd2, g=g3, b=b3, emit_bf16=False)
    (y_p,) = ffn2(x2_p, tm=512)
    (y_s,) = ffn2(x2_s, tm=512)

    y_prompt = y_p.reshape(BATCH, SEQ, D_MODEL)
    y_sample = y_s.reshape(DEC_BATCH, DEC_SEQ, D_MODEL)
    k_meta = h_s[meta_rows, COL_K:COL_K + D_SWA_KV].reshape(N_META, N_KV_SWA, HEAD_DIM)
    v_meta = h_s[meta_rows, COL_V:COL_V + D_SWA_KV].reshape(N_META, N_KV_SWA, HEAD_DIM)
    p_meta_k = jnp.broadcast_to(k_meta[None, None], (1, BATCH, N_META, N_KV_SWA, HEAD_DIM))
    p_meta_v = jnp.broadcast_to(v_meta[None, None], (1, BATCH, N_META, N_KV_SWA, HEAD_DIM))
    hp = h_p.reshape(BATCH, SEQ, D_MAIN)
    sample_cols = lambda c0, width: h_s[:N_SAMPLE_ROWS, c0:c0 + width].reshape(DEC_BATCH, DEC_SEQ, width)
    p_win_k = hp[:, SEQ - WINDOW:, COL_K:COL_K + D_SWA_KV].reshape(1, BATCH, WINDOW, N_KV_SWA, HEAD_DIM)
    p_win_v = hp[:, SEQ - WINDOW:, COL_V:COL_V + D_SWA_KV].reshape(1, BATCH, WINDOW, N_KV_SWA, HEAD_DIM)
    p_conv = hp[:, SEQ - (CONV_WIDTH - 1):, COL_CONV:COL_CONV + D_CONV][None]
    p_gdn = s_prompt[None]
    s_win_k = sample_cols(COL_K, D_SWA_KV).reshape(1, DEC_BATCH, DEC_SEQ, N_KV_SWA, HEAD_DIM)
    s_win_v = sample_cols(COL_V, D_SWA_KV).reshape(1, DEC_BATCH, DEC_SEQ, N_KV_SWA, HEAD_DIM)
    s_conv = sample_cols(COL_CONV, D_CONV)[:, DEC_SEQ - (CONV_WIDTH - 1):][None]
    s_gdn = s_sample[None]
    return (y_prompt, y_sample, p_meta_k, p_meta_v, p_win_k, p_win_v, p_conv, p_gdn, s_win_k, s_win_v, s_conv,
            s_gdn)
```

```python
import functools

import jax
import jax.numpy as jnp
from jax import lax
from jax.experimental import pallas as pl
from jax.experimental.pallas import tpu as pltpu

D_MODEL = 4096
BATCH = 4
SEQ = 2048
DEC_BATCH = 16
DEC_SEQ = 64
PAST_LEN = 2048
CHUNK = 64
N_META = 16
WINDOW = 128
HEAD_DIM = 128
N_HEADS_SWA = 16
N_KV_SWA = 4
GQA_GROUP = 4
N_HEADS_GDN = 16
DK_GDN = 128
DV_GDN = 128
CONV_WIDTH = 4
D_SWA_Q = N_HEADS_SWA * HEAD_DIM
D_SWA_KV = N_KV_SWA * HEAD_DIM
D_GDN = N_HEADS_GDN * DK_GDN
D_CONV = 3 * D_GDN
D_FF = 11008
ROPE_THETA = 10000.0
LN_EPS = 1e-5
RMS_EPS = 1e-6
ALPHA = 2.0 ** 0.25

N_PROMPT_ROWS = BATCH * SEQ
N_SAMPLE_ROWS = DEC_BATCH * DEC_SEQ
CHUNKS_PER_SEQ = SEQ // CHUNK
META_ROW0 = N_SAMPLE_ROWS
SM_TILE = 528
N_SM_ROWS = 2 * SM_TILE

COL_CONV = 0
COL_Z = D_CONV
COL_Q = COL_Z + D_GDN
COL_K = COL_Q + D_SWA_Q
COL_V = COL_K + D_SWA_KV
D_MAIN = COL_V + D_SWA_KV
TAIL_W = 128

FFN_TF = 256
LN_ROWS_CHOICES = (64, 48, 16, 8)
LN_UNROLL = 2
PROJ_TN_F32 = 512
PROJ_TN_BF16 = 1024
CARRY = 8

V7X_VMEM_BYTES = 64 * 1024 * 1024
VMEM_LIMIT = V7X_VMEM_BYTES - 4 * 1024 * 1024

BF16 = jnp.bfloat16
F32 = jnp.float32
HI = lax.Precision.HIGHEST


def _dot(a, b, precision=None):
    return jnp.dot(a, b, preferred_element_type=F32, precision=precision)


def _dot_nt(a, b):
    return lax.dot_general(a, b, (((1,), (1,)), ((), ())), preferred_element_type=F32)


def _head_dot(a, b, lhs_contract, rhs_contract):
    dims = (((lhs_contract,), (rhs_contract,)), ((0,), (0,)))
    return lax.dot_general(a.astype(BF16), b.astype(BF16), dims, preferred_element_type=F32)


_hdot = functools.partial(_head_dot, lhs_contract=2, rhs_contract=1)
_hdot_nt = functools.partial(_head_dot, lhs_contract=2, rhs_contract=2)
_hdot_tn = functools.partial(_head_dot, lhs_contract=1, rhs_contract=1)


def _layernorm_rows(y, g, b):
    mu = jnp.mean(y, axis=-1, keepdims=True)
    d = y - mu
    var = jnp.mean(d * d, axis=-1, keepdims=True)
    return d * lax.rsqrt(var + LN_EPS) * g + b


def _layernorm_into(o_ref, pre_norm_rows, g_ref, b_ref):
    n_rows = o_ref.shape[0]
    per_chunk = next(r for r in LN_ROWS_CHOICES if n_rows % r == 0)

    def chunk(c, carry):
        rows = pl.ds(pl.multiple_of(c * per_chunk, per_chunk), per_chunk)
        o_ref[rows, :] = _layernorm_rows(pre_norm_rows(rows), g_ref[...], b_ref[...])
        return carry

    lax.fori_loop(0, n_rows // per_chunk, chunk, 0, unroll=LN_UNROLL)


def _silu(x):
    return x * jax.nn.sigmoid(x)


def _ffn_body(x_ref, wg_ref, wu_ref, wd_ref, g_ref, b_ref, *rest, emit_bf16, n_cast):
    cast_in, rest = rest[:n_cast], rest[n_cast:]
    o_ref, rest = rest[0], rest[1:]
    if emit_bf16:
        ob_ref, rest = rest[0], rest[1:]
    cast_out, (xb_ref,) = rest[:n_cast], rest[n_cast:]
    f = pl.program_id(1)

    for src, dst in zip(cast_in, cast_out):
        dst[...] = src[...].astype(BF16)

    @pl.when(f == 0)
    def _():
        xb_ref[...] = x_ref[...].astype(BF16)
        o_ref[...] = jnp.zeros_like(o_ref)

    xb = xb_ref[...]
    hidden = _silu(_dot(xb, wg_ref[...])) * _dot(xb, wu_ref[...])
    o_ref[...] += _dot(hidden.astype(BF16), wd_ref[...])

    @pl.when(f == pl.num_programs(1) - 1)
    def _():
        _layernorm_into(o_ref, lambda rows: ALPHA * x_ref[rows, :] + 0.5 * o_ref[rows, :], g_ref, b_ref)
        if emit_bf16:
            ob_ref[...] = o_ref[...].astype(BF16)


def _ffn(x, wg, wu, wd, g, b, *, tm, emit_bf16, cast_next=None):
    rows = x.shape[0]
    grid = (rows // tm, D_FF // FFN_TF)
    row_spec = lambda: pl.BlockSpec((tm, D_MODEL), lambda i, f: (i, 0))
    in_specs = [
        pl.BlockSpec((tm, D_MODEL), lambda i, f: (i, 0)),
        pl.BlockSpec((D_MODEL, FFN_TF), lambda i, f: (0, f)),
        pl.BlockSpec((D_MODEL, FFN_TF), lambda i, f: (0, f)),
        pl.BlockSpec((FFN_TF, D_MODEL), lambda i, f: (f, 0)),
        pl.BlockSpec((1, D_MODEL), lambda i, f: (0, 0)),
        pl.BlockSpec((1, D_MODEL), lambda i, f: (0, 0)),
    ]
    args = [x, wg, wu, wd, g, b]
    out_shape = [jax.ShapeDtypeStruct((rows, D_MODEL), F32)]
    out_specs = [row_spec()]
    if emit_bf16:
        out_shape.append(jax.ShapeDtypeStruct((rows, D_MODEL), BF16))
        out_specs.append(row_spec())
    n_cast = 0
    if cast_next is not None:
        assert grid[0] * FFN_TF == D_MODEL and FFN_TF % grid[0] == 0
        n_cast = 3
        up_map = lambda i, f: (0, 1, i, f)
        sq = (None, None, FFN_TF, FFN_TF)
        strip = FFN_TF // grid[0]
        down_map = lambda i, f: (0, 1, f * grid[0] + i, 0)
        in_specs += [pl.BlockSpec(sq, up_map), pl.BlockSpec(sq, up_map),
                     pl.BlockSpec((None, None, strip, D_MODEL), down_map)]
        args += list(cast_next)
        out_shape += [jax.ShapeDtypeStruct((D_MODEL, D_FF), BF16), jax.ShapeDtypeStruct((D_MODEL, D_FF), BF16),
                      jax.ShapeDtypeStruct((D_FF, D_MODEL), BF16)]
        out_specs += [pl.BlockSpec((FFN_TF, FFN_TF), lambda i, f: (i, f)),
                      pl.BlockSpec((FFN_TF, FFN_TF), lambda i, f: (i, f)),
                      pl.BlockSpec((strip, D_MODEL), lambda i, f: (f * grid[0] + i, 0))]
    return pl.pallas_call(
        functools.partial(_ffn_body, emit_bf16=emit_bf16, n_cast=n_cast),
        grid=grid,
        in_specs=in_specs,
        out_specs=out_specs,
        out_shape=out_shape,
        scratch_shapes=[pltpu.VMEM((tm, D_MODEL), BF16)],
        compiler_params=pltpu.CompilerParams(
            dimension_semantics=("parallel", "arbitrary"), vmem_limit_bytes=VMEM_LIMIT),
        name="ffn",
    )(*args)


def _proj_body(x_ref, w_ref, wt_ref, cos_ref, sin_ref, o_ref, t_ref, *wb_ref, tn):
    j = pl.program_id(1)
    wb = w_ref[...].astype(BF16)
    if wb_ref:
        wb_ref[0][...] = wb
    acc = _dot_nt(x_ref[...], wb)
    n_full, n_part = divmod((D_SWA_Q + D_SWA_KV) // HEAD_DIM, tn // HEAD_DIM)

    def store(n_rope_heads):
        cos = cos_ref[...]
        sin = sin_ref[...]
        for s in range(tn // HEAD_DIM):
            cols = slice(s * HEAD_DIM, (s + 1) * HEAD_DIM)
            blk = acc[:, cols]
            o_ref[:, cols] = blk * cos + pltpu.roll(blk, HEAD_DIM // 2, 1) * sin if s < n_rope_heads else blk

    pl.when(j < n_full)(lambda: store(tn // HEAD_DIM))
    pl.when(j == n_full)(lambda: store(n_part))

    @pl.when(j > n_full)
    def _():
        o_ref[...] = acc

    @pl.when(j == 0)
    def _():
        t_ref[...] = _dot_nt(x_ref[...], wt_ref[...])


def _proj(xb, w_t, w_tail, cos, sin, *, tm, tn, emit_weights):
    rows = xb.shape[0]
    grid = (rows // tm, D_MAIN // tn)
    n_swa = (D_SWA_Q + 2 * D_SWA_KV) // tn

    def dest(i, j):
        return i, jnp.where(j < n_swa, j + COL_Q // tn, j - n_swa)

    out_specs = [pl.BlockSpec((tm, tn), dest), pl.BlockSpec((tm, TAIL_W), lambda i, j: (i, 0))]
    out_shape = [jax.ShapeDtypeStruct((rows, D_MAIN), F32), jax.ShapeDtypeStruct((rows, TAIL_W), F32)]
    if emit_weights:
        assert grid[0] == 1
        out_specs.append(pl.BlockSpec((tn, D_MODEL), lambda i, j: (j, 0)))
        out_shape.append(jax.ShapeDtypeStruct((D_MAIN, D_MODEL), BF16))
    return pl.pallas_call(
        functools.partial(_proj_body, tn=tn),
        grid=grid,
        in_specs=[
            pl.BlockSpec((tm, D_MODEL), lambda i, j: (i, 0)),
            pl.BlockSpec((tn, D_MODEL), lambda i, j: (j, 0)),
            pl.BlockSpec((TAIL_W, D_MODEL), lambda i, j: (0, 0)),
            pl.BlockSpec((tm, HEAD_DIM), lambda i, j: (i, 0)),
            pl.BlockSpec((tm, HEAD_DIM), lambda i, j: (i, 0)),
        ],
        out_specs=out_specs,
        out_shape=out_shape,
        compiler_params=pltpu.CompilerParams(
            dimension_semantics=("parallel", "arbitrary"), vmem_limit_bytes=VMEM_LIMIT),
        name="proj_in",
    )(xb, w_t, w_tail, cos, sin)


N_KEYS = N_META + 3 * CHUNK


def _stack_heads(ref, rows, heads):
    return jnp.concatenate([ref[rows, h * HEAD_DIM:(h + 1) * HEAD_DIM] for h in heads], axis=0)


def _group_heads(g):
    return range(g * GQA_GROUP, (g + 1) * GQA_GROUP)


def _sink_rows(sink_ref, g):
    return jnp.concatenate([jnp.full((CHUNK, 1), sink_ref[h], F32) for h in _group_heads(g)], axis=0)


def _softmax_pv(q, k, v, sink, mask):
    s = _hdot_nt(q, k) * (HEAD_DIM ** -0.5)
    if mask is not None:
        s = jnp.where(mask, s, -jnp.inf)
    m = jnp.maximum(jnp.max(s, axis=-1, keepdims=True), sink)
    p = jnp.exp(s - m)
    den = jnp.sum(p, axis=-1, keepdims=True) + jnp.exp(sink - m)
    return _hdot(p, v) / den


def _attn_sample_body(sink_ref, q_ref, km_ref, vm_ref, k2_ref, v2_ref, k1_ref, v1_ref, k0_ref, v0_ref, o_ref):
    groups = range(N_KV_SWA)
    every = slice(None)
    k = jnp.stack([jnp.concatenate([_stack_heads(r, every, [g]) for r in (km_ref, k2_ref, k1_ref, k0_ref)], axis=0)
                   for g in groups])
    v = jnp.stack([jnp.concatenate([_stack_heads(r, every, [g]) for r in (vm_ref, v2_ref, v1_ref, v0_ref)], axis=0)
                   for g in groups])
    q = jnp.stack([_stack_heads(q_ref, every, _group_heads(g)) for g in groups])
    sink = jnp.stack([_sink_rows(sink_ref, g) for g in groups])
    o = _softmax_pv(q, k, v, sink, None)
    for g in groups:
        for j, h in enumerate(_group_heads(g)):
            o_ref[:, h * HEAD_DIM:(h + 1) * HEAD_DIM] = o[g, j * CHUNK:(j + 1) * CHUNK].astype(BF16)


def _attn_pair_body(sink_ref, q_ref, km_ref, vm_ref, kp_ref, vp_ref, kc_ref, vc_ref, o_ref):
    has_prev = pl.program_id(0) % (CHUNKS_PER_SEQ // 2) >= 1
    col = lax.broadcasted_iota(jnp.int32, (1, N_KEYS), 1)
    groups = range(N_KV_SWA)
    key_rows = ((slice(0, 2 * CHUNK), slice(0, CHUNK)), (slice(CHUNK, 2 * CHUNK), slice(0, 2 * CHUNK)))
    n_prev = (2 * CHUNK, CHUNK)
    q, k, v, sink, mask = [], [], [], [], []
    for sub in range(2):
        prev_rows, cur_rows = key_rows[sub]
        visible = jnp.logical_or(jnp.logical_or(col < N_META, col >= N_META + n_prev[sub]), has_prev)
        for g in groups:
            k.append(jnp.concatenate([_stack_heads(km_ref, slice(None), [g]), _stack_heads(kp_ref, prev_rows, [g]),
                                      _stack_heads(kc_ref, cur_rows, [g])], axis=0))
            v.append(jnp.concatenate([_stack_heads(vm_ref, slice(None), [g]), _stack_heads(vp_ref, prev_rows, [g]),
                                      _stack_heads(vc_ref, cur_rows, [g])], axis=0))
            q.append(_stack_heads(q_ref, slice(sub * CHUNK, (sub + 1) * CHUNK), _group_heads(g)))
            sink.append(_sink_rows(sink_ref, g))
            mask.append(visible)
    o = _softmax_pv(jnp.stack(q), jnp.stack(k), jnp.stack(v), jnp.stack(sink), jnp.stack(mask))
    for sub in range(2):
        for g in groups:
            for j, h in enumerate(_group_heads(g)):
                o_ref[sub * CHUNK:(sub + 1) * CHUNK, h * HEAD_DIM:(h + 1) * HEAD_DIM] = (
                    o[sub * N_KV_SWA + g, j * CHUNK:(j + 1) * CHUNK].astype(BF16))


def _attn_prompt(sinks, h_p, h_m):
    kcol = COL_K // D_SWA_KV
    vcol = COL_V // D_SWA_KV
    qcol = COL_Q // D_SWA_Q
    pairs_per_seq = CHUNKS_PER_SEQ // 2
    prev = lambda colblk: (lambda p: (p - jnp.where(p % pairs_per_seq >= 1, 1, 0), colblk))
    kv = lambda imap: pl.BlockSpec((2 * CHUNK, D_SWA_KV), imap)
    meta = lambda colblk: pl.BlockSpec((N_META, D_SWA_KV), lambda p: (META_ROW0 // N_META, colblk))
    return pl.pallas_call(
        _attn_pair_body,
        grid=(N_PROMPT_ROWS // (2 * CHUNK),),
        in_specs=[
            pl.BlockSpec(memory_space=pltpu.SMEM),
            pl.BlockSpec((2 * CHUNK, D_SWA_Q), lambda p: (p, qcol)),
            meta(kcol), meta(vcol),
            kv(prev(kcol)), kv(prev(vcol)),
            kv(lambda p: (p, kcol)), kv(lambda p: (p, vcol)),
        ],
        out_specs=pl.BlockSpec((2 * CHUNK, D_SWA_Q), lambda p: (p, 0)),
        out_shape=jax.ShapeDtypeStruct((N_PROMPT_ROWS, D_SWA_Q), BF16),
        compiler_params=pltpu.CompilerParams(dimension_semantics=("parallel",)),
        name="swa_prompt",
    )(sinks, h_p, h_m, h_m, h_p, h_p, h_p, h_p)


def _attn_sample(sinks, h_s, meta_k, meta_v, win_k, win_v):
    kcol = COL_K // D_SWA_KV
    vcol = COL_V // D_SWA_KV
    qcol = COL_Q // D_SWA_Q
    kv = lambda imap: pl.BlockSpec((CHUNK, D_SWA_KV), imap)
    return pl.pallas_call(
        _attn_sample_body,
        grid=(DEC_BATCH,),
        in_specs=[
            pl.BlockSpec(memory_space=pltpu.SMEM),
            pl.BlockSpec((CHUNK, D_SWA_Q), lambda s: (s, qcol)),
            pl.BlockSpec((N_META, D_SWA_KV), lambda s: (s, 0)),
            pl.BlockSpec((N_META, D_SWA_KV), lambda s: (s, 0)),
            kv(lambda s: (2 * s, 0)), kv(lambda s: (2 * s, 0)),
            kv(lambda s: (2 * s + 1, 0)), kv(lambda s: (2 * s + 1, 0)),
            kv(lambda s: (s, kcol)), kv(lambda s: (s, vcol)),
        ],
        out_specs=pl.BlockSpec((CHUNK, D_SWA_Q), lambda s: (s, 0)),
        out_shape=jax.ShapeDtypeStruct((N_SAMPLE_ROWS, D_SWA_Q), BF16),
        compiler_params=pltpu.CompilerParams(dimension_semantics=("parallel",)),
        name="swa_sample",
    )(sinks, h_s, meta_k, meta_v, win_k, win_v, win_k, win_v, h_s, h_s)


def _unit_lower_inverse(a, n):
    r = lax.broadcasted_iota(jnp.int32, (n, n), 0)
    c = lax.broadcasted_iota(jnp.int32, (n, n), 1)
    eye = (r == c).astype(F32)
    base = 3
    a0 = jnp.where((r >> base) == (c >> base), a, 0.0)
    a2 = _hdot(a0, a0)
    a4 = _hdot(a2, a2)
    x = _hdot(_hdot(eye - a0, eye + a2), eye + a4)
    shift = base
    while (1 << shift) < n:
        pair = jnp.logical_and((r >> (shift + 1)) == (c >> (shift + 1)), (r >> shift) != (c >> shift))
        ak = jnp.where(pair, a, 0.0)
        x = x - _hdot(x, _hdot(ak, x))
        shift += 1
    return x


def _gdn_body(x_ref, z_ref, t_ref, w_ref, alog_ref, dt_ref, nw_ref, s0_ref, b0_ref, o_ref, sout_ref, s_scr,
              xe_scr, *, C, n_chunks):
    c = pl.program_id(1)

    @pl.when(c == 0)
    def _():
        s_scr[...] = s0_ref[0]
        xe_scr[0:CARRY, :] = b0_ref[0]

    @pl.when(c > 0)
    def _():
        xe_scr[0:CARRY, :] = xe_scr[C:C + CARRY, :]

    xe_scr[CARRY:CARRY + C, :] = x_ref[...]

    t = t_ref[...]
    beta = jax.nn.sigmoid(t)
    ta = t + dt_ref[...]
    softplus = jnp.maximum(ta, 0.0) + jnp.log(1.0 + jnp.exp(-jnp.abs(ta)))
    g = -jnp.exp(alog_ref[...]) * softplus
    r = lax.broadcasted_iota(jnp.int32, (C, C), 0)
    cc = lax.broadcasted_iota(jnp.int32, (C, C), 1)
    incl = r >= cc
    strict = r > cc
    gc = _dot(incl.astype(F32), g, HI)
    gc_t = gc.T
    heads = range(N_HEADS_GDN)
    bh = jnp.stack([beta[:, h:h + 1] for h in heads])
    gcol = jnp.stack([gc[:, N_HEADS_GDN + h:N_HEADS_GDN + h + 1] for h in heads])
    grow = jnp.stack([gc_t[N_HEADS_GDN + h:N_HEADS_GDN + h + 1, :] for h in heads])
    glast = gcol[:, C - 1:C, :]
    decay = jnp.where(incl, jnp.exp(jnp.where(incl, gcol - grow, 0.0)), 0.0)
    e_g = jnp.exp(gcol)

    xe = xe_scr[...]
    acc = w_ref[CONV_WIDTH - 1:CONV_WIDTH, :] * xe[CARRY:, :]
    for lag in range(1, CONV_WIDTH):
        tap = CONV_WIDTH - 1 - lag
        acc = acc + w_ref[tap:tap + 1, :] * pltpu.roll(xe, lag, 0)[CARRY:, :]
    y = _silu(acc)
    split = lambda base: jnp.stack([y[:, base + h * DK_GDN:base + (h + 1) * DK_GDN] for h in heads])
    q, k, v = split(0), split(D_GDN), split(2 * D_GDN)
    q = q * lax.rsqrt(jnp.sum(q * q, axis=-1, keepdims=True) + RMS_EPS) * (DK_GDN ** -0.5)
    k = k * lax.rsqrt(jnp.sum(k * k, axis=-1, keepdims=True) + RMS_EPS)
    kb = k * bh

    kq = _hdot_nt(jnp.concatenate([kb, q], axis=1), k)
    a = jnp.where(strict, kq[:, :C] * decay, 0.0)
    qk = kq[:, C:] * decay
    t_inv = _unit_lower_inverse(a, C)

    s_prev = s_scr[...]
    ws = _hdot(jnp.concatenate([kb * e_g, q * e_g], axis=1), s_prev)
    v_new = _hdot(t_inv, v * bh - ws[:, :C])
    o = ws[:, C:] + _hdot(qk, v_new)
    s_scr[...] = s_prev * jnp.exp(glast) + _hdot_tn(k * jnp.exp(glast - gcol), v_new)

    o = o * lax.rsqrt(jnp.mean(o * o, axis=-1, keepdims=True) + RMS_EPS) * nw_ref[...]
    for h in heads:
        hs = slice(h * DV_GDN, (h + 1) * DV_GDN)
        o_ref[:, hs] = (o[h] * _silu(z_ref[:, hs])).astype(BF16)

    @pl.when(c == n_chunks - 1)
    def _():
        sout_ref[0] = s_scr[...]


def _gdn(h_src, tail_src, conv_w, alog_row, dt_row, nw_row, s0, buf0, *, n_seq, n_chunks, C, shared_init,
         first_chunk=0):
    init = (lambda s: 0) if shared_init else (lambda s: s)
    vec = pl.BlockSpec((1, TAIL_W), lambda s, c: (0, 0))
    src = lambda s, c: first_chunk + s * n_chunks + c
    return pl.pallas_call(
        functools.partial(_gdn_body, C=C, n_chunks=n_chunks),
        grid=(n_seq, n_chunks),
        in_specs=[
            pl.BlockSpec((C, D_CONV), lambda s, c: (src(s, c), COL_CONV // D_CONV)),
            pl.BlockSpec((C, D_GDN), lambda s, c: (src(s, c), COL_Z // D_GDN)),
            pl.BlockSpec((C, TAIL_W), lambda s, c: (src(s, c), 0)),
            pl.BlockSpec((CONV_WIDTH, D_CONV), lambda s, c: (0, 0)),
            vec, vec, vec,
            pl.BlockSpec((1, N_HEADS_GDN, DK_GDN, DV_GDN), lambda s, c: (init(s), 0, 0, 0)),
            pl.BlockSpec((1, CARRY, D_CONV), lambda s, c: (init(s), 0, 0)),
        ],
        out_specs=[
            pl.BlockSpec((C, D_GDN), lambda s, c: (s * n_chunks + c, 0)),
            pl.BlockSpec((1, N_HEADS_GDN, DK_GDN, DV_GDN), lambda s, c: (s, 0, 0, 0)),
        ],
        out_shape=[
            jax.ShapeDtypeStruct((n_seq * n_chunks * C, D_GDN), BF16),
            jax.ShapeDtypeStruct((n_seq, N_HEADS_GDN, DK_GDN, DV_GDN), F32),
        ],
        scratch_shapes=[
            pltpu.VMEM((N_HEADS_GDN, DK_GDN, DV_GDN), F32),
            pltpu.VMEM((CARRY + C, D_CONV), F32),
        ],
        compiler_params=pltpu.CompilerParams(dimension_semantics=("parallel", "arbitrary")),
        name="gdn",
    )(h_src, h_src, tail_src, conv_w, alog_row, dt_row, nw_row, s0, buf0)


def _out_body(a_ref, b_ref, w_ref, x_ref, g_ref, beta_ref, o_ref):
    o_ref[...] = ALPHA * x_ref[...] + _dot(a_ref[...], w_ref[0:D_SWA_Q, :]) + _dot(b_ref[...], w_ref[D_SWA_Q:, :])
    _layernorm_into(o_ref, lambda rows: o_ref[rows, :], g_ref, beta_ref)


def _out_proj(o_swa, o_gdn, w_out, x1, g, b, *, tm):
    rows = o_swa.shape[0]
    return pl.pallas_call(
        _out_body,
        grid=(rows // tm,),
        in_specs=[
            pl.BlockSpec((tm, D_SWA_Q), lambda i: (i, 0)),
            pl.BlockSpec((tm, D_GDN), lambda i: (i, 0)),
            pl.BlockSpec((D_MODEL, D_MODEL), lambda i: (0, 0), pipeline_mode=pl.Buffered(1)),
            pl.BlockSpec((tm, D_MODEL), lambda i: (i, 0)),
            pl.BlockSpec((1, D_MODEL), lambda i: (0, 0)),
            pl.BlockSpec((1, D_MODEL), lambda i: (0, 0)),
        ],
        out_specs=pl.BlockSpec((tm, D_MODEL), lambda i: (i, 0)),
        out_shape=jax.ShapeDtypeStruct((rows, D_MODEL), F32),
        compiler_params=pltpu.CompilerParams(
            dimension_semantics=("parallel",), vmem_limit_bytes=VMEM_LIMIT),
        name="proj_out",
    )(o_swa, o_gdn, w_out, x1, g, b)


def _rope_tables(pos):
    half = HEAD_DIM // 2
    inv = ROPE_THETA ** (-jnp.arange(half, dtype=F32) / half)
    ang = pos.astype(F32)[:, None] * inv[None, :]
    cos = jnp.cos(ang)
    sin = jnp.sin(ang)
    return jnp.concatenate([cos, cos], axis=1), jnp.concatenate([-sin, sin], axis=1)


def kernel(x_prompt, x_sample, cache_meta_k, cache_meta_v, cache_win_k, cache_win_v, state_conv, state_gdn,
           meta_tokens, ln_g, ln_b, ffn_w_gate, ffn_w_up, ffn_w_down, w_in, w_out, attn_sinks, conv_w,
           gdn_a_log, gdn_dt_bias, gdn_norm_w):
    l = 0
    wg1, wu1, wd1 = (w[l, 0].astype(BF16) for w in (ffn_w_gate, ffn_w_up, ffn_w_down))
    w_main = jnp.swapaxes(w_in[l], 0, 1)
    w_tail = jnp.pad(w_main[D_MAIN:], ((0, TAIL_W - 2 * N_HEADS_GDN), (0, 0))).astype(BF16)
    wo = w_out[l].astype(BF16)
    g1, g2, g3 = (ln_g[l, i][None, :] for i in range(3))
    b1, b2, b3 = (ln_b[l, i][None, :] for i in range(3))
    pad_tail = lambda v, off: jnp.pad(v.astype(F32), (off, TAIL_W - off - N_HEADS_GDN))[None, :]
    alog_row = pad_tail(gdn_a_log[l], N_HEADS_GDN)
    dt_row = pad_tail(gdn_dt_bias[l], N_HEADS_GDN)
    nw_row = gdn_norm_w[l].astype(F32)[None, :]
    cw = conv_w[l]
    sinks = attn_sinks[l].astype(F32)

    n_pad = N_SM_ROWS - N_SAMPLE_ROWS - N_META
    xp = x_prompt.reshape(N_PROMPT_ROWS, D_MODEL)
    xs = jnp.concatenate([x_sample.reshape(N_SAMPLE_ROWS, D_MODEL), meta_tokens.astype(F32),
                          jnp.zeros((n_pad, D_MODEL), F32)], axis=0)
    cos_p, sin_p = _rope_tables(jnp.tile(N_META + jnp.arange(SEQ, dtype=jnp.int32), BATCH))
    cos_s, sin_s = _rope_tables(jnp.concatenate([
        jnp.tile(N_META + PAST_LEN + jnp.arange(DEC_SEQ, dtype=jnp.int32), DEC_BATCH),
        jnp.arange(N_META, dtype=jnp.int32), jnp.zeros((n_pad,), jnp.int32)]))

    ffn1 = functools.partial(_ffn, wg=wg1, wu=wu1, wd=wd1, g=g1, b=b1, emit_bf16=True)
    x1_s, x1b_s = ffn1(xs, tm=SM_TILE)
    x1_p, x1b_p, wg2, wu2, wd2 = ffn1(xp, tm=512, cast_next=(ffn_w_gate, ffn_w_up, ffn_w_down))
    h_s, t_s, w_main_b = _proj(x1b_s, w_main, w_tail, cos_s, sin_s, tm=N_SM_ROWS, tn=PROJ_TN_F32,
                               emit_weights=True)
    h_p, t_p = _proj(x1b_p, w_main_b, w_tail, cos_p, sin_p, tm=1024, tn=PROJ_TN_BF16, emit_weights=False)
    meta_rows = slice(META_ROW0, META_ROW0 + N_META)

    o_swa_p = _attn_prompt(sinks, h_p, h_s)
    o_swa_s = _attn_sample(
        sinks, h_s,
        cache_meta_k[l].reshape(DEC_BATCH * N_META, D_SWA_KV), cache_meta_v[l].reshape(DEC_BATCH * N_META, D_SWA_KV),
        cache_win_k[l].reshape(DEC_BATCH * WINDOW, D_SWA_KV), cache_win_v[l].reshape(DEC_BATCH * WINDOW, D_SWA_KV))

    zero_s = jnp.zeros((1, N_HEADS_GDN, DK_GDN, DV_GDN), F32)
    zero_buf = jnp.zeros((1, CARRY, D_CONV), F32)
    gdn = functools.partial(_gdn, conv_w=cw, alog_row=alog_row, dt_row=dt_row, nw_row=nw_row)
    _, s_meta = gdn(h_s, t_s, s0=zero_s, buf0=zero_buf, n_seq=1, n_chunks=1, C=N_META, shared_init=True,
                    first_chunk=META_ROW0 // N_META)
    buf_meta = h_s[None, META_ROW0 + N_META - CARRY:META_ROW0 + N_META, COL_CONV:COL_CONV + D_CONV]
    o_gdn_p, s_prompt = gdn(h_p, t_p, s0=s_meta, buf0=buf_meta, n_seq=BATCH, n_chunks=CHUNKS_PER_SEQ, C=CHUNK,
                            shared_init=True)
    buf_s = jnp.pad(state_conv[l].astype(F32), ((0, 0), (CARRY - (CONV_WIDTH - 1), 0), (0, 0)))
    o_gdn_s, s_sample = gdn(h_s, t_s, s0=state_gdn[l].astype(F32), buf0=buf_s, n_seq=DEC_BATCH, n_chunks=1,
                            C=CHUNK, shared_init=False)

    x2_p = _out_proj(o_swa_p, o_gdn_p, wo, x1_p, g2, b2, tm=256)
    x2_s = _out_proj(o_swa_s, o_gdn_s, wo, x1_s, g2, b2, tm=256)
    ffn2 = functools.partial(_ffn, wg=wg2, wu=wu2, wd=wd2, g=g3, b=b3, emit_bf16=False)
    (y_p,) = ffn2(x2_p, tm=512)
    (y_s,) = ffn2(x2_s, tm=512)

    y_prompt = y_p.reshape(BATCH, SEQ, D_MODEL)
    y_sample = y_s.reshape(DEC_BATCH, DEC_SEQ, D_MODEL)
    k_meta = h_s[meta_rows, COL_K:COL_K + D_SWA_KV].reshape(N_META, N_KV_SWA, HEAD_DIM)
    v_meta = h_s[meta_rows, COL_V:COL_V + D_SWA_KV].reshape(N_META, N_KV_SWA, HEAD_DIM)
    p_meta_k = jnp.broadcast_to(k_meta[None, None], (1, BATCH, N_META, N_KV_SWA, HEAD_DIM))
    p_meta_v = jnp.broadcast_to(v_meta[None, None], (1, BATCH, N_META, N_KV_SWA, HEAD_DIM))
    hp = h_p.reshape(BATCH, SEQ, D_MAIN)
    sample_cols = lambda c0, width: h_s[:N_SAMPLE_ROWS, c0:c0 + width].reshape(DEC_BATCH, DEC_SEQ, width)
    p_win_k = hp[:, SEQ - WINDOW:, COL_K:COL_K + D_SWA_KV].reshape(1, BATCH, WINDOW, N_KV_SWA, HEAD_DIM)
    p_win_v = hp[:, SEQ - WINDOW:, COL_V:COL_V + D_SWA_KV].reshape(1, BATCH, WINDOW, N_KV_SWA, HEAD_DIM)
    p_conv = hp[:, SEQ - (CONV_WIDTH - 1):, COL_CONV:COL_CONV + D_CONV][None]
    p_gdn = s_prompt[None]
    s_win_k = sample_cols(COL_K, D_SWA_KV).reshape(1, DEC_BATCH, DEC_SEQ, N_KV_SWA, HEAD_DIM)
    s_win_v = sample_cols(COL_V, D_SWA_KV).reshape(1, DEC_BATCH, DEC_SEQ, N_KV_SWA, HEAD_DIM)
    s_conv = sample_cols(COL_CONV, D_CONV)[:, DEC_SEQ - (CONV_WIDTH - 1):][None]
    s_gdn = s_sample[None]
    return (y_prompt, y_sample, p_meta_k, p_meta_v, p_win_k, p_win_v, p_conv, p_gdn, s_win_k, s_win_v, s_conv,
            s_gdn)
```

```python
import functools

import jax
import jax.numpy as jnp
from jax import lax
from jax.experimental import pallas as pl
from jax.experimental.pallas import tpu as pltpu

D_MODEL = 4096
BATCH = 4
SEQ = 2048
DEC_BATCH = 16
DEC_SEQ = 64
PAST_LEN = 2048
CHUNK = 64
N_META = 16
WINDOW = 128
HEAD_DIM = 128
N_HEADS_SWA = 16
N_KV_SWA = 4
GQA_GROUP = 4
N_HEADS_GDN = 16
DK_GDN = 128
DV_GDN = 128
CONV_WIDTH = 4
D_SWA_Q = N_HEADS_SWA * HEAD_DIM
D_SWA_KV = N_KV_SWA * HEAD_DIM
D_GDN = N_HEADS_GDN * DK_GDN
D_CONV = 3 * D_GDN
D_FF = 11008
ROPE_THETA = 10000.0
LN_EPS = 1e-5
RMS_EPS = 1e-6
ALPHA = 2.0 ** 0.25

N_PROMPT_ROWS = BATCH * SEQ
N_SAMPLE_ROWS = DEC_BATCH * DEC_SEQ
CHUNKS_PER_SEQ = SEQ // CHUNK
META_ROW0 = N_SAMPLE_ROWS
SM_TILE = 528
N_SM_ROWS = 2 * SM_TILE

COL_CONV = 0
COL_Z = D_CONV
COL_Q = COL_Z + D_GDN
COL_K = COL_Q + D_SWA_Q
COL_V = COL_K + D_SWA_KV
D_MAIN = COL_V + D_SWA_KV
TAIL_W = 128

FFN_TF = 256
LN_ROWS_CHOICES = (64, 48, 16, 8)
LN_UNROLL = 2
PROJ_TN_F32 = 512
PROJ_TN_BF16 = 1024
CAST_STRIP_ROWS = (16, 32, 64, 128, 256)
CARRY = 8

V7X_VMEM_BYTES = 64 * 1024 * 1024
VMEM_LIMIT = V7X_VMEM_BYTES - 4 * 1024 * 1024

BF16 = jnp.bfloat16
F32 = jnp.float32
HI = lax.Precision.HIGHEST


def _dot(a, b, precision=None):
    return jnp.dot(a, b, preferred_element_type=F32, precision=precision)


def _dot_nt(a, b):
    return lax.dot_general(a, b, (((1,), (1,)), ((), ())), preferred_element_type=F32)


def _head_dot(a, b, lhs_contract, rhs_contract):
    dims = (((lhs_contract,), (rhs_contract,)), ((0,), (0,)))
    return lax.dot_general(a.astype(BF16), b.astype(BF16), dims, preferred_element_type=F32)


_hdot = functools.partial(_head_dot, lhs_contract=2, rhs_contract=1)
_hdot_nt = functools.partial(_head_dot, lhs_contract=2, rhs_contract=2)
_hdot_tn = functools.partial(_head_dot, lhs_contract=1, rhs_contract=1)


def _layernorm_rows(y, g, b):
    mu = jnp.mean(y, axis=-1, keepdims=True)
    d = y - mu
    var = jnp.mean(d * d, axis=-1, keepdims=True)
    return d * lax.rsqrt(var + LN_EPS) * g + b


def _layernorm_into(o_ref, pre_norm_rows, g_ref, b_ref):
    n_rows = o_ref.shape[0]
    per_chunk = next(r for r in LN_ROWS_CHOICES if n_rows % r == 0)

    def chunk(c, carry):
        rows = pl.ds(pl.multiple_of(c * per_chunk, per_chunk), per_chunk)
        o_ref[rows, :] = _layernorm_rows(pre_norm_rows(rows), g_ref[...], b_ref[...])
        return carry

    lax.fori_loop(0, n_rows // per_chunk, chunk, 0, unroll=LN_UNROLL)


def _silu(x):
    return x * jax.nn.sigmoid(x)


def _ffn_body(x_ref, wg_ref, wu_ref, wd_ref, g_ref, b_ref, *rest, emit_bf16, n_cast):
    cast_in, rest = rest[:n_cast], rest[n_cast:]
    o_ref, rest = rest[0], rest[1:]
    if emit_bf16:
        ob_ref, rest = rest[0], rest[1:]
    cast_out, (xb_ref,) = rest[:n_cast], rest[n_cast:]
    f = pl.program_id(1)

    for src, dst in zip(cast_in, cast_out):
        dst[...] = src[...].astype(BF16)

    @pl.when(f == 0)
    def _():
        xb_ref[...] = x_ref[...].astype(BF16)
        o_ref[...] = jnp.zeros_like(o_ref)

    xb = xb_ref[...]
    hidden = _silu(_dot(xb, wg_ref[...])) * _dot(xb, wu_ref[...])
    o_ref[...] += _dot(hidden.astype(BF16), wd_ref[...])

    @pl.when(f == pl.num_programs(1) - 1)
    def _():
        _layernorm_into(o_ref, lambda rows: ALPHA * x_ref[rows, :] + 0.5 * o_ref[rows, :], g_ref, b_ref)
        if emit_bf16:
            ob_ref[...] = o_ref[...].astype(BF16)


def _ffn(x, wg, wu, wd, g, b, *, tm, emit_bf16, cast_next=None):
    rows = x.shape[0]
    grid = (rows // tm, D_FF // FFN_TF)
    row_spec = lambda: pl.BlockSpec((tm, D_MODEL), lambda i, f: (i, 0))
    in_specs = [
        pl.BlockSpec((tm, D_MODEL), lambda i, f: (i, 0)),
        pl.BlockSpec((D_MODEL, FFN_TF), lambda i, f: (0, f)),
        pl.BlockSpec((D_MODEL, FFN_TF), lambda i, f: (0, f)),
        pl.BlockSpec((FFN_TF, D_MODEL), lambda i, f: (f, 0)),
        pl.BlockSpec((1, D_MODEL), lambda i, f: (0, 0)),
        pl.BlockSpec((1, D_MODEL), lambda i, f: (0, 0)),
    ]
    args = [x, wg, wu, wd, g, b]
    out_shape = [jax.ShapeDtypeStruct((rows, D_MODEL), F32)]
    out_specs = [row_spec()]
    if emit_bf16:
        out_shape.append(jax.ShapeDtypeStruct((rows, D_MODEL), BF16))
        out_specs.append(row_spec())
    n_cast = 0
    if cast_next is not None:
        assert grid[0] * FFN_TF == D_MODEL and FFN_TF % grid[0] == 0
        n_cast = 3
        up_map = lambda i, f: (0, 1, i, f)
        sq = (None, None, FFN_TF, FFN_TF)
        strip = FFN_TF // grid[0]
        down_map = lambda i, f: (0, 1, f * grid[0] + i, 0)
        in_specs += [pl.BlockSpec(sq, up_map), pl.BlockSpec(sq, up_map),
                     pl.BlockSpec((None, None, strip, D_MODEL), down_map)]
        args += list(cast_next)
        out_shape += [jax.ShapeDtypeStruct((D_MODEL, D_FF), BF16), jax.ShapeDtypeStruct((D_MODEL, D_FF), BF16),
                      jax.ShapeDtypeStruct((D_FF, D_MODEL), BF16)]
        out_specs += [pl.BlockSpec((FFN_TF, FFN_TF), lambda i, f: (i, f)),
                      pl.BlockSpec((FFN_TF, FFN_TF), lambda i, f: (i, f)),
                      pl.BlockSpec((strip, D_MODEL), lambda i, f: (f * grid[0] + i, 0))]
    return pl.pallas_call(
        functools.partial(_ffn_body, emit_bf16=emit_bf16, n_cast=n_cast),
        grid=grid,
        in_specs=in_specs,
        out_specs=out_specs,
        out_shape=out_shape,
        scratch_shapes=[pltpu.VMEM((tm, D_MODEL), BF16)],
        compiler_params=pltpu.CompilerParams(
            dimension_semantics=("parallel", "arbitrary"), vmem_limit_bytes=VMEM_LIMIT),
        name="ffn",
    )(*args)


def _proj_body(x_ref, w_ref, wt_ref, cos_ref, sin_ref, *rest, tn, emit_weights, has_cast_job):
    rest = list(rest)
    cast_src = rest.pop(0) if has_cast_job else None
    o_ref, t_ref = rest[:2]
    j = pl.program_id(1)
    if has_cast_job:
        rest[-1][...] = cast_src[...].astype(BF16)
    wb = w_ref[...].astype(BF16)
    if emit_weights:
        rest[2][...] = wb
    acc = _dot_nt(x_ref[...], wb)
    n_full, n_part = divmod((D_SWA_Q + D_SWA_KV) // HEAD_DIM, tn // HEAD_DIM)

    def store(n_rope_heads):
        cos = cos_ref[...]
        sin = sin_ref[...]
        for s in range(tn // HEAD_DIM):
            cols = slice(s * HEAD_DIM, (s + 1) * HEAD_DIM)
            blk = acc[:, cols]
            o_ref[:, cols] = blk * cos + pltpu.roll(blk, HEAD_DIM // 2, 1) * sin if s < n_rope_heads else blk

    pl.when(j < n_full)(lambda: store(tn // HEAD_DIM))
    pl.when(j == n_full)(lambda: store(n_part))

    @pl.when(j > n_full)
    def _():
        o_ref[...] = acc

    @pl.when(j == 0)
    def _():
        t_ref[...] = _dot_nt(x_ref[...], wt_ref[...])


def _proj(xb, w_t, w_tail, cos, sin, *, tm, tn, emit_weights, cast_job=None):
    rows = xb.shape[0]
    grid = (rows // tm, D_MAIN // tn)
    n_swa = (D_SWA_Q + 2 * D_SWA_KV) // tn

    def dest(i, j):
        return i, jnp.where(j < n_swa, j + COL_Q // tn, j - n_swa)

    in_specs = [
        pl.BlockSpec((tm, D_MODEL), lambda i, j: (i, 0)),
        pl.BlockSpec((tn, D_MODEL), lambda i, j: (j, 0)),
        pl.BlockSpec((TAIL_W, D_MODEL), lambda i, j: (0, 0)),
        pl.BlockSpec((tm, HEAD_DIM), lambda i, j: (i, 0)),
        pl.BlockSpec((tm, HEAD_DIM), lambda i, j: (i, 0)),
    ]
    args = [xb, w_t, w_tail, cos, sin]
    out_specs = [pl.BlockSpec((tm, tn), dest), pl.BlockSpec((tm, TAIL_W), lambda i, j: (i, 0))]
    out_shape = [jax.ShapeDtypeStruct((rows, D_MAIN), F32), jax.ShapeDtypeStruct((rows, TAIL_W), F32)]
    if emit_weights:
        assert grid[0] == 1
        out_specs.append(pl.BlockSpec((tn, D_MODEL), lambda i, j: (j, 0)))
        out_shape.append(jax.ShapeDtypeStruct((D_MAIN, D_MODEL), BF16))
    if cast_job is not None:
        n_steps = grid[0] * grid[1]
        r = cast_job.shape[0]
        strip = next(s for s in CAST_STRIP_ROWS if r % s == 0 and r // s <= n_steps)
        strip_map = lambda i, j: (jnp.minimum(i * grid[1] + j, r // strip - 1), 0)
        in_specs.insert(5, pl.BlockSpec((strip, D_MODEL), strip_map))
        args.append(cast_job)
        out_specs.append(pl.BlockSpec((strip, D_MODEL), strip_map))
        out_shape.append(jax.ShapeDtypeStruct(cast_job.shape, BF16))
    return pl.pallas_call(
        functools.partial(_proj_body, tn=tn, emit_weights=emit_weights, has_cast_job=cast_job is not None),
        grid=grid,
        in_specs=in_specs,
        out_specs=out_specs,
        out_shape=out_shape,
        compiler_params=pltpu.CompilerParams(
            dimension_semantics=("parallel", "arbitrary"), vmem_limit_bytes=VMEM_LIMIT),
        name="proj_in",
    )(*args)


N_KEYS = N_META + 3 * CHUNK
SWA_RUN = 4


def _stack_heads(ref, rows, heads):
    return jnp.concatenate([ref[rows, h * HEAD_DIM:(h + 1) * HEAD_DIM] for h in heads], axis=0)


def _group_heads(g):
    return range(g * GQA_GROUP, (g + 1) * GQA_GROUP)


def _sink_rows(sink_ref, g):
    return jnp.concatenate([jnp.full((CHUNK, 1), sink_ref[h], F32) for h in _group_heads(g)], axis=0)


def _softmax_pv(q, k, v, sink, mask):
    s = _hdot_nt(q, k) * (HEAD_DIM ** -0.5)
    if mask is not None:
        s = jnp.where(mask, s, -jnp.inf)
    m = jnp.maximum(jnp.max(s, axis=-1, keepdims=True), sink)
    p = jnp.exp(s - m)
    den = jnp.sum(p, axis=-1, keepdims=True) + jnp.exp(sink - m)
    return _hdot(p, v) / den


def _attn_sample_body(sink_ref, q_ref, km_ref, vm_ref, k2_ref, v2_ref, k1_ref, v1_ref, k0_ref, v0_ref, o_ref):
    groups = range(N_KV_SWA)
    every = slice(None)
    k = jnp.stack([jnp.concatenate([_stack_heads(r, every, [g]) for r in (km_ref, k2_ref, k1_ref, k0_ref)], axis=0)
                   for g in groups])
    v = jnp.stack([jnp.concatenate([_stack_heads(r, every, [g]) for r in (vm_ref, v2_ref, v1_ref, v0_ref)], axis=0)
                   for g in groups])
    q = jnp.stack([_stack_heads(q_ref, every, _group_heads(g)) for g in groups])
    sink = jnp.stack([_sink_rows(sink_ref, g) for g in groups])
    o = _softmax_pv(q, k, v, sink, None)
    for g in groups:
        for j, h in enumerate(_group_heads(g)):
            o_ref[:, h * HEAD_DIM:(h + 1) * HEAD_DIM] = o[g, j * CHUNK:(j + 1) * CHUNK].astype(BF16)


def _attn_run_body(sink_ref, q_ref, km_ref, vm_ref, kp_ref, vp_ref, kc_ref, vc_ref, o_ref):
    has_prev = pl.program_id(0) % (CHUNKS_PER_SEQ // SWA_RUN) >= 1
    col = lax.broadcasted_iota(jnp.int32, (1, N_KEYS), 1)
    groups = range(N_KV_SWA)
    q, k, v, sink, mask = [], [], [], [], []
    for sub in range(SWA_RUN):
        n_prev = max(2 - sub, 0)
        prev_rows = slice((2 - n_prev) * CHUNK, 2 * CHUNK)
        cur_rows = slice(max(sub - 2, 0) * CHUNK, (sub + 1) * CHUNK)
        visible = jnp.logical_or(jnp.logical_or(col < N_META, col >= N_META + n_prev * CHUNK), has_prev)
        for g in groups:
            k_parts = [_stack_heads(km_ref, slice(None), [g]), _stack_heads(kc_ref, cur_rows, [g])]
            v_parts = [_stack_heads(vm_ref, slice(None), [g]), _stack_heads(vc_ref, cur_rows, [g])]
            if n_prev:
                k_parts.insert(1, _stack_heads(kp_ref, prev_rows, [g]))
                v_parts.insert(1, _stack_heads(vp_ref, prev_rows, [g]))
            k.append(jnp.concatenate(k_parts, axis=0))
            v.append(jnp.concatenate(v_parts, axis=0))
            q.append(_stack_heads(q_ref, slice(sub * CHUNK, (sub + 1) * CHUNK), _group_heads(g)))
            sink.append(_sink_rows(sink_ref, g))
            mask.append(visible)
    o = _softmax_pv(jnp.stack(q), jnp.stack(k), jnp.stack(v), jnp.stack(sink), jnp.stack(mask))
    for sub in range(SWA_RUN):
        for g in groups:
            for j, h in enumerate(_group_heads(g)):
                o_ref[sub * CHUNK:(sub + 1) * CHUNK, h * HEAD_DIM:(h + 1) * HEAD_DIM] = (
                    o[sub * N_KV_SWA + g, j * CHUNK:(j + 1) * CHUNK].astype(BF16))


def _attn_prompt(sinks, h_p, h_m):
    kcol = COL_K // D_SWA_KV
    vcol = COL_V // D_SWA_KV
    qcol = COL_Q // D_SWA_Q
    runs_per_seq = CHUNKS_PER_SEQ // SWA_RUN
    pairs_per_run = SWA_RUN // 2
    prev = lambda colblk: (lambda p: (p * pairs_per_run - jnp.where(p % runs_per_seq >= 1, 1, 0), colblk))
    kv_prev = lambda colblk: pl.BlockSpec((2 * CHUNK, D_SWA_KV), prev(colblk))
    kv_run = lambda colblk: pl.BlockSpec((SWA_RUN * CHUNK, D_SWA_KV), lambda p: (p, colblk))
    meta = lambda colblk: pl.BlockSpec((N_META, D_SWA_KV), lambda p: (META_ROW0 // N_META, colblk))
    return pl.pallas_call(
        _attn_run_body,
        grid=(N_PROMPT_ROWS // (SWA_RUN * CHUNK),),
        in_specs=[
            pl.BlockSpec(memory_space=pltpu.SMEM),
            pl.BlockSpec((SWA_RUN * CHUNK, D_SWA_Q), lambda p: (p, qcol)),
            meta(kcol), meta(vcol),
            kv_prev(kcol), kv_prev(vcol),
            kv_run(kcol), kv_run(vcol),
        ],
        out_specs=pl.BlockSpec((SWA_RUN * CHUNK, D_SWA_Q), lambda p: (p, 0)),
        out_shape=jax.ShapeDtypeStruct((N_PROMPT_ROWS, D_SWA_Q), BF16),
        compiler_params=pltpu.CompilerParams(dimension_semantics=("parallel",)),
        name="swa_prompt",
    )(sinks, h_p, h_m, h_m, h_p, h_p, h_p, h_p)


def _attn_sample(sinks, h_s, meta_k, meta_v, win_k, win_v):
    kcol = COL_K // D_SWA_KV
    vcol = COL_V // D_SWA_KV
    qcol = COL_Q // D_SWA_Q
    kv = lambda imap: pl.BlockSpec((CHUNK, D_SWA_KV), imap)
    return pl.pallas_call(
        _attn_sample_body,
        grid=(DEC_BATCH,),
        in_specs=[
            pl.BlockSpec(memory_space=pltpu.SMEM),
            pl.BlockSpec((CHUNK, D_SWA_Q), lambda s: (s, qcol)),
            pl.BlockSpec((N_META, D_SWA_KV), lambda s: (s, 0)),
            pl.BlockSpec((N_META, D_SWA_KV), lambda s: (s, 0)),
            kv(lambda s: (2 * s, 0)), kv(lambda s: (2 * s, 0)),
            kv(lambda s: (2 * s + 1, 0)), kv(lambda s: (2 * s + 1, 0)),
            kv(lambda s: (s, kcol)), kv(lambda s: (s, vcol)),
        ],
        out_specs=pl.BlockSpec((CHUNK, D_SWA_Q), lambda s: (s, 0)),
        out_shape=jax.ShapeDtypeStruct((N_SAMPLE_ROWS, D_SWA_Q), BF16),
        compiler_params=pltpu.CompilerParams(dimension_semantics=("parallel",)),
        name="swa_sample",
    )(sinks, h_s, meta_k, meta_v, win_k, win_v, win_k, win_v, h_s, h_s)


def _unit_lower_inverse(a, n):
    r = lax.broadcasted_iota(jnp.int32, (n, n), 0)
    c = lax.broadcasted_iota(jnp.int32, (n, n), 1)
    eye = (r == c).astype(F32)
    base = 3
    a0 = jnp.where((r >> base) == (c >> base), a, 0.0)
    a2 = _hdot(a0, a0)
    a4 = _hdot(a2, a2)
    x = _hdot(_hdot(eye - a0, eye + a2), eye + a4)
    shift = base
    while (1 << shift) < n:
        pair = jnp.logical_and((r >> (shift + 1)) == (c >> (shift + 1)), (r >> shift) != (c >> shift))
        ak = jnp.where(pair, a, 0.0)
        x = x - _hdot(x, _hdot(ak, x))
        shift += 1
    return x


def _gdn_body(x_ref, z_ref, t_ref, w_ref, alog_ref, dt_ref, nw_ref, s0_ref, b0_ref, o_ref, sout_ref, s_scr,
              xe_scr, *, C, n_chunks):
    c = pl.program_id(1)

    @pl.when(c == 0)
    def _():
        s_scr[...] = s0_ref[0]
        xe_scr[0:CARRY, :] = b0_ref[0]

    @pl.when(c > 0)
    def _():
        xe_scr[0:CARRY, :] = xe_scr[C:C + CARRY, :]

    xe_scr[CARRY:CARRY + C, :] = x_ref[...]

    t = t_ref[...]
    beta = jax.nn.sigmoid(t)
    ta = t + dt_ref[...]
    softplus = jnp.maximum(ta, 0.0) + jnp.log(1.0 + jnp.exp(-jnp.abs(ta)))
    g = -jnp.exp(alog_ref[...]) * softplus
    r = lax.broadcasted_iota(jnp.int32, (C, C), 0)
    cc = lax.broadcasted_iota(jnp.int32, (C, C), 1)
    incl = r >= cc
    strict = r > cc
    gc = _dot(incl.astype(F32), g, HI)
    gc_t = gc.T
    heads = range(N_HEADS_GDN)
    bh = jnp.stack([beta[:, h:h + 1] for h in heads])
    gcol = jnp.stack([gc[:, N_HEADS_GDN + h:N_HEADS_GDN + h + 1] for h in heads])
    grow = jnp.stack([gc_t[N_HEADS_GDN + h:N_HEADS_GDN + h + 1, :] for h in heads])
    glast = gcol[:, C - 1:C, :]
    decay = jnp.where(incl, jnp.exp(jnp.where(incl, gcol - grow, 0.0)), 0.0)
    e_g = jnp.exp(gcol)

    xe = xe_scr[...]
    acc = w_ref[CONV_WIDTH - 1:CONV_WIDTH, :] * xe[CARRY:, :]
    for lag in range(1, CONV_WIDTH):
        tap = CONV_WIDTH - 1 - lag
        acc = acc + w_ref[tap:tap + 1, :] * pltpu.roll(xe, lag, 0)[CARRY:, :]
    y = _silu(acc)
    split = lambda base: jnp.stack([y[:, base + h * DK_GDN:base + (h + 1) * DK_GDN] for h in heads])
    q, k, v = split(0), split(D_GDN), split(2 * D_GDN)
    q = q * lax.rsqrt(jnp.sum(q * q, axis=-1, keepdims=True) + RMS_EPS) * (DK_GDN ** -0.5)
    k = k * lax.rsqrt(jnp.sum(k * k, axis=-1, keepdims=True) + RMS_EPS)
    kb = k * bh

    kq = _hdot_nt(jnp.concatenate([kb, q], axis=1), k)
    a = jnp.where(strict, kq[:, :C] * decay, 0.0)
    qk = kq[:, C:] * decay
    t_inv = _unit_lower_inverse(a, C)

    s_prev = s_scr[...]
    ws = _hdot(jnp.concatenate([kb * e_g, q * e_g], axis=1), s_prev)
    v_new = _hdot(t_inv, v * bh - ws[:, :C])
    o = ws[:, C:] + _hdot(qk, v_new)
    s_scr[...] = s_prev * jnp.exp(glast) + _hdot_tn(k * jnp.exp(glast - gcol), v_new)

    o = o * lax.rsqrt(jnp.mean(o * o, axis=-1, keepdims=True) + RMS_EPS) * nw_ref[...]
    for h in heads:
        hs = slice(h * DV_GDN, (h + 1) * DV_GDN)
        o_ref[:, hs] = (o[h] * _silu(z_ref[:, hs])).astype(BF16)

    @pl.when(c == n_chunks - 1)
    def _():
        sout_ref[0] = s_scr[...]


def _gdn(h_src, tail_src, conv_w, alog_row, dt_row, nw_row, s0, buf0, *, n_seq, n_chunks, C, shared_init,
         first_chunk=0):
    init = (lambda s: 0) if shared_init else (lambda s: s)
    vec = pl.BlockSpec((1, TAIL_W), lambda s, c: (0, 0))
    src = lambda s, c: first_chunk + s * n_chunks + c
    return pl.pallas_call(
        functools.partial(_gdn_body, C=C, n_chunks=n_chunks),
        grid=(n_seq, n_chunks),
        in_specs=[
            pl.BlockSpec((C, D_CONV), lambda s, c: (src(s, c), COL_CONV // D_CONV)),
            pl.BlockSpec((C, D_GDN), lambda s, c: (src(s, c), COL_Z // D_GDN)),
            pl.BlockSpec((C, TAIL_W), lambda s, c: (src(s, c), 0)),
            pl.BlockSpec((CONV_WIDTH, D_CONV), lambda s, c: (0, 0)),
            vec, vec, vec,
            pl.BlockSpec((1, N_HEADS_GDN, DK_GDN, DV_GDN), lambda s, c: (init(s), 0, 0, 0)),
            pl.BlockSpec((1, CARRY, D_CONV), lambda s, c: (init(s), 0, 0)),
        ],
        out_specs=[
            pl.BlockSpec((C, D_GDN), lambda s, c: (s * n_chunks + c, 0)),
            pl.BlockSpec((1, N_HEADS_GDN, DK_GDN, DV_GDN), lambda s, c: (s, 0, 0, 0)),
        ],
        out_shape=[
            jax.ShapeDtypeStruct((n_seq * n_chunks * C, D_GDN), BF16),
            jax.ShapeDtypeStruct((n_seq, N_HEADS_GDN, DK_GDN, DV_GDN), F32),
        ],
        scratch_shapes=[
            pltpu.VMEM((N_HEADS_GDN, DK_GDN, DV_GDN), F32),
            pltpu.VMEM((CARRY + C, D_CONV), F32),
        ],
        compiler_params=pltpu.CompilerParams(dimension_semantics=("parallel", "arbitrary")),
        name="gdn",
    )(h_src, h_src, tail_src, conv_w, alog_row, dt_row, nw_row, s0, buf0)


def _out_body(a_ref, b_ref, w_ref, x_ref, g_ref, beta_ref, o_ref):
    o_ref[...] = ALPHA * x_ref[...] + _dot(a_ref[...], w_ref[0:D_SWA_Q, :]) + _dot(b_ref[...], w_ref[D_SWA_Q:, :])
    _layernorm_into(o_ref, lambda rows: o_ref[rows, :], g_ref, beta_ref)


def _out_proj(o_swa, o_gdn, w_out, x1, g, b, *, tm):
    rows = o_swa.shape[0]
    return pl.pallas_call(
        _out_body,
        grid=(rows // tm,),
        in_specs=[
            pl.BlockSpec((tm, D_SWA_Q), lambda i: (i, 0)),
            pl.BlockSpec((tm, D_GDN), lambda i: (i, 0)),
            pl.BlockSpec((D_MODEL, D_MODEL), lambda i: (0, 0), pipeline_mode=pl.Buffered(1)),
            pl.BlockSpec((tm, D_MODEL), lambda i: (i, 0)),
            pl.BlockSpec((1, D_MODEL), lambda i: (0, 0)),
            pl.BlockSpec((1, D_MODEL), lambda i: (0, 0)),
        ],
        out_specs=pl.BlockSpec((tm, D_MODEL), lambda i: (i, 0)),
        out_shape=jax.ShapeDtypeStruct((rows, D_MODEL), F32),
        compiler_params=pltpu.CompilerParams(
            dimension_semantics=("parallel",), vmem_limit_bytes=VMEM_LIMIT),
        name="proj_out",
    )(o_swa, o_gdn, w_out, x1, g, b)


def _rope_tables(pos):
    half = HEAD_DIM // 2
    inv = ROPE_THETA ** (-jnp.arange(half, dtype=F32) / half)
    ang = pos.astype(F32)[:, None] * inv[None, :]
    cos = jnp.cos(ang)
    sin = jnp.sin(ang)
    return jnp.concatenate([cos, cos], axis=1), jnp.concatenate([-sin, sin], axis=1)


def kernel(x_prompt, x_sample, cache_meta_k, cache_meta_v, cache_win_k, cache_win_v, state_conv, state_gdn,
           meta_tokens, ln_g, ln_b, ffn_w_gate, ffn_w_up, ffn_w_down, w_in, w_out, attn_sinks, conv_w,
           gdn_a_log, gdn_dt_bias, gdn_norm_w):
    l = 0
    wg1, wu1, wd1 = (w[l, 0].astype(BF16) for w in (ffn_w_gate, ffn_w_up, ffn_w_down))
    w_main = jnp.swapaxes(w_in[l], 0, 1)
    w_tail = jnp.pad(w_main[D_MAIN:], ((0, TAIL_W - 2 * N_HEADS_GDN), (0, 0))).astype(BF16)
    g1, g2, g3 = (ln_g[l, i][None, :] for i in range(3))
    b1, b2, b3 = (ln_b[l, i][None, :] for i in range(3))
    pad_tail = lambda v, off: jnp.pad(v.astype(F32), (off, TAIL_W - off - N_HEADS_GDN))[None, :]
    alog_row = pad_tail(gdn_a_log[l], N_HEADS_GDN)
    dt_row = pad_tail(gdn_dt_bias[l], N_HEADS_GDN)
    nw_row = gdn_norm_w[l].astype(F32)[None, :]
    cw = conv_w[l]
    sinks = attn_sinks[l].astype(F32)

    n_pad = N_SM_ROWS - N_SAMPLE_ROWS - N_META
    xp = x_prompt.reshape(N_PROMPT_ROWS, D_MODEL)
    xs = jnp.concatenate([x_sample.reshape(N_SAMPLE_ROWS, D_MODEL), meta_tokens.astype(F32),
                          jnp.zeros((n_pad, D_MODEL), F32)], axis=0)
    cos_p, sin_p = _rope_tables(jnp.tile(N_META + jnp.arange(SEQ, dtype=jnp.int32), BATCH))
    cos_s, sin_s = _rope_tables(jnp.concatenate([
        jnp.tile(N_META + PAST_LEN + jnp.arange(DEC_SEQ, dtype=jnp.int32), DEC_BATCH),
        jnp.arange(N_META, dtype=jnp.int32), jnp.zeros((n_pad,), jnp.int32)]))

    ffn1 = functools.partial(_ffn, wg=wg1, wu=wu1, wd=wd1, g=g1, b=b1, emit_bf16=True)
    x1_s, x1b_s = ffn1(xs, tm=SM_TILE)
    x1_p, x1b_p, wg2, wu2, wd2 = ffn1(xp, tm=512, cast_next=(ffn_w_gate, ffn_w_up, ffn_w_down))
    h_s, t_s, w_main_b = _proj(x1b_s, w_main, w_tail, cos_s, sin_s, tm=N_SM_ROWS, tn=PROJ_TN_F32,
                               emit_weights=True)
    h_p, t_p, wo = _proj(x1b_p, w_main_b, w_tail, cos_p, sin_p, tm=1024, tn=PROJ_TN_BF16, emit_weights=False,
                         cast_job=w_out[l])
    meta_rows = slice(META_ROW0, META_ROW0 + N_META)

    o_swa_p = _attn_prompt(sinks, h_p, h_s)
    o_swa_s = _attn_sample(
        sinks, h_s,
        cache_meta_k[l].reshape(DEC_BATCH * N_META, D_SWA_KV), cache_meta_v[l].reshape(DEC_BATCH * N_META, D_SWA_KV),
        cache_win_k[l].reshape(DEC_BATCH * WINDOW, D_SWA_KV), cache_win_v[l].reshape(DEC_BATCH * WINDOW, D_SWA_KV))

    zero_s = jnp.zeros((1, N_HEADS_GDN, DK_GDN, DV_GDN), F32)
    zero_buf = jnp.zeros((1, CARRY, D_CONV), F32)
    gdn = functools.partial(_gdn, conv_w=cw, alog_row=alog_row, dt_row=dt_row, nw_row=nw_row)
    _, s_meta = gdn(h_s, t_s, s0=zero_s, buf0=zero_buf, n_seq=1, n_chunks=1, C=N_META, shared_init=True,
                    first_chunk=META_ROW0 // N_META)
    buf_meta = h_s[None, META_ROW0 + N_META - CARRY:META_ROW0 + N_META, COL_CONV:COL_CONV + D_CONV]
    o_gdn_p, s_prompt = gdn(h_p, t_p, s0=s_meta, buf0=buf_meta, n_seq=BATCH, n_chunks=CHUNKS_PER_SEQ, C=CHUNK,
                            shared_init=True)
    buf_s = jnp.pad(state_conv[l].astype(F32), ((0, 0), (CARRY - (CONV_WIDTH - 1), 0), (0, 0)))
    o_gdn_s, s_sample = gdn(h_s, t_s, s0=state_gdn[l].astype(F32), buf0=buf_s, n_seq=DEC_BATCH, n_chunks=1,
                            C=CHUNK, shared_init=False)

    x2_p = _out_proj(o_swa_p, o_gdn_p, wo, x1_p, g2, b2, tm=256)
    x2_s = _out_proj(o_swa_s, o_gdn_s, wo, x1_s, g2, b2, tm=256)
    ffn2 = functools.partial(_ffn, wg=wg2, wu=wu2, wd=wd2, g=g3, b=b3, emit_bf16=False)
    (y_p,) = ffn2(x2_p, tm=512)
    (y_s,) = ffn2(x2_s, tm=512)

    y_prompt = y_p.reshape(BATCH, SEQ, D_MODEL)
    y_sample = y_s.reshape(DEC_BATCH, DEC_SEQ, D_MODEL)
    k_meta = h_s[meta_rows, COL_K:COL_K + D_SWA_KV].reshape(N_META, N_KV_SWA, HEAD_DIM)
    v_meta = h_s[meta_rows, COL_V:COL_V + D_SWA_KV].reshape(N_META, N_KV_SWA, HEAD_DIM)
    p_meta_k = jnp.broadcast_to(k_meta[None, None], (1, BATCH, N_META, N_KV_SWA, HEAD_DIM))
    p_meta_v = jnp.broadcast_to(v_meta[None, None], (1, BATCH, N_META, N_KV_SWA, HEAD_DIM))
    hp = h_p.reshape(BATCH, SEQ, D_MAIN)
    sample_cols = lambda c0, width: h_s[:N_SAMPLE_ROWS, c0:c0 + width].reshape(DEC_BATCH, DEC_SEQ, width)
    p_win_k = hp[:, SEQ - WINDOW:, COL_K:COL_K + D_SWA_KV].reshape(1, BATCH, WINDOW, N_KV_SWA, HEAD_DIM)
    p_win_v = hp[:, SEQ - WINDOW:, COL_V:COL_V + D_SWA_KV].reshape(1, BATCH, WINDOW, N_KV_SWA, HEAD_DIM)
    p_conv = hp[:, SEQ - (CONV_WIDTH - 1):, COL_CONV:COL_CONV + D_CONV][None]
    p_gdn = s_prompt[None]
    s_win_k = sample_cols(COL_K, D_SWA_KV).reshape(1, DEC_BATCH, DEC_SEQ, N_KV_SWA, HEAD_DIM)
    s_win_v = sample_cols(COL_V, D_SWA_KV).reshape(1, DEC_BATCH, DEC_SEQ, N_KV_SWA, HEAD_DIM)
    s_conv = sample_cols(COL_CONV, D_CONV)[:, DEC_SEQ - (CONV_WIDTH - 1):][None]
    s_gdn = s_sample[None]
    return (y_prompt, y_sample, p_meta_k, p_meta_v, p_win_k, p_win_v, p_conv, p_gdn, s_win_k, s_win_v, s_conv,
            s_gdn)
```

```python
import functools

import jax
import jax.numpy as jnp
from jax import lax
from jax.experimental import pallas as pl
from jax.experimental.pallas import tpu as pltpu

D_MODEL = 4096
BATCH = 4
SEQ = 2048
DEC_BATCH = 16
DEC_SEQ = 64
PAST_LEN = 2048
CHUNK = 64
N_META = 16
WINDOW = 128
HEAD_DIM = 128
N_HEADS_SWA = 16
N_KV_SWA = 4
GQA_GROUP = 4
N_HEADS_GDN = 16
DK_GDN = 128
DV_GDN = 128
CONV_WIDTH = 4
D_SWA_Q = N_HEADS_SWA * HEAD_DIM
D_SWA_KV = N_KV_SWA * HEAD_DIM
D_GDN = N_HEADS_GDN * DK_GDN
D_CONV = 3 * D_GDN
D_FF = 11008
ROPE_THETA = 10000.0
LN_EPS = 1e-5
RMS_EPS = 1e-6
ALPHA = 2.0 ** 0.25

N_PROMPT_ROWS = BATCH * SEQ
N_SAMPLE_ROWS = DEC_BATCH * DEC_SEQ
CHUNKS_PER_SEQ = SEQ // CHUNK
META_ROW0 = N_SAMPLE_ROWS
SM_TILE = 528
N_SM_ROWS = 2 * SM_TILE

COL_CONV = 0
COL_Z = D_CONV
COL_Q = COL_Z + D_GDN
COL_K = COL_Q + D_SWA_Q
COL_V = COL_K + D_SWA_KV
D_MAIN = COL_V + D_SWA_KV
TAIL_W = 128

FFN_TF = 256
LN_ROWS_CHOICES = (64, 48, 16, 8)
LN_UNROLL = 2
PROJ_TN_F32 = 512
PROJ_TN_BF16 = 1024
CAST_STRIP_ROWS = (16, 32, 64, 128, 256)
CARRY = 8

V7X_VMEM_BYTES = 64 * 1024 * 1024
VMEM_LIMIT = V7X_VMEM_BYTES - 4 * 1024 * 1024

BF16 = jnp.bfloat16
F32 = jnp.float32
HI = lax.Precision.HIGHEST


def _dot(a, b, precision=None):
    return jnp.dot(a, b, preferred_element_type=F32, precision=precision)


def _dot_nt(a, b):
    return lax.dot_general(a, b, (((1,), (1,)), ((), ())), preferred_element_type=F32)


def _head_dot(a, b, lhs_contract, rhs_contract):
    dims = (((lhs_contract,), (rhs_contract,)), ((0,), (0,)))
    return lax.dot_general(a.astype(BF16), b.astype(BF16), dims, preferred_element_type=F32)


_hdot = functools.partial(_head_dot, lhs_contract=2, rhs_contract=1)
_hdot_nt = functools.partial(_head_dot, lhs_contract=2, rhs_contract=2)
_hdot_tn = functools.partial(_head_dot, lhs_contract=1, rhs_contract=1)


def _layernorm_rows(y, g, b):
    mu = jnp.mean(y, axis=-1, keepdims=True)
    d = y - mu
    var = jnp.mean(d * d, axis=-1, keepdims=True)
    return d * lax.rsqrt(var + LN_EPS) * g + b


def _layernorm_into(o_ref, pre_norm_rows, g_ref, b_ref):
    n_rows = o_ref.shape[0]
    per_chunk = next(r for r in LN_ROWS_CHOICES if n_rows % r == 0)

    def chunk(c, carry):
        rows = pl.ds(pl.multiple_of(c * per_chunk, per_chunk), per_chunk)
        o_ref[rows, :] = _layernorm_rows(pre_norm_rows(rows), g_ref[...], b_ref[...])
        return carry

    lax.fori_loop(0, n_rows // per_chunk, chunk, 0, unroll=LN_UNROLL)


def _silu(x):
    return x * jax.nn.sigmoid(x)


def _cast_job(w, prefix, n_steps, step):
    r, c = w.shape[len(prefix):]
    strip = next(s for s in CAST_STRIP_ROWS if r % s == 0 and r // s <= n_steps)
    last = r // strip - 1
    strip_of = lambda *ids: jnp.minimum(step(*ids), last)
    in_spec = pl.BlockSpec((None,) * len(prefix) + (strip, c), lambda *ids: prefix + (strip_of(*ids), 0))
    out_spec = pl.BlockSpec((strip, c), lambda *ids: (strip_of(*ids), 0))
    return in_spec, out_spec, jax.ShapeDtypeStruct((r, c), BF16)


def _ffn_body(x_ref, wg_ref, wu_ref, wd_ref, g_ref, b_ref, o_ref, *rest, emit_bf16):
    if emit_bf16:
        ob_ref, xb_ref = rest
    else:
        (xb_ref,) = rest
    f = pl.program_id(1)

    @pl.when(f == 0)
    def _():
        xb_ref[...] = x_ref[...].astype(BF16)
        o_ref[...] = jnp.zeros_like(o_ref)

    xb = xb_ref[...]
    hidden = _silu(_dot(xb, wg_ref[...])) * _dot(xb, wu_ref[...])
    o_ref[...] += _dot(hidden.astype(BF16), wd_ref[...])

    @pl.when(f == pl.num_programs(1) - 1)
    def _():
        _layernorm_into(o_ref, lambda rows: ALPHA * x_ref[rows, :] + 0.5 * o_ref[rows, :], g_ref, b_ref)
        if emit_bf16:
            ob_ref[...] = o_ref[...].astype(BF16)


def _ffn(x, wg, wu, wd, g, b, *, tm, emit_bf16):
    rows = x.shape[0]
    row_spec = lambda: pl.BlockSpec((tm, D_MODEL), lambda i, f: (i, 0))
    out_shape = [jax.ShapeDtypeStruct((rows, D_MODEL), F32)]
    out_specs = [row_spec()]
    if emit_bf16:
        out_shape.append(jax.ShapeDtypeStruct((rows, D_MODEL), BF16))
        out_specs.append(row_spec())
    return pl.pallas_call(
        functools.partial(_ffn_body, emit_bf16=emit_bf16),
        grid=(rows // tm, D_FF // FFN_TF),
        in_specs=[
            row_spec(),
            pl.BlockSpec((D_MODEL, FFN_TF), lambda i, f: (0, f)),
            pl.BlockSpec((D_MODEL, FFN_TF), lambda i, f: (0, f)),
            pl.BlockSpec((FFN_TF, D_MODEL), lambda i, f: (f, 0)),
            pl.BlockSpec((1, D_MODEL), lambda i, f: (0, 0)),
            pl.BlockSpec((1, D_MODEL), lambda i, f: (0, 0)),
        ],
        out_specs=out_specs,
        out_shape=out_shape,
        scratch_shapes=[pltpu.VMEM((tm, D_MODEL), BF16)],
        compiler_params=pltpu.CompilerParams(
            dimension_semantics=("parallel", "arbitrary"), vmem_limit_bytes=VMEM_LIMIT),
        name="ffn",
    )(x, wg, wu, wd, g, b)


def _proj_body(x_ref, w_ref, wt_ref, cos_ref, sin_ref, *rest, tn, emit_weights, has_cast_job):
    rest = list(rest)
    cast_src = rest.pop(0) if has_cast_job else None
    o_ref, t_ref = rest[:2]
    j = pl.program_id(1)
    if has_cast_job:
        rest[-1][...] = cast_src[...].astype(BF16)
    wb = w_ref[...].astype(BF16)
    if emit_weights:
        rest[2][...] = wb
    acc = _dot_nt(x_ref[...], wb)
    n_full, n_part = divmod((D_SWA_Q + D_SWA_KV) // HEAD_DIM, tn // HEAD_DIM)

    def store(n_rope_heads):
        cos = cos_ref[...]
        sin = sin_ref[...]
        for s in range(tn // HEAD_DIM):
            cols = slice(s * HEAD_DIM, (s + 1) * HEAD_DIM)
            blk = acc[:, cols]
            o_ref[:, cols] = blk * cos + pltpu.roll(blk, HEAD_DIM // 2, 1) * sin if s < n_rope_heads else blk

    pl.when(j < n_full)(lambda: store(tn // HEAD_DIM))
    pl.when(j == n_full)(lambda: store(n_part))

    @pl.when(j > n_full)
    def _():
        o_ref[...] = acc

    @pl.when(j == 0)
    def _():
        t_ref[...] = _dot_nt(x_ref[...], wt_ref[...])


def _proj(xb, w_t, w_tail, cos, sin, *, tm, tn, emit_weights, cast_job=None):
    rows = xb.shape[0]
    grid = (rows // tm, D_MAIN // tn)
    n_swa = (D_SWA_Q + 2 * D_SWA_KV) // tn

    def dest(i, j):
        return i, jnp.where(j < n_swa, j + COL_Q // tn, j - n_swa)

    in_specs = [
        pl.BlockSpec((tm, D_MODEL), lambda i, j: (i, 0)),
        pl.BlockSpec((tn, D_MODEL), lambda i, j: (j, 0)),
        pl.BlockSpec((TAIL_W, D_MODEL), lambda i, j: (0, 0)),
        pl.BlockSpec((tm, HEAD_DIM), lambda i, j: (i, 0)),
        pl.BlockSpec((tm, HEAD_DIM), lambda i, j: (i, 0)),
    ]
    args = [xb, w_t, w_tail, cos, sin]
    out_specs = [pl.BlockSpec((tm, tn), dest), pl.BlockSpec((tm, TAIL_W), lambda i, j: (i, 0))]
    out_shape = [jax.ShapeDtypeStruct((rows, D_MAIN), F32), jax.ShapeDtypeStruct((rows, TAIL_W), F32)]
    if emit_weights:
        assert grid[0] == 1
        out_specs.append(pl.BlockSpec((tn, D_MODEL), lambda i, j: (j, 0)))
        out_shape.append(jax.ShapeDtypeStruct((D_MAIN, D_MODEL), BF16))
    if cast_job is not None:
        job_in, job_out, job_shape = _cast_job(cast_job, (), grid[0] * grid[1], lambda i, j: i * grid[1] + j)
        in_specs.append(job_in)
        args.append(cast_job)
        out_specs.append(job_out)
        out_shape.append(job_shape)
    return pl.pallas_call(
        functools.partial(_proj_body, tn=tn, emit_weights=emit_weights, has_cast_job=cast_job is not None),
        grid=grid,
        in_specs=in_specs,
        out_specs=out_specs,
        out_shape=out_shape,
        compiler_params=pltpu.CompilerParams(
            dimension_semantics=("parallel", "arbitrary"), vmem_limit_bytes=VMEM_LIMIT),
        name="proj_in",
    )(*args)


N_KEYS = N_META + 3 * CHUNK
SWA_RUN = 4


def _stack_heads(ref, rows, heads):
    return jnp.concatenate([ref[rows, h * HEAD_DIM:(h + 1) * HEAD_DIM] for h in heads], axis=0)


def _group_heads(g):
    return range(g * GQA_GROUP, (g + 1) * GQA_GROUP)


def _sink_rows(sink_ref, g):
    return jnp.concatenate([jnp.full((CHUNK, 1), sink_ref[h], F32) for h in _group_heads(g)], axis=0)


def _softmax_pv(q, k, v, sink, mask):
    s = _hdot_nt(q, k) * (HEAD_DIM ** -0.5)
    if mask is not None:
        s = jnp.where(mask, s, -jnp.inf)
    m = jnp.maximum(jnp.max(s, axis=-1, keepdims=True), sink)
    p = jnp.exp(s - m)
    den = jnp.sum(p, axis=-1, keepdims=True) + jnp.exp(sink - m)
    return _hdot(p, v) / den


def _attn_sample_body(sink_ref, q_ref, km_ref, vm_ref, k2_ref, v2_ref, k1_ref, v1_ref, k0_ref, v0_ref, o_ref):
    groups = range(N_KV_SWA)
    every = slice(None)
    k = jnp.stack([jnp.concatenate([_stack_heads(r, every, [g]) for r in (km_ref, k2_ref, k1_ref, k0_ref)], axis=0)
                   for g in groups])
    v = jnp.stack([jnp.concatenate([_stack_heads(r, every, [g]) for r in (vm_ref, v2_ref, v1_ref, v0_ref)], axis=0)
                   for g in groups])
    q = jnp.stack([_stack_heads(q_ref, every, _group_heads(g)) for g in groups])
    sink = jnp.stack([_sink_rows(sink_ref, g) for g in groups])
    o = _softmax_pv(q, k, v, sink, None)
    for g in groups:
        for j, h in enumerate(_group_heads(g)):
            o_ref[:, h * HEAD_DIM:(h + 1) * HEAD_DIM] = o[g, j * CHUNK:(j + 1) * CHUNK].astype(BF16)


def _attn_run_body(sink_ref, q_ref, km_ref, vm_ref, kp_ref, vp_ref, kc_ref, vc_ref, o_ref):
    has_prev = pl.program_id(0) % (CHUNKS_PER_SEQ // SWA_RUN) >= 1
    col = lax.broadcasted_iota(jnp.int32, (1, N_KEYS), 1)
    groups = range(N_KV_SWA)
    q, k, v, sink, mask = [], [], [], [], []
    for sub in range(SWA_RUN):
        n_prev = max(2 - sub, 0)
        prev_rows = slice((2 - n_prev) * CHUNK, 2 * CHUNK)
        cur_rows = slice(max(sub - 2, 0) * CHUNK, (sub + 1) * CHUNK)
        visible = jnp.logical_or(jnp.logical_or(col < N_META, col >= N_META + n_prev * CHUNK), has_prev)
        for g in groups:
            k_parts = [_stack_heads(km_ref, slice(None), [g]), _stack_heads(kc_ref, cur_rows, [g])]
            v_parts = [_stack_heads(vm_ref, slice(None), [g]), _stack_heads(vc_ref, cur_rows, [g])]
            if n_prev:
                k_parts.insert(1, _stack_heads(kp_ref, prev_rows, [g]))
                v_parts.insert(1, _stack_heads(vp_ref, prev_rows, [g]))
            k.append(jnp.concatenate(k_parts, axis=0))
            v.append(jnp.concatenate(v_parts, axis=0))
            q.append(_stack_heads(q_ref, slice(sub * CHUNK, (sub + 1) * CHUNK), _group_heads(g)))
            sink.append(_sink_rows(sink_ref, g))
            mask.append(visible)
    o = _softmax_pv(jnp.stack(q), jnp.stack(k), jnp.stack(v), jnp.stack(sink), jnp.stack(mask))
    for sub in range(SWA_RUN):
        for g in groups:
            for j, h in enumerate(_group_heads(g)):
                o_ref[sub * CHUNK:(sub + 1) * CHUNK, h * HEAD_DIM:(h + 1) * HEAD_DIM] = (
                    o[sub * N_KV_SWA + g, j * CHUNK:(j + 1) * CHUNK].astype(BF16))


def _attn_prompt(sinks, h_p, h_m):
    kcol = COL_K // D_SWA_KV
    vcol = COL_V // D_SWA_KV
    qcol = COL_Q // D_SWA_Q
    runs_per_seq = CHUNKS_PER_SEQ // SWA_RUN
    pairs_per_run = SWA_RUN // 2
    prev = lambda colblk: (lambda p: (p * pairs_per_run - jnp.where(p % runs_per_seq >= 1, 1, 0), colblk))
    kv_prev = lambda colblk: pl.BlockSpec((2 * CHUNK, D_SWA_KV), prev(colblk))
    kv_run = lambda colblk: pl.BlockSpec((SWA_RUN * CHUNK, D_SWA_KV), lambda p: (p, colblk))
    meta = lambda colblk: pl.BlockSpec((N_META, D_SWA_KV), lambda p: (META_ROW0 // N_META, colblk))
    return pl.pallas_call(
        _attn_run_body,
        grid=(N_PROMPT_ROWS // (SWA_RUN * CHUNK),),
        in_specs=[
            pl.BlockSpec(memory_space=pltpu.SMEM),
            pl.BlockSpec((SWA_RUN * CHUNK, D_SWA_Q), lambda p: (p, qcol)),
            meta(kcol), meta(vcol),
            kv_prev(kcol), kv_prev(vcol),
            kv_run(kcol), kv_run(vcol),
        ],
        out_specs=pl.BlockSpec((SWA_RUN * CHUNK, D_SWA_Q), lambda p: (p, 0)),
        out_shape=jax.ShapeDtypeStruct((N_PROMPT_ROWS, D_SWA_Q), BF16),
        compiler_params=pltpu.CompilerParams(dimension_semantics=("parallel",)),
        name="swa_prompt",
    )(sinks, h_p, h_m, h_m, h_p, h_p, h_p, h_p)


def _attn_sample(sinks, h_s, meta_k, meta_v, win_k, win_v):
    kcol = COL_K // D_SWA_KV
    vcol = COL_V // D_SWA_KV
    qcol = COL_Q // D_SWA_Q
    kv = lambda imap: pl.BlockSpec((CHUNK, D_SWA_KV), imap)
    return pl.pallas_call(
        _attn_sample_body,
        grid=(DEC_BATCH,),
        in_specs=[
            pl.BlockSpec(memory_space=pltpu.SMEM),
            pl.BlockSpec((CHUNK, D_SWA_Q), lambda s: (s, qcol)),
            pl.BlockSpec((N_META, D_SWA_KV), lambda s: (s, 0)),
            pl.BlockSpec((N_META, D_SWA_KV), lambda s: (s, 0)),
            kv(lambda s: (2 * s, 0)), kv(lambda s: (2 * s, 0)),
            kv(lambda s: (2 * s + 1, 0)), kv(lambda s: (2 * s + 1, 0)),
            kv(lambda s: (s, kcol)), kv(lambda s: (s, vcol)),
        ],
        out_specs=pl.BlockSpec((CHUNK, D_SWA_Q), lambda s: (s, 0)),
        out_shape=jax.ShapeDtypeStruct((N_SAMPLE_ROWS, D_SWA_Q), BF16),
        compiler_params=pltpu.CompilerParams(dimension_semantics=("parallel",)),
        name="swa_sample",
    )(sinks, h_s, meta_k, meta_v, win_k, win_v, win_k, win_v, h_s, h_s)


def _unit_lower_inverse(a, n):
    r = lax.broadcasted_iota(jnp.int32, (n, n), 0)
    c = lax.broadcasted_iota(jnp.int32, (n, n), 1)
    eye = (r == c).astype(F32)
    base = 3
    a0 = jnp.where((r >> base) == (c >> base), a, 0.0)
    a2 = _hdot(a0, a0)
    a4 = _hdot(a2, a2)
    x = _hdot(_hdot(eye - a0, eye + a2), eye + a4)
    shift = base
    while (1 << shift) < n:
        pair = jnp.logical_and((r >> (shift + 1)) == (c >> (shift + 1)), (r >> shift) != (c >> shift))
        ak = jnp.where(pair, a, 0.0)
        x = x - _hdot(x, _hdot(ak, x))
        shift += 1
    return x


def _gdn_body(x_ref, z_ref, t_ref, w_ref, alog_ref, dt_ref, nw_ref, s0_ref, b0_ref, *rest, C, n_chunks, n_cast):
    cast_src, (o_ref, sout_ref) = rest[:n_cast], rest[n_cast:n_cast + 2]
    cast_dst, (s_scr, xe_scr) = rest[n_cast + 2:2 * n_cast + 2], rest[2 * n_cast + 2:]
    c = pl.program_id(1)

    for src, dst in zip(cast_src, cast_dst):
        dst[...] = src[...].astype(BF16)

    @pl.when(c == 0)
    def _():
        s_scr[...] = s0_ref[0]
        xe_scr[0:CARRY, :] = b0_ref[0]

    @pl.when(c > 0)
    def _():
        xe_scr[0:CARRY, :] = xe_scr[C:C + CARRY, :]

    xe_scr[CARRY:CARRY + C, :] = x_ref[...]

    t = t_ref[...]
    beta = jax.nn.sigmoid(t)
    ta = t + dt_ref[...]
    softplus = jnp.maximum(ta, 0.0) + jnp.log(1.0 + jnp.exp(-jnp.abs(ta)))
    g = -jnp.exp(alog_ref[...]) * softplus
    r = lax.broadcasted_iota(jnp.int32, (C, C), 0)
    cc = lax.broadcasted_iota(jnp.int32, (C, C), 1)
    incl = r >= cc
    strict = r > cc
    gc = _dot(incl.astype(F32), g, HI)
    gc_t = gc.T
    heads = range(N_HEADS_GDN)
    bh = jnp.stack([beta[:, h:h + 1] for h in heads])
    gcol = jnp.stack([gc[:, N_HEADS_GDN + h:N_HEADS_GDN + h + 1] for h in heads])
    grow = jnp.stack([gc_t[N_HEADS_GDN + h:N_HEADS_GDN + h + 1, :] for h in heads])
    glast = gcol[:, C - 1:C, :]
    decay = jnp.where(incl, jnp.exp(jnp.where(incl, gcol - grow, 0.0)), 0.0)
    e_g = jnp.exp(gcol)

    xe = xe_scr[...]
    acc = w_ref[CONV_WIDTH - 1:CONV_WIDTH, :] * xe[CARRY:, :]
    for lag in range(1, CONV_WIDTH):
        tap = CONV_WIDTH - 1 - lag
        acc = acc + w_ref[tap:tap + 1, :] * pltpu.roll(xe, lag, 0)[CARRY:, :]
    y = _silu(acc)
    split = lambda base: jnp.stack([y[:, base + h * DK_GDN:base + (h + 1) * DK_GDN] for h in heads])
    q, k, v = split(0), split(D_GDN), split(2 * D_GDN)
    q = q * lax.rsqrt(jnp.sum(q * q, axis=-1, keepdims=True) + RMS_EPS) * (DK_GDN ** -0.5)
    k = k * lax.rsqrt(jnp.sum(k * k, axis=-1, keepdims=True) + RMS_EPS)
    kb = k * bh

    kq = _hdot_nt(jnp.concatenate([kb, q], axis=1), k)
    a = jnp.where(strict, kq[:, :C] * decay, 0.0)
    qk = kq[:, C:] * decay
    t_inv = _unit_lower_inverse(a, C)

    s_prev = s_scr[...]
    ws = _hdot(jnp.concatenate([kb * e_g, q * e_g], axis=1), s_prev)
    v_new = _hdot(t_inv, v * bh - ws[:, :C])
    o = ws[:, C:] + _hdot(qk, v_new)
    s_scr[...] = s_prev * jnp.exp(glast) + _hdot_tn(k * jnp.exp(glast - gcol), v_new)

    o = o * lax.rsqrt(jnp.mean(o * o, axis=-1, keepdims=True) + RMS_EPS) * nw_ref[...]
    for h in heads:
        hs = slice(h * DV_GDN, (h + 1) * DV_GDN)
        o_ref[:, hs] = (o[h] * _silu(z_ref[:, hs])).astype(BF16)

    @pl.when(c == n_chunks - 1)
    def _():
        sout_ref[0] = s_scr[...]


def _gdn(h_src, tail_src, conv_w, alog_row, dt_row, nw_row, s0, buf0, *, n_seq, n_chunks, C, shared_init,
         first_chunk=0, cast_jobs=()):
    init = (lambda s: 0) if shared_init else (lambda s: s)
    vec = pl.BlockSpec((1, TAIL_W), lambda s, c: (0, 0))
    src = lambda s, c: first_chunk + s * n_chunks + c
    in_specs = [
        pl.BlockSpec((C, D_CONV), lambda s, c: (src(s, c), COL_CONV // D_CONV)),
        pl.BlockSpec((C, D_GDN), lambda s, c: (src(s, c), COL_Z // D_GDN)),
        pl.BlockSpec((C, TAIL_W), lambda s, c: (src(s, c), 0)),
        pl.BlockSpec((CONV_WIDTH, D_CONV), lambda s, c: (0, 0)),
        vec, vec, vec,
        pl.BlockSpec((1, N_HEADS_GDN, DK_GDN, DV_GDN), lambda s, c: (init(s), 0, 0, 0)),
        pl.BlockSpec((1, CARRY, D_CONV), lambda s, c: (init(s), 0, 0)),
    ]
    out_specs = [
        pl.BlockSpec((C, D_GDN), lambda s, c: (s * n_chunks + c, 0)),
        pl.BlockSpec((1, N_HEADS_GDN, DK_GDN, DV_GDN), lambda s, c: (s, 0, 0, 0)),
    ]
    out_shape = [
        jax.ShapeDtypeStruct((n_seq * n_chunks * C, D_GDN), BF16),
        jax.ShapeDtypeStruct((n_seq, N_HEADS_GDN, DK_GDN, DV_GDN), F32),
    ]
    for w, prefix in cast_jobs:
        job_in, job_out, job_shape = _cast_job(w, prefix, n_seq * n_chunks, lambda s, c: s * n_chunks + c)
        in_specs.append(job_in)
        out_specs.append(job_out)
        out_shape.append(job_shape)
    return pl.pallas_call(
        functools.partial(_gdn_body, C=C, n_chunks=n_chunks, n_cast=len(cast_jobs)),
        grid=(n_seq, n_chunks),
        in_specs=in_specs,
        out_specs=out_specs,
        out_shape=out_shape,
        scratch_shapes=[
            pltpu.VMEM((N_HEADS_GDN, DK_GDN, DV_GDN), F32),
            pltpu.VMEM((CARRY + C, D_CONV), F32),
        ],
        compiler_params=pltpu.CompilerParams(
            dimension_semantics=("parallel", "arbitrary"), vmem_limit_bytes=VMEM_LIMIT),
        name="gdn",
    )(h_src, h_src, tail_src, conv_w, alog_row, dt_row, nw_row, s0, buf0, *[w for w, _ in cast_jobs])


def _out_body(a_ref, b_ref, w_ref, x_ref, g_ref, beta_ref, o_ref):
    o_ref[...] = ALPHA * x_ref[...] + _dot(a_ref[...], w_ref[0:D_SWA_Q, :]) + _dot(b_ref[...], w_ref[D_SWA_Q:, :])
    _layernorm_into(o_ref, lambda rows: o_ref[rows, :], g_ref, beta_ref)


def _out_proj(o_swa, o_gdn, w_out, x1, g, b, *, tm):
    rows = o_swa.shape[0]
    return pl.pallas_call(
        _out_body,
        grid=(rows // tm,),
        in_specs=[
            pl.BlockSpec((tm, D_SWA_Q), lambda i: (i, 0)),
            pl.BlockSpec((tm, D_GDN), lambda i: (i, 0)),
            pl.BlockSpec((D_MODEL, D_MODEL), lambda i: (0, 0), pipeline_mode=pl.Buffered(1)),
            pl.BlockSpec((tm, D_MODEL), lambda i: (i, 0)),
            pl.BlockSpec((1, D_MODEL), lambda i: (0, 0)),
            pl.BlockSpec((1, D_MODEL), lambda i: (0, 0)),
        ],
        out_specs=pl.BlockSpec((tm, D_MODEL), lambda i: (i, 0)),
        out_shape=jax.ShapeDtypeStruct((rows, D_MODEL), F32),
        compiler_params=pltpu.CompilerParams(
            dimension_semantics=("parallel",), vmem_limit_bytes=VMEM_LIMIT),
        name="proj_out",
    )(o_swa, o_gdn, w_out, x1, g, b)


def _rope_tables(pos):
    half = HEAD_DIM // 2
    inv = ROPE_THETA ** (-jnp.arange(half, dtype=F32) / half)
    ang = pos.astype(F32)[:, None] * inv[None, :]
    cos = jnp.cos(ang)
    sin = jnp.sin(ang)
    return jnp.concatenate([cos, cos], axis=1), jnp.concatenate([-sin, sin], axis=1)


def kernel(x_prompt, x_sample, cache_meta_k, cache_meta_v, cache_win_k, cache_win_v, state_conv, state_gdn,
           meta_tokens, ln_g, ln_b, ffn_w_gate, ffn_w_up, ffn_w_down, w_in, w_out, attn_sinks, conv_w,
           gdn_a_log, gdn_dt_bias, gdn_norm_w):
    l = 0
    wg1, wu1, wd1 = (w[l, 0].astype(BF16) for w in (ffn_w_gate, ffn_w_up, ffn_w_down))
    w_main = jnp.swapaxes(w_in[l], 0, 1)
    w_tail = jnp.pad(w_main[D_MAIN:], ((0, TAIL_W - 2 * N_HEADS_GDN), (0, 0))).astype(BF16)
    g1, g2, g3 = (ln_g[l, i][None, :] for i in range(3))
    b1, b2, b3 = (ln_b[l, i][None, :] for i in range(3))
    pad_tail = lambda v, off: jnp.pad(v.astype(F32), (off, TAIL_W - off - N_HEADS_GDN))[None, :]
    alog_row = pad_tail(gdn_a_log[l], N_HEADS_GDN)
    dt_row = pad_tail(gdn_dt_bias[l], N_HEADS_GDN)
    nw_row = gdn_norm_w[l].astype(F32)[None, :]
    cw = conv_w[l]
    sinks = attn_sinks[l].astype(F32)

    n_pad = N_SM_ROWS - N_SAMPLE_ROWS - N_META
    xp = x_prompt.reshape(N_PROMPT_ROWS, D_MODEL)
    xs = jnp.concatenate([x_sample.reshape(N_SAMPLE_ROWS, D_MODEL), meta_tokens.astype(F32),
                          jnp.zeros((n_pad, D_MODEL), F32)], axis=0)
    cos_p, sin_p = _rope_tables(jnp.tile(N_META + jnp.arange(SEQ, dtype=jnp.int32), BATCH))
    cos_s, sin_s = _rope_tables(jnp.concatenate([
        jnp.tile(N_META + PAST_LEN + jnp.arange(DEC_SEQ, dtype=jnp.int32), DEC_BATCH),
        jnp.arange(N_META, dtype=jnp.int32), jnp.zeros((n_pad,), jnp.int32)]))

    ffn1 = functools.partial(_ffn, wg=wg1, wu=wu1, wd=wd1, g=g1, b=b1, emit_bf16=True)
    x1_s, x1b_s = ffn1(xs, tm=SM_TILE)
    x1_p, x1b_p = ffn1(xp, tm=512)
    h_s, t_s, w_main_b = _proj(x1b_s, w_main, w_tail, cos_s, sin_s, tm=N_SM_ROWS, tn=PROJ_TN_F32,
                               emit_weights=True)
    h_p, t_p, wo = _proj(x1b_p, w_main_b, w_tail, cos_p, sin_p, tm=1024, tn=PROJ_TN_BF16, emit_weights=False,
                         cast_job=w_out[l])
    meta_rows = slice(META_ROW0, META_ROW0 + N_META)

    o_swa_p = _attn_prompt(sinks, h_p, h_s)
    o_swa_s = _attn_sample(
        sinks, h_s,
        cache_meta_k[l].reshape(DEC_BATCH * N_META, D_SWA_KV), cache_meta_v[l].reshape(DEC_BATCH * N_META, D_SWA_KV),
        cache_win_k[l].reshape(DEC_BATCH * WINDOW, D_SWA_KV), cache_win_v[l].reshape(DEC_BATCH * WINDOW, D_SWA_KV))

    zero_s = jnp.zeros((1, N_HEADS_GDN, DK_GDN, DV_GDN), F32)
    zero_buf = jnp.zeros((1, CARRY, D_CONV), F32)
    gdn = functools.partial(_gdn, conv_w=cw, alog_row=alog_row, dt_row=dt_row, nw_row=nw_row)
    _, s_meta = gdn(h_s, t_s, s0=zero_s, buf0=zero_buf, n_seq=1, n_chunks=1, C=N_META, shared_init=True,
                    first_chunk=META_ROW0 // N_META)[:2]
    buf_meta = h_s[None, META_ROW0 + N_META - CARRY:META_ROW0 + N_META, COL_CONV:COL_CONV + D_CONV]
    second = (l, 1)
    o_gdn_p, s_prompt, wg2, wu2, wd2 = gdn(
        h_p, t_p, s0=s_meta, buf0=buf_meta, n_seq=BATCH, n_chunks=CHUNKS_PER_SEQ, C=CHUNK, shared_init=True,
        cast_jobs=((ffn_w_gate, second), (ffn_w_up, second), (ffn_w_down, second)))
    buf_s = jnp.pad(state_conv[l].astype(F32), ((0, 0), (CARRY - (CONV_WIDTH - 1), 0), (0, 0)))
    o_gdn_s, s_sample = gdn(h_s, t_s, s0=state_gdn[l].astype(F32), buf0=buf_s, n_seq=DEC_BATCH, n_chunks=1,
                            C=CHUNK, shared_init=False)[:2]

    x2_p = _out_proj(o_swa_p, o_gdn_p, wo, x1_p, g2, b2, tm=256)
    x2_s = _out_proj(o_swa_s, o_gdn_s, wo, x1_s, g2, b2, tm=256)
    ffn2 = functools.partial(_ffn, wg=wg2, wu=wu2, wd=wd2, g=g3, b=b3, emit_bf16=False)
    (y_p,) = ffn2(x2_p, tm=512)
    (y_s,) = ffn2(x2_s, tm=512)

    y_prompt = y_p.reshape(BATCH, SEQ, D_MODEL)
    y_sample = y_s.reshape(DEC_BATCH, DEC_SEQ, D_MODEL)
    k_meta = h_s[meta_rows, COL_K:COL_K + D_SWA_KV].reshape(N_META, N_KV_SWA, HEAD_DIM)
    v_meta = h_s[meta_rows, COL_V:COL_V + D_SWA_KV].reshape(N_META, N_KV_SWA, HEAD_DIM)
    p_meta_k = jnp.broadcast_to(k_meta[None, None], (1, BATCH, N_META, N_KV_SWA, HEAD_DIM))
    p_meta_v = jnp.broadcast_to(v_meta[None, None], (1, BATCH, N_META, N_KV_SWA, HEAD_DIM))
    hp = h_p.reshape(BATCH, SEQ, D_MAIN)
    sample_cols = lambda c0, width: h_s[:N_SAMPLE_ROWS, c0:c0 + width].reshape(DEC_BATCH, DEC_SEQ, width)
    p_win_k = hp[:, SEQ - WINDOW:, COL_K:COL_K + D_SWA_KV].reshape(1, BATCH, WINDOW, N_KV_SWA, HEAD_DIM)
    p_win_v = hp[:, SEQ - WINDOW:, COL_V:COL_V + D_SWA_KV].reshape(1, BATCH, WINDOW, N_KV_SWA, HEAD_DIM)
    p_conv = hp[:, SEQ - (CONV_WIDTH - 1):, COL_CONV:COL_CONV + D_CONV][None]
    p_gdn = s_prompt[None]
    s_win_k = sample_cols(COL_K, D_SWA_KV).reshape(1, DEC_BATCH, DEC_SEQ, N_KV_SWA, HEAD_DIM)
    s_win_v = sample_cols(COL_V, D_SWA_KV).reshape(1, DEC_BATCH, DEC_SEQ, N_KV_SWA, HEAD_DIM)
    s_conv = sample_cols(COL_CONV, D_CONV)[:, DEC_SEQ - (CONV_WIDTH - 1):][None]
    s_gdn = s_sample[None]
    return (y_prompt, y_sample, p_meta_k, p_meta_v, p_win_k, p_win_v, p_conv, p_gdn, s_win_k, s_win_v, s_conv,
            s_gdn)
```

```python
import functools

import jax
import jax.numpy as jnp
from jax import lax
from jax.experimental import pallas as pl
from jax.experimental.pallas import tpu as pltpu

D_MODEL = 4096
BATCH = 4
SEQ = 2048
DEC_BATCH = 16
DEC_SEQ = 64
PAST_LEN = 2048
CHUNK = 64
N_META = 16
WINDOW = 128
HEAD_DIM = 128
N_HEADS_SWA = 16
N_KV_SWA = 4
GQA_GROUP = 4
N_HEADS_GDN = 16
DK_GDN = 128
DV_GDN = 128
CONV_WIDTH = 4
D_SWA_Q = N_HEADS_SWA * HEAD_DIM
D_SWA_KV = N_KV_SWA * HEAD_DIM
D_GDN = N_HEADS_GDN * DK_GDN
D_CONV = 3 * D_GDN
D_FF = 11008
ROPE_THETA = 10000.0
LN_EPS = 1e-5
RMS_EPS = 1e-6
ALPHA = 2.0 ** 0.25

N_PROMPT_ROWS = BATCH * SEQ
N_SAMPLE_ROWS = DEC_BATCH * DEC_SEQ
CHUNKS_PER_SEQ = SEQ // CHUNK
META_ROW0 = N_SAMPLE_ROWS
N_SM_ROWS = 1056
SM_HEAD_ROWS = 256
SM_REST_TILE = (N_SM_ROWS - SM_HEAD_ROWS) // 2

COL_CONV = 0
COL_Z = D_CONV
COL_Q = COL_Z + D_GDN
COL_K = COL_Q + D_SWA_Q
COL_V = COL_K + D_SWA_KV
D_MAIN = COL_V + D_SWA_KV
TAIL_W = 128

FFN_TF = 256
LN_ROWS_CHOICES = (80, 64, 16, 8)
LN_UNROLL = 2
PROJ_TN_F32 = 512
PROJ_TN_BF16 = 1024
CAST_STRIP_ROWS = (16, 32, 64, 128, 256)
CARRY = 8

V7X_VMEM_BYTES = 64 * 1024 * 1024
VMEM_LIMIT = V7X_VMEM_BYTES - 4 * 1024 * 1024

BF16 = jnp.bfloat16
F32 = jnp.float32
HI = lax.Precision.HIGHEST


def _dot(a, b, precision=None):
    return jnp.dot(a, b, preferred_element_type=F32, precision=precision)


def _dot_nt(a, b):
    return lax.dot_general(a, b, (((1,), (1,)), ((), ())), preferred_element_type=F32)


def _head_dot(a, b, lhs_contract, rhs_contract):
    dims = (((lhs_contract,), (rhs_contract,)), ((0,), (0,)))
    return lax.dot_general(a.astype(BF16), b.astype(BF16), dims, preferred_element_type=F32)


_hdot = functools.partial(_head_dot, lhs_contract=2, rhs_contract=1)
_hdot_nt = functools.partial(_head_dot, lhs_contract=2, rhs_contract=2)
_hdot_tn = functools.partial(_head_dot, lhs_contract=1, rhs_contract=1)


def _layernorm_rows(y, g, b):
    mu = jnp.mean(y, axis=-1, keepdims=True)
    d = y - mu
    var = jnp.mean(d * d, axis=-1, keepdims=True)
    return d * lax.rsqrt(var + LN_EPS) * g + b


def _layernorm_into(o_ref, pre_norm_rows, g_ref, b_ref):
    n_rows = o_ref.shape[0]
    per_chunk = next(r for r in LN_ROWS_CHOICES if n_rows % r == 0)

    def chunk(c, carry):
        rows = pl.ds(pl.multiple_of(c * per_chunk, per_chunk), per_chunk)
        o_ref[rows, :] = _layernorm_rows(pre_norm_rows(rows), g_ref[...], b_ref[...])
        return carry

    lax.fori_loop(0, n_rows // per_chunk, chunk, 0, unroll=LN_UNROLL)


def _silu(x):
    return x * jax.nn.sigmoid(x)


def _cast_job(w, prefix, n_steps, step):
    r, c = w.shape[len(prefix):]
    strip = next(s for s in CAST_STRIP_ROWS if r % s == 0 and r // s <= n_steps)
    last = r // strip - 1
    strip_of = lambda *ids: jnp.minimum(step(*ids), last)
    in_spec = pl.BlockSpec((None,) * len(prefix) + (strip, c), lambda *ids: prefix + (strip_of(*ids), 0))
    out_spec = pl.BlockSpec((strip, c), lambda *ids: (strip_of(*ids), 0))
    return in_spec, out_spec, jax.ShapeDtypeStruct((r, c), BF16)


def _ffn_body(x_ref, wg_ref, wu_ref, wd_ref, g_ref, b_ref, o_ref, *rest, emit_bf16, emit_weights):
    rest = list(rest)
    xb_ref = rest.pop()
    ob_ref = rest.pop(0) if emit_bf16 else None
    f = pl.program_id(1)

    @pl.when(f == 0)
    def _():
        xb_ref[...] = x_ref[...].astype(BF16)
        o_ref[...] = jnp.zeros_like(o_ref)

    if emit_weights:
        for src, dst in zip((wg_ref, wu_ref, wd_ref), rest):
            dst[...] = src[...].astype(BF16)
        wg_ref, wu_ref, wd_ref = rest
    xb = xb_ref[...]
    hidden = _silu(_dot(xb, wg_ref[...])) * _dot(xb, wu_ref[...])
    o_ref[...] += _dot(hidden.astype(BF16), wd_ref[...])

    @pl.when(f == pl.num_programs(1) - 1)
    def _():
        _layernorm_into(o_ref, lambda rows: ALPHA * x_ref[rows, :] + 0.5 * o_ref[rows, :], g_ref, b_ref)
        if emit_bf16:
            ob_ref[...] = o_ref[...].astype(BF16)


def _ffn(x, wg, wu, wd, g, b, *, tm, emit_bf16, f32_weights_at=None):
    rows = x.shape[0]
    row_spec = lambda: pl.BlockSpec((tm, D_MODEL), lambda i, f: (i, 0))
    out_shape = [jax.ShapeDtypeStruct((rows, D_MODEL), F32)]
    out_specs = [row_spec()]
    if emit_bf16:
        out_shape.append(jax.ShapeDtypeStruct((rows, D_MODEL), BF16))
        out_specs.append(row_spec())
    up_spec = pl.BlockSpec((D_MODEL, FFN_TF), lambda i, f: (0, f))
    down_spec = pl.BlockSpec((FFN_TF, D_MODEL), lambda i, f: (f, 0))
    w_specs = [up_spec, up_spec, down_spec]
    if f32_weights_at is not None:
        assert rows == tm
        out_specs += w_specs
        out_shape += [jax.ShapeDtypeStruct((D_MODEL, D_FF), BF16), jax.ShapeDtypeStruct((D_MODEL, D_FF), BF16),
                      jax.ShapeDtypeStruct((D_FF, D_MODEL), BF16)]
        w_specs = [pl.BlockSpec((None, None, D_MODEL, FFN_TF), lambda i, f: f32_weights_at + (0, f)),
                   pl.BlockSpec((None, None, D_MODEL, FFN_TF), lambda i, f: f32_weights_at + (0, f)),
                   pl.BlockSpec((None, None, FFN_TF, D_MODEL), lambda i, f: f32_weights_at + (f, 0))]
    return pl.pallas_call(
        functools.partial(_ffn_body, emit_bf16=emit_bf16, emit_weights=f32_weights_at is not None),
        grid=(rows // tm, D_FF // FFN_TF),
        in_specs=[
            row_spec(),
            *w_specs,
            pl.BlockSpec((1, D_MODEL), lambda i, f: (0, 0)),
            pl.BlockSpec((1, D_MODEL), lambda i, f: (0, 0)),
        ],
        out_specs=out_specs,
        out_shape=out_shape,
        scratch_shapes=[pltpu.VMEM((tm, D_MODEL), BF16)],
        compiler_params=pltpu.CompilerParams(
            dimension_semantics=("parallel", "arbitrary"), vmem_limit_bytes=VMEM_LIMIT),
        name="ffn",
    )(x, wg, wu, wd, g, b)


def _proj_body(x_ref, w_ref, wt_ref, cos_ref, sin_ref, *rest, tn, emit_weights, has_cast_job):
    rest = list(rest)
    cast_src = rest.pop(0) if has_cast_job else None
    o_ref, t_ref = rest[:2]
    j = pl.program_id(1)
    if has_cast_job:
        rest[-1][...] = cast_src[...].astype(BF16)
    wb = w_ref[...].astype(BF16)
    if emit_weights:
        rest[2][...] = wb
    acc = _dot_nt(x_ref[...], wb)
    n_full, n_part = divmod((D_SWA_Q + D_SWA_KV) // HEAD_DIM, tn // HEAD_DIM)

    def store(n_rope_heads):
        cos = cos_ref[...]
        sin = sin_ref[...]
        for s in range(tn // HEAD_DIM):
            cols = slice(s * HEAD_DIM, (s + 1) * HEAD_DIM)
            blk = acc[:, cols]
            o_ref[:, cols] = blk * cos + pltpu.roll(blk, HEAD_DIM // 2, 1) * sin if s < n_rope_heads else blk

    pl.when(j < n_full)(lambda: store(tn // HEAD_DIM))
    pl.when(j == n_full)(lambda: store(n_part))

    @pl.when(j > n_full)
    def _():
        o_ref[...] = acc

    @pl.when(j == 0)
    def _():
        t_ref[...] = _dot_nt(x_ref[...], wt_ref[...])


def _proj(xb, w_t, w_tail, cos, sin, *, tm, tn, emit_weights, cast_job=None):
    rows = xb.shape[0]
    grid = (rows // tm, D_MAIN // tn)
    n_swa = (D_SWA_Q + 2 * D_SWA_KV) // tn

    def dest(i, j):
        return i, jnp.where(j < n_swa, j + COL_Q // tn, j - n_swa)

    in_specs = [
        pl.BlockSpec((tm, D_MODEL), lambda i, j: (i, 0)),
        pl.BlockSpec((tn, D_MODEL), lambda i, j: (j, 0)),
        pl.BlockSpec((TAIL_W, D_MODEL), lambda i, j: (0, 0)),
        pl.BlockSpec((tm, HEAD_DIM), lambda i, j: (i, 0)),
        pl.BlockSpec((tm, HEAD_DIM), lambda i, j: (i, 0)),
    ]
    args = [xb, w_t, w_tail, cos, sin]
    out_specs = [pl.BlockSpec((tm, tn), dest), pl.BlockSpec((tm, TAIL_W), lambda i, j: (i, 0))]
    out_shape = [jax.ShapeDtypeStruct((rows, D_MAIN), F32), jax.ShapeDtypeStruct((rows, TAIL_W), F32)]
    if emit_weights:
        assert grid[0] == 1
        out_specs.append(pl.BlockSpec((tn, D_MODEL), lambda i, j: (j, 0)))
        out_shape.append(jax.ShapeDtypeStruct((D_MAIN, D_MODEL), BF16))
    if cast_job is not None:
        job_in, job_out, job_shape = _cast_job(cast_job, (), grid[0] * grid[1], lambda i, j: i * grid[1] + j)
        in_specs.append(job_in)
        args.append(cast_job)
        out_specs.append(job_out)
        out_shape.append(job_shape)
    return pl.pallas_call(
        functools.partial(_proj_body, tn=tn, emit_weights=emit_weights, has_cast_job=cast_job is not None),
        grid=grid,
        in_specs=in_specs,
        out_specs=out_specs,
        out_shape=out_shape,
        compiler_params=pltpu.CompilerParams(
            dimension_semantics=("parallel", "arbitrary"), vmem_limit_bytes=VMEM_LIMIT),
        name="proj_in",
    )(*args)


N_KEYS = N_META + 3 * CHUNK
SWA_RUN = 4


def _stack_heads(ref, rows, heads):
    return jnp.concatenate([ref[rows, h * HEAD_DIM:(h + 1) * HEAD_DIM] for h in heads], axis=0)


def _group_heads(g):
    return range(g * GQA_GROUP, (g + 1) * GQA_GROUP)


def _sink_rows(sink_ref, g):
    return jnp.concatenate([jnp.full((CHUNK, 1), sink_ref[h], F32) for h in _group_heads(g)], axis=0)


def _softmax_pv(q, k, v, sink, mask):
    s = _hdot_nt(q, k) * (HEAD_DIM ** -0.5)
    if mask is not None:
        s = jnp.where(mask, s, -jnp.inf)
    m = jnp.maximum(jnp.max(s, axis=-1, keepdims=True), sink)
    p = jnp.exp(s - m)
    den = jnp.sum(p, axis=-1, keepdims=True) + jnp.exp(sink - m)
    return _hdot(p, v) / den


def _attn_sample_body(sink_ref, q_ref, km_ref, vm_ref, k2_ref, v2_ref, k1_ref, v1_ref, k0_ref, v0_ref, o_ref):
    groups = range(N_KV_SWA)
    every = slice(None)
    k = jnp.stack([jnp.concatenate([_stack_heads(r, every, [g]) for r in (km_ref, k2_ref, k1_ref, k0_ref)], axis=0)
                   for g in groups])
    v = jnp.stack([jnp.concatenate([_stack_heads(r, every, [g]) for r in (vm_ref, v2_ref, v1_ref, v0_ref)], axis=0)
                   for g in groups])
    q = jnp.stack([_stack_heads(q_ref, every, _group_heads(g)) for g in groups])
    sink = jnp.stack([_sink_rows(sink_ref, g) for g in groups])
    o = _softmax_pv(q, k, v, sink, None)
    for g in groups:
        for j, h in enumerate(_group_heads(g)):
            o_ref[:, h * HEAD_DIM:(h + 1) * HEAD_DIM] = o[g, j * CHUNK:(j + 1) * CHUNK].astype(BF16)


def _attn_run_body(sink_ref, q_ref, km_ref, vm_ref, kp_ref, vp_ref, kc_ref, vc_ref, o_ref):
    has_prev = pl.program_id(0) % (CHUNKS_PER_SEQ // SWA_RUN) >= 1
    col = lax.broadcasted_iota(jnp.int32, (1, N_KEYS), 1)
    groups = range(N_KV_SWA)
    q, k, v, sink, mask = [], [], [], [], []
    for sub in range(SWA_RUN):
        n_prev = max(2 - sub, 0)
        prev_rows = slice((2 - n_prev) * CHUNK, 2 * CHUNK)
        cur_rows = slice(max(sub - 2, 0) * CHUNK, (sub + 1) * CHUNK)
        visible = jnp.logical_or(jnp.logical_or(col < N_META, col >= N_META + n_prev * CHUNK), has_prev)
        for g in groups:
            k_parts = [_stack_heads(km_ref, slice(None), [g]), _stack_heads(kc_ref, cur_rows, [g])]
            v_parts = [_stack_heads(vm_ref, slice(None), [g]), _stack_heads(vc_ref, cur_rows, [g])]
            if n_prev:
                k_parts.insert(1, _stack_heads(kp_ref, prev_rows, [g]))
                v_parts.insert(1, _stack_heads(vp_ref, prev_rows, [g]))
            k.append(jnp.concatenate(k_parts, axis=0))
            v.append(jnp.concatenate(v_parts, axis=0))
            q.append(_stack_heads(q_ref, slice(sub * CHUNK, (sub + 1) * CHUNK), _group_heads(g)))
            sink.append(_sink_rows(sink_ref, g))
            mask.append(visible)
    o = _softmax_pv(jnp.stack(q), jnp.stack(k), jnp.stack(v), jnp.stack(sink), jnp.stack(mask))
    for sub in range(SWA_RUN):
        for g in groups:
            for j, h in enumerate(_group_heads(g)):
                o_ref[sub * CHUNK:(sub + 1) * CHUNK, h * HEAD_DIM:(h + 1) * HEAD_DIM] = (
                    o[sub * N_KV_SWA + g, j * CHUNK:(j + 1) * CHUNK].astype(BF16))


def _attn_prompt(sinks, h_p, h_m):
    kcol = COL_K // D_SWA_KV
    vcol = COL_V // D_SWA_KV
    qcol = COL_Q // D_SWA_Q
    runs_per_seq = CHUNKS_PER_SEQ // SWA_RUN
    pairs_per_run = SWA_RUN // 2
    prev = lambda colblk: (lambda p: (p * pairs_per_run - jnp.where(p % runs_per_seq >= 1, 1, 0), colblk))
    kv_prev = lambda colblk: pl.BlockSpec((2 * CHUNK, D_SWA_KV), prev(colblk))
    kv_run = lambda colblk: pl.BlockSpec((SWA_RUN * CHUNK, D_SWA_KV), lambda p: (p, colblk))
    meta = lambda colblk: pl.BlockSpec((N_META, D_SWA_KV), lambda p: (META_ROW0 // N_META, colblk))
    return pl.pallas_call(
        _attn_run_body,
        grid=(N_PROMPT_ROWS // (SWA_RUN * CHUNK),),
        in_specs=[
            pl.BlockSpec(memory_space=pltpu.SMEM),
            pl.BlockSpec((SWA_RUN * CHUNK, D_SWA_Q), lambda p: (p, qcol)),
            meta(kcol), meta(vcol),
            kv_prev(kcol), kv_prev(vcol),
            kv_run(kcol), kv_run(vcol),
        ],
        out_specs=pl.BlockSpec((SWA_RUN * CHUNK, D_SWA_Q), lambda p: (p, 0)),
        out_shape=jax.ShapeDtypeStruct((N_PROMPT_ROWS, D_SWA_Q), BF16),
        compiler_params=pltpu.CompilerParams(dimension_semantics=("parallel",)),
        name="swa_prompt",
    )(sinks, h_p, h_m, h_m, h_p, h_p, h_p, h_p)


def _attn_sample(sinks, h_s, meta_k, meta_v, win_k, win_v):
    kcol = COL_K // D_SWA_KV
    vcol = COL_V // D_SWA_KV
    qcol = COL_Q // D_SWA_Q
    kv = lambda imap: pl.BlockSpec((CHUNK, D_SWA_KV), imap)
    return pl.pallas_call(
        _attn_sample_body,
        grid=(DEC_BATCH,),
        in_specs=[
            pl.BlockSpec(memory_space=pltpu.SMEM),
            pl.BlockSpec((CHUNK, D_SWA_Q), lambda s: (s, qcol)),
            pl.BlockSpec((N_META, D_SWA_KV), lambda s: (s, 0)),
            pl.BlockSpec((N_META, D_SWA_KV), lambda s: (s, 0)),
            kv(lambda s: (2 * s, 0)), kv(lambda s: (2 * s, 0)),
            kv(lambda s: (2 * s + 1, 0)), kv(lambda s: (2 * s + 1, 0)),
            kv(lambda s: (s, kcol)), kv(lambda s: (s, vcol)),
        ],
        out_specs=pl.BlockSpec((CHUNK, D_SWA_Q), lambda s: (s, 0)),
        out_shape=jax.ShapeDtypeStruct((N_SAMPLE_ROWS, D_SWA_Q), BF16),
        compiler_params=pltpu.CompilerParams(dimension_semantics=("parallel",)),
        name="swa_sample",
    )(sinks, h_s, meta_k, meta_v, win_k, win_v, win_k, win_v, h_s, h_s)


def _unit_lower_inverse(a, n):
    r = lax.broadcasted_iota(jnp.int32, (n, n), 0)
    c = lax.broadcasted_iota(jnp.int32, (n, n), 1)
    eye = (r == c).astype(F32)
    base = 3
    a0 = jnp.where((r >> base) == (c >> base), a, 0.0)
    a2 = _hdot(a0, a0)
    a4 = _hdot(a2, a2)
    x = _hdot(_hdot(eye - a0, eye + a2), eye + a4)
    shift = base
    while (1 << shift) < n:
        pair = jnp.logical_and((r >> (shift + 1)) == (c >> (shift + 1)), (r >> shift) != (c >> shift))
        ak = jnp.where(pair, a, 0.0)
        x = x - _hdot(x, _hdot(ak, x))
        shift += 1
    return x


def _gdn_body(x_ref, z_ref, t_ref, w_ref, alog_ref, dt_ref, nw_ref, s0_ref, b0_ref, *rest, C, n_chunks, n_cast):
    cast_src, (o_ref, sout_ref) = rest[:n_cast], rest[n_cast:n_cast + 2]
    cast_dst, (s_scr, xe_scr) = rest[n_cast + 2:2 * n_cast + 2], rest[2 * n_cast + 2:]
    c = pl.program_id(1)

    for src, dst in zip(cast_src, cast_dst):
        dst[...] = src[...].astype(BF16)

    @pl.when(c == 0)
    def _():
        s_scr[...] = s0_ref[0]
        xe_scr[0:CARRY, :] = b0_ref[0]

    @pl.when(c > 0)
    def _():
        xe_scr[0:CARRY, :] = xe_scr[C:C + CARRY, :]

    xe_scr[CARRY:CARRY + C, :] = x_ref[...]

    t = t_ref[...]
    beta = jax.nn.sigmoid(t)
    ta = t + dt_ref[...]
    softplus = jnp.maximum(ta, 0.0) + jnp.log(1.0 + jnp.exp(-jnp.abs(ta)))
    g = -jnp.exp(alog_ref[...]) * softplus
    r = lax.broadcasted_iota(jnp.int32, (C, C), 0)
    cc = lax.broadcasted_iota(jnp.int32, (C, C), 1)
    incl = r >= cc
    strict = r > cc
    gc = _dot(incl.astype(F32), g, HI)
    gc_t = gc.T
    heads = range(N_HEADS_GDN)
    bh = jnp.stack([beta[:, h:h + 1] for h in heads])
    gcol = jnp.stack([gc[:, N_HEADS_GDN + h:N_HEADS_GDN + h + 1] for h in heads])
    grow = jnp.stack([gc_t[N_HEADS_GDN + h:N_HEADS_GDN + h + 1, :] for h in heads])
    glast = gcol[:, C - 1:C, :]
    decay = jnp.where(incl, jnp.exp(jnp.where(incl, gcol - grow, 0.0)), 0.0)
    e_g = jnp.exp(gcol)

    xe = xe_scr[...]
    acc = w_ref[CONV_WIDTH - 1:CONV_WIDTH, :] * xe[CARRY:, :]
    for lag in range(1, CONV_WIDTH):
        tap = CONV_WIDTH - 1 - lag
        acc = acc + w_ref[tap:tap + 1, :] * pltpu.roll(xe, lag, 0)[CARRY:, :]
    y = _silu(acc)
    split = lambda base: jnp.stack([y[:, base + h * DK_GDN:base + (h + 1) * DK_GDN] for h in heads])
    q, k, v = split(0), split(D_GDN), split(2 * D_GDN)
    q = q * lax.rsqrt(jnp.sum(q * q, axis=-1, keepdims=True) + RMS_EPS) * (DK_GDN ** -0.5)
    k = k * lax.rsqrt(jnp.sum(k * k, axis=-1, keepdims=True) + RMS_EPS)
    kb = k * bh

    kq = _hdot_nt(jnp.concatenate([kb, q], axis=1), k)
    a = jnp.where(strict, kq[:, :C] * decay, 0.0)
    qk = kq[:, C:] * decay
    t_inv = _unit_lower_inverse(a, C)

    s_prev = s_scr[...]
    ws = _hdot(jnp.concatenate([kb * e_g, q * e_g], axis=1), s_prev)
    v_new = _hdot(t_inv, v * bh - ws[:, :C])
    o = ws[:, C:] + _hdot(qk, v_new)
    s_scr[...] = s_prev * jnp.exp(glast) + _hdot_tn(k * jnp.exp(glast - gcol), v_new)

    o = o * lax.rsqrt(jnp.mean(o * o, axis=-1, keepdims=True) + RMS_EPS) * nw_ref[...]
    for h in heads:
        hs = slice(h * DV_GDN, (h + 1) * DV_GDN)
        o_ref[:, hs] = (o[h] * _silu(z_ref[:, hs])).astype(BF16)

    @pl.when(c == n_chunks - 1)
    def _():
        sout_ref[0] = s_scr[...]


def _gdn(h_src, tail_src, conv_w, alog_row, dt_row, nw_row, s0, buf0, *, n_seq, n_chunks, C, shared_init,
         first_chunk=0, cast_jobs=()):
    init = (lambda s: 0) if shared_init else (lambda s: s)
    vec = pl.BlockSpec((1, TAIL_W), lambda s, c: (0, 0))
    src = lambda s, c: first_chunk + s * n_chunks + c
    in_specs = [
        pl.BlockSpec((C, D_CONV), lambda s, c: (src(s, c), COL_CONV // D_CONV)),
        pl.BlockSpec((C, D_GDN), lambda s, c: (src(s, c), COL_Z // D_GDN)),
        pl.BlockSpec((C, TAIL_W), lambda s, c: (src(s, c), 0)),
        pl.BlockSpec((CONV_WIDTH, D_CONV), lambda s, c: (0, 0)),
        vec, vec, vec,
        pl.BlockSpec((1, N_HEADS_GDN, DK_GDN, DV_GDN), lambda s, c: (init(s), 0, 0, 0)),
        pl.BlockSpec((1, CARRY, D_CONV), lambda s, c: (init(s), 0, 0)),
    ]
    out_specs = [
        pl.BlockSpec((C, D_GDN), lambda s, c: (s * n_chunks + c, 0)),
        pl.BlockSpec((1, N_HEADS_GDN, DK_GDN, DV_GDN), lambda s, c: (s, 0, 0, 0)),
    ]
    out_shape = [
        jax.ShapeDtypeStruct((n_seq * n_chunks * C, D_GDN), BF16),
        jax.ShapeDtypeStruct((n_seq, N_HEADS_GDN, DK_GDN, DV_GDN), F32),
    ]
    for w, prefix in cast_jobs:
        job_in, job_out, job_shape = _cast_job(w, prefix, n_seq * n_chunks, lambda s, c: s * n_chunks + c)
        in_specs.append(job_in)
        out_specs.append(job_out)
        out_shape.append(job_shape)
    return pl.pallas_call(
        functools.partial(_gdn_body, C=C, n_chunks=n_chunks, n_cast=len(cast_jobs)),
        grid=(n_seq, n_chunks),
        in_specs=in_specs,
        out_specs=out_specs,
        out_shape=out_shape,
        scratch_shapes=[
            pltpu.VMEM((N_HEADS_GDN, DK_GDN, DV_GDN), F32),
            pltpu.VMEM((CARRY + C, D_CONV), F32),
        ],
        compiler_params=pltpu.CompilerParams(
            dimension_semantics=("parallel", "arbitrary"), vmem_limit_bytes=VMEM_LIMIT),
        name="gdn",
    )(h_src, h_src, tail_src, conv_w, alog_row, dt_row, nw_row, s0, buf0, *[w for w, _ in cast_jobs])


def _out_body(a_ref, b_ref, w_ref, x_ref, g_ref, beta_ref, o_ref):
    o_ref[...] = ALPHA * x_ref[...] + _dot(a_ref[...], w_ref[0:D_SWA_Q, :]) + _dot(b_ref[...], w_ref[D_SWA_Q:, :])
    _layernorm_into(o_ref, lambda rows: o_ref[rows, :], g_ref, beta_ref)


def _out_proj(o_swa, o_gdn, w_out, x1, g, b, *, tm):
    rows = o_swa.shape[0]
    return pl.pallas_call(
        _out_body,
        grid=(rows // tm,),
        in_specs=[
            pl.BlockSpec((tm, D_SWA_Q), lambda i: (i, 0)),
            pl.BlockSpec((tm, D_GDN), lambda i: (i, 0)),
            pl.BlockSpec((D_MODEL, D_MODEL), lambda i: (0, 0), pipeline_mode=pl.Buffered(1)),
            pl.BlockSpec((tm, D_MODEL), lambda i: (i, 0)),
            pl.BlockSpec((1, D_MODEL), lambda i: (0, 0)),
            pl.BlockSpec((1, D_MODEL), lambda i: (0, 0)),
        ],
        out_specs=pl.BlockSpec((tm, D_MODEL), lambda i: (i, 0)),
        out_shape=jax.ShapeDtypeStruct((rows, D_MODEL), F32),
        compiler_params=pltpu.CompilerParams(
            dimension_semantics=("parallel",), vmem_limit_bytes=VMEM_LIMIT),
        name="proj_out",
    )(o_swa, o_gdn, w_out, x1, g, b)


def _rope_tables(pos):
    half = HEAD_DIM // 2
    inv = ROPE_THETA ** (-jnp.arange(half, dtype=F32) / half)
    ang = pos.astype(F32)[:, None] * inv[None, :]
    cos = jnp.cos(ang)
    sin = jnp.sin(ang)
    return jnp.concatenate([cos, cos], axis=1), jnp.concatenate([-sin, sin], axis=1)


def kernel(x_prompt, x_sample, cache_meta_k, cache_meta_v, cache_win_k, cache_win_v, state_conv, state_gdn,
           meta_tokens, ln_g, ln_b, ffn_w_gate, ffn_w_up, ffn_w_down, w_in, w_out, attn_sinks, conv_w,
           gdn_a_log, gdn_dt_bias, gdn_norm_w):
    l = 0
    w_main = jnp.swapaxes(w_in[l], 0, 1)
    w_tail = jnp.pad(w_main[D_MAIN:], ((0, TAIL_W - 2 * N_HEADS_GDN), (0, 0))).astype(BF16)
    g1, g2, g3 = (ln_g[l, i][None, :] for i in range(3))
    b1, b2, b3 = (ln_b[l, i][None, :] for i in range(3))
    pad_tail = lambda v, off: jnp.pad(v.astype(F32), (off, TAIL_W - off - N_HEADS_GDN))[None, :]
    alog_row = pad_tail(gdn_a_log[l], N_HEADS_GDN)
    dt_row = pad_tail(gdn_dt_bias[l], N_HEADS_GDN)
    nw_row = gdn_norm_w[l].astype(F32)[None, :]
    cw = conv_w[l]
    sinks = attn_sinks[l].astype(F32)

    n_pad = N_SM_ROWS - N_SAMPLE_ROWS - N_META
    xp = x_prompt.reshape(N_PROMPT_ROWS, D_MODEL)
    xs = x_sample.reshape(N_SAMPLE_ROWS, D_MODEL)
    xs_head = xs[:SM_HEAD_ROWS]
    xs_rest = jnp.concatenate([xs[SM_HEAD_ROWS:], meta_tokens.astype(F32), jnp.zeros((n_pad, D_MODEL), F32)], axis=0)
    cos_p, sin_p = _rope_tables(jnp.tile(N_META + jnp.arange(SEQ, dtype=jnp.int32), BATCH))
    cos_s, sin_s = _rope_tables(jnp.concatenate([
        jnp.tile(N_META + PAST_LEN + jnp.arange(DEC_SEQ, dtype=jnp.int32), DEC_BATCH),
        jnp.arange(N_META, dtype=jnp.int32), jnp.zeros((n_pad,), jnp.int32)]))

    x1_h, x1b_h, wg1, wu1, wd1 = _ffn(xs_head, ffn_w_gate, ffn_w_up, ffn_w_down, g1, b1, tm=SM_HEAD_ROWS,
                                      emit_bf16=True, f32_weights_at=(l, 0))
    ffn1 = functools.partial(_ffn, wg=wg1, wu=wu1, wd=wd1, g=g1, b=b1, emit_bf16=True)
    x1_r, x1b_r = ffn1(xs_rest, tm=SM_REST_TILE)
    x1_s = jnp.concatenate([x1_h, x1_r], axis=0)
    x1b_s = jnp.concatenate([x1b_h, x1b_r], axis=0)
    x1_p, x1b_p = ffn1(xp, tm=512)
    h_s, t_s, w_main_b = _proj(x1b_s, w_main, w_tail, cos_s, sin_s, tm=N_SM_ROWS, tn=PROJ_TN_F32,
                               emit_weights=True)
    h_p, t_p, wo = _proj(x1b_p, w_main_b, w_tail, cos_p, sin_p, tm=1024, tn=PROJ_TN_BF16, emit_weights=False,
                         cast_job=w_out[l])
    meta_rows = slice(META_ROW0, META_ROW0 + N_META)

    o_swa_p = _attn_prompt(sinks, h_p, h_s)
    o_swa_s = _attn_sample(
        sinks, h_s,
        cache_meta_k[l].reshape(DEC_BATCH * N_META, D_SWA_KV), cache_meta_v[l].reshape(DEC_BATCH * N_META, D_SWA_KV),
        cache_win_k[l].reshape(DEC_BATCH * WINDOW, D_SWA_KV), cache_win_v[l].reshape(DEC_BATCH * WINDOW, D_SWA_KV))

    zero_s = jnp.zeros((1, N_HEADS_GDN, DK_GDN, DV_GDN), F32)
    zero_buf = jnp.zeros((1, CARRY, D_CONV), F32)
    gdn = functools.partial(_gdn, conv_w=cw, alog_row=alog_row, dt_row=dt_row, nw_row=nw_row)
    _, s_meta = gdn(h_s, t_s, s0=zero_s, buf0=zero_buf, n_seq=1, n_chunks=1, C=N_META, shared_init=True,
                    first_chunk=META_ROW0 // N_META)[:2]
    buf_meta = h_s[None, META_ROW0 + N_META - CARRY:META_ROW0 + N_META, COL_CONV:COL_CONV + D_CONV]
    second = (l, 1)
    o_gdn_p, s_prompt, wg2, wu2, wd2 = gdn(
        h_p, t_p, s0=s_meta, buf0=buf_meta, n_seq=BATCH, n_chunks=CHUNKS_PER_SEQ, C=CHUNK, shared_init=True,
        cast_jobs=((ffn_w_gate, second), (ffn_w_up, second), (ffn_w_down, second)))
    buf_s = jnp.pad(state_conv[l].astype(F32), ((0, 0), (CARRY - (CONV_WIDTH - 1), 0), (0, 0)))
    o_gdn_s, s_sample = gdn(h_s, t_s, s0=state_gdn[l].astype(F32), buf0=buf_s, n_seq=DEC_BATCH, n_chunks=1,
                            C=CHUNK, shared_init=False)[:2]

    x2_p = _out_proj(o_swa_p, o_gdn_p, wo, x1_p, g2, b2, tm=256)
    x2_s = _out_proj(o_swa_s, o_gdn_s, wo, x1_s, g2, b2, tm=256)
    ffn2 = functools.partial(_ffn, wg=wg2, wu=wu2, wd=wd2, g=g3, b=b3, emit_bf16=False)
    (y_p,) = ffn2(x2_p, tm=512)
    (y_s,) = ffn2(x2_s, tm=512)

    y_prompt = y_p.reshape(BATCH, SEQ, D_MODEL)
    y_sample = y_s.reshape(DEC_BATCH, DEC_SEQ, D_MODEL)
    k_meta = h_s[meta_rows, COL_K:COL_K + D_SWA_KV].reshape(N_META, N_KV_SWA, HEAD_DIM)
    v_meta = h_s[meta_rows, COL_V:COL_V + D_SWA_KV].reshape(N_META, N_KV_SWA, HEAD_DIM)
    p_meta_k = jnp.broadcast_to(k_meta[None, None], (1, BATCH, N_META, N_KV_SWA, HEAD_DIM))
    p_meta_v = jnp.broadcast_to(v_meta[None, None], (1, BATCH, N_META, N_KV_SWA, HEAD_DIM))
    hp = h_p.reshape(BATCH, SEQ, D_MAIN)
    sample_cols = lambda c0, width: h_s[:N_SAMPLE_ROWS, c0:c0 + width].reshape(DEC_BATCH, DEC_SEQ, width)
    p_win_k = hp[:, SEQ - WINDOW:, COL_K:COL_K + D_SWA_KV].reshape(1, BATCH, WINDOW, N_KV_SWA, HEAD_DIM)
    p_win_v = hp[:, SEQ - WINDOW:, COL_V:COL_V + D_SWA_KV].reshape(1, BATCH, WINDOW, N_KV_SWA, HEAD_DIM)
    p_conv = hp[:, SEQ - (CONV_WIDTH - 1):, COL_CONV:COL_CONV + D_CONV][None]
    p_gdn = s_prompt[None]
    s_win_k = sample_cols(COL_K, D_SWA_KV).reshape(1, DEC_BATCH, DEC_SEQ, N_KV_SWA, HEAD_DIM)
    s_win_v = sample_cols(COL_V, D_SWA_KV).reshape(1, DEC_BATCH, DEC_SEQ, N_KV_SWA, HEAD_DIM)
    s_conv = sample_cols(COL_CONV, D_CONV)[:, DEC_SEQ - (CONV_WIDTH - 1):][None]
    s_gdn = s_sample[None]
    return (y_prompt, y_sample, p_meta_k, p_meta_v, p_win_k, p_win_v, p_conv, p_gdn, s_win_k, s_win_v, s_conv,
            s_gdn)
```

```python
import functools

import jax
import jax.numpy as jnp
from jax import lax
from jax.experimental import pallas as pl
from jax.experimental.pallas import tpu as pltpu

D_MODEL = 4096
BATCH = 4
SEQ = 2048
DEC_BATCH = 16
DEC_SEQ = 64
PAST_LEN = 2048
CHUNK = 64
N_META = 16
WINDOW = 128
HEAD_DIM = 128
N_HEADS_SWA = 16
N_KV_SWA = 4
GQA_GROUP = 4
N_HEADS_GDN = 16
DK_GDN = 128
DV_GDN = 128
CONV_WIDTH = 4
D_SWA_Q = N_HEADS_SWA * HEAD_DIM
D_SWA_KV = N_KV_SWA * HEAD_DIM
D_GDN = N_HEADS_GDN * DK_GDN
D_CONV = 3 * D_GDN
D_FF = 11008
ROPE_THETA = 10000.0
LN_EPS = 1e-5
RMS_EPS = 1e-6
ALPHA = 2.0 ** 0.25

N_PROMPT_ROWS = BATCH * SEQ
N_SAMPLE_ROWS = DEC_BATCH * DEC_SEQ
CHUNKS_PER_SEQ = SEQ // CHUNK
META_ROW0 = N_SAMPLE_ROWS
SM_TILE = 528
N_SM_ROWS = 2 * SM_TILE

COL_CONV = 0
COL_Z = D_CONV
COL_Q = COL_Z + D_GDN
COL_K = COL_Q + D_SWA_Q
COL_V = COL_K + D_SWA_KV
D_MAIN = COL_V + D_SWA_KV
TAIL_W = 128

FFN_TF = 256
LN_ROWS_CHOICES = (64, 48, 16, 8)
LN_UNROLL = 2
PROJ_TN_F32 = 512
PROJ_TN_BF16 = 1024
CAST_STRIP_ROWS = (16, 32, 64, 128, 256)
CARRY = 8

V7X_VMEM_BYTES = 64 * 1024 * 1024
VMEM_LIMIT = V7X_VMEM_BYTES - 4 * 1024 * 1024

BF16 = jnp.bfloat16
F32 = jnp.float32
HI = lax.Precision.HIGHEST


def _dot(a, b, precision=None):
    return jnp.dot(a, b, preferred_element_type=F32, precision=precision)


def _dot_nt(a, b):
    return lax.dot_general(a, b, (((1,), (1,)), ((), ())), preferred_element_type=F32)


def _head_dot(a, b, lhs_contract, rhs_contract):
    dims = (((lhs_contract,), (rhs_contract,)), ((0,), (0,)))
    return lax.dot_general(a.astype(BF16), b.astype(BF16), dims, preferred_element_type=F32)


_hdot = functools.partial(_head_dot, lhs_contract=2, rhs_contract=1)
_hdot_nt = functools.partial(_head_dot, lhs_contract=2, rhs_contract=2)
_hdot_tn = functools.partial(_head_dot, lhs_contract=1, rhs_contract=1)


def _layernorm_rows(y, g, b):
    mu = jnp.mean(y, axis=-1, keepdims=True)
    d = y - mu
    var = jnp.mean(d * d, axis=-1, keepdims=True)
    return d * lax.rsqrt(var + LN_EPS) * g + b


def _layernorm_into(o_ref, pre_norm_rows, g_ref, b_ref):
    n_rows = o_ref.shape[0]
    per_chunk = next(r for r in LN_ROWS_CHOICES if n_rows % r == 0)

    def chunk(c, carry):
        rows = pl.ds(pl.multiple_of(c * per_chunk, per_chunk), per_chunk)
        o_ref[rows, :] = _layernorm_rows(pre_norm_rows(rows), g_ref[...], b_ref[...])
        return carry

    lax.fori_loop(0, n_rows // per_chunk, chunk, 0, unroll=LN_UNROLL)


def _silu(x):
    return x * jax.nn.sigmoid(x)


def _cast_job(w, prefix, n_steps, step):
    r, c = w.shape[len(prefix):]
    strip = next(s for s in CAST_STRIP_ROWS if r % s == 0 and r // s <= n_steps)
    last = r // strip - 1
    strip_of = lambda *ids: jnp.minimum(step(*ids), last)
    in_spec = pl.BlockSpec((None,) * len(prefix) + (strip, c), lambda *ids: prefix + (strip_of(*ids), 0))
    out_spec = pl.BlockSpec((strip, c), lambda *ids: (strip_of(*ids), 0))
    return in_spec, out_spec, jax.ShapeDtypeStruct((r, c), BF16)


def _ffn_body(x_ref, wg_ref, wu_ref, wd_ref, g_ref, b_ref, o_ref, *rest, emit_bf16):
    if emit_bf16:
        ob_ref, xb_ref = rest
    else:
        (xb_ref,) = rest
    f = pl.program_id(1)

    @pl.when(f == 0)
    def _():
        xb_ref[...] = x_ref[...].astype(BF16)
        o_ref[...] = jnp.zeros_like(o_ref)

    xb = xb_ref[...]
    hidden = _silu(_dot(xb, wg_ref[...])) * _dot(xb, wu_ref[...])
    o_ref[...] += _dot(hidden.astype(BF16), wd_ref[...])

    @pl.when(f == pl.num_programs(1) - 1)
    def _():
        _layernorm_into(o_ref, lambda rows: ALPHA * x_ref[rows, :] + 0.5 * o_ref[rows, :], g_ref, b_ref)
        if emit_bf16:
            ob_ref[...] = o_ref[...].astype(BF16)


def _ffn(x, wg, wu, wd, g, b, *, tm, emit_bf16):
    rows = x.shape[0]
    row_spec = lambda: pl.BlockSpec((tm, D_MODEL), lambda i, f: (i, 0))
    out_shape = [jax.ShapeDtypeStruct((rows, D_MODEL), F32)]
    out_specs = [row_spec()]
    if emit_bf16:
        out_shape.append(jax.ShapeDtypeStruct((rows, D_MODEL), BF16))
        out_specs.append(row_spec())
    return pl.pallas_call(
        functools.partial(_ffn_body, emit_bf16=emit_bf16),
        grid=(rows // tm, D_FF // FFN_TF),
        in_specs=[
            row_spec(),
            pl.BlockSpec((D_MODEL, FFN_TF), lambda i, f: (0, f)),
            pl.BlockSpec((D_MODEL, FFN_TF), lambda i, f: (0, f)),
            pl.BlockSpec((FFN_TF, D_MODEL), lambda i, f: (f, 0)),
            pl.BlockSpec((1, D_MODEL), lambda i, f: (0, 0)),
            pl.BlockSpec((1, D_MODEL), lambda i, f: (0, 0)),
        ],
        out_specs=out_specs,
        out_shape=out_shape,
        scratch_shapes=[pltpu.VMEM((tm, D_MODEL), BF16)],
        compiler_params=pltpu.CompilerParams(
            dimension_semantics=("parallel", "arbitrary"), vmem_limit_bytes=VMEM_LIMIT),
        name="ffn",
    )(x, wg, wu, wd, g, b)


def _proj_body(x_ref, w_ref, wt_ref, cos_ref, sin_ref, *rest, tn, emit_weights, has_cast_job):
    rest = list(rest)
    cast_src = rest.pop(0) if has_cast_job else None
    o_ref, t_ref = rest[:2]
    j = pl.program_id(1)
    if has_cast_job:
        rest[-1][...] = cast_src[...].astype(BF16)
    wb = w_ref[...].astype(BF16)
    if emit_weights:
        rest[2][...] = wb
    acc = _dot_nt(x_ref[...], wb)
    n_full, n_part = divmod((D_SWA_Q + D_SWA_KV) // HEAD_DIM, tn // HEAD_DIM)

    def store(n_rope_heads):
        cos = cos_ref[...]
        sin = sin_ref[...]
        for s in range(tn // HEAD_DIM):
            cols = slice(s * HEAD_DIM, (s + 1) * HEAD_DIM)
            blk = acc[:, cols]
            o_ref[:, cols] = blk * cos + pltpu.roll(blk, HEAD_DIM // 2, 1) * sin if s < n_rope_heads else blk

    pl.when(j < n_full)(lambda: store(tn // HEAD_DIM))
    pl.when(j == n_full)(lambda: store(n_part))

    @pl.when(j > n_full)
    def _():
        o_ref[...] = acc

    @pl.when(j == 0)
    def _():
        t_ref[...] = _dot_nt(x_ref[...], wt_ref[...])


def _proj(xb, w_t, w_tail, cos, sin, *, tm, tn, emit_weights, cast_job=None):
    rows = xb.shape[0]
    grid = (rows // tm, D_MAIN // tn)
    n_swa = (D_SWA_Q + 2 * D_SWA_KV) // tn

    def dest(i, j):
        return i, jnp.where(j < n_swa, j + COL_Q // tn, j - n_swa)

    in_specs = [
        pl.BlockSpec((tm, D_MODEL), lambda i, j: (i, 0)),
        pl.BlockSpec((tn, D_MODEL), lambda i, j: (j, 0)),
        pl.BlockSpec((TAIL_W, D_MODEL), lambda i, j: (0, 0)),
        pl.BlockSpec((tm, HEAD_DIM), lambda i, j: (i, 0)),
        pl.BlockSpec((tm, HEAD_DIM), lambda i, j: (i, 0)),
    ]
    args = [xb, w_t, w_tail, cos, sin]
    out_specs = [pl.BlockSpec((tm, tn), dest), pl.BlockSpec((tm, TAIL_W), lambda i, j: (i, 0))]
    out_shape = [jax.ShapeDtypeStruct((rows, D_MAIN), F32), jax.ShapeDtypeStruct((rows, TAIL_W), F32)]
    if emit_weights:
        assert grid[0] == 1
        out_specs.append(pl.BlockSpec((tn, D_MODEL), lambda i, j: (j, 0)))
        out_shape.append(jax.ShapeDtypeStruct((D_MAIN, D_MODEL), BF16))
    if cast_job is not None:
        job_in, job_out, job_shape = _cast_job(cast_job, (), grid[0] * grid[1], lambda i, j: i * grid[1] + j)
        in_specs.append(job_in)
        args.append(cast_job)
        out_specs.append(job_out)
        out_shape.append(job_shape)
    return pl.pallas_call(
        functools.partial(_proj_body, tn=tn, emit_weights=emit_weights, has_cast_job=cast_job is not None),
        grid=grid,
        in_specs=in_specs,
        out_specs=out_specs,
        out_shape=out_shape,
        compiler_params=pltpu.CompilerParams(
            dimension_semantics=("parallel", "arbitrary"), vmem_limit_bytes=VMEM_LIMIT),
        name="proj_in",
    )(*args)


N_KEYS = N_META + 3 * CHUNK
SWA_RUN = 8


def _stack_heads(ref, rows, heads):
    return jnp.concatenate([ref[rows, h * HEAD_DIM:(h + 1) * HEAD_DIM] for h in heads], axis=0)


def _group_heads(g):
    return range(g * GQA_GROUP, (g + 1) * GQA_GROUP)


def _sink_rows(sink_ref, g):
    return jnp.concatenate([jnp.full((CHUNK, 1), sink_ref[h], F32) for h in _group_heads(g)], axis=0)


def _softmax_pv(q, k, v, sink, mask):
    s = _hdot_nt(q, k) * (HEAD_DIM ** -0.5)
    if mask is not None:
        s = jnp.where(mask, s, -jnp.inf)
    m = jnp.maximum(jnp.max(s, axis=-1, keepdims=True), sink)
    p = jnp.exp(s - m)
    den = jnp.sum(p, axis=-1, keepdims=True) + jnp.exp(sink - m)
    return _hdot(p, v) / den


def _attn_sample_body(sink_ref, q_ref, km_ref, vm_ref, k2_ref, v2_ref, k1_ref, v1_ref, k0_ref, v0_ref, o_ref):
    groups = range(N_KV_SWA)
    every = slice(None)
    k = jnp.stack([jnp.concatenate([_stack_heads(r, every, [g]) for r in (km_ref, k2_ref, k1_ref, k0_ref)], axis=0)
                   for g in groups])
    v = jnp.stack([jnp.concatenate([_stack_heads(r, every, [g]) for r in (vm_ref, v2_ref, v1_ref, v0_ref)], axis=0)
                   for g in groups])
    q = jnp.stack([_stack_heads(q_ref, every, _group_heads(g)) for g in groups])
    sink = jnp.stack([_sink_rows(sink_ref, g) for g in groups])
    o = _softmax_pv(q, k, v, sink, None)
    for g in groups:
        for j, h in enumerate(_group_heads(g)):
            o_ref[:, h * HEAD_DIM:(h + 1) * HEAD_DIM] = o[g, j * CHUNK:(j + 1) * CHUNK].astype(BF16)


def _attn_run_body(sink_ref, q_ref, km_ref, vm_ref, kp_ref, vp_ref, kc_ref, vc_ref, o_ref):
    has_prev = pl.program_id(0) % (CHUNKS_PER_SEQ // SWA_RUN) >= 1
    col = lax.broadcasted_iota(jnp.int32, (1, N_KEYS), 1)
    groups = range(N_KV_SWA)
    q, k, v, sink, mask = [], [], [], [], []
    for sub in range(SWA_RUN):
        n_prev = max(2 - sub, 0)
        prev_rows = slice((2 - n_prev) * CHUNK, 2 * CHUNK)
        cur_rows = slice(max(sub - 2, 0) * CHUNK, (sub + 1) * CHUNK)
        visible = jnp.logical_or(jnp.logical_or(col < N_META, col >= N_META + n_prev * CHUNK), has_prev)
        for g in groups:
            k_parts = [_stack_heads(km_ref, slice(None), [g]), _stack_heads(kc_ref, cur_rows, [g])]
            v_parts = [_stack_heads(vm_ref, slice(None), [g]), _stack_heads(vc_ref, cur_rows, [g])]
            if n_prev:
                k_parts.insert(1, _stack_heads(kp_ref, prev_rows, [g]))
                v_parts.insert(1, _stack_heads(vp_ref, prev_rows, [g]))
            k.append(jnp.concatenate(k_parts, axis=0))
            v.append(jnp.concatenate(v_parts, axis=0))
            q.append(_stack_heads(q_ref, slice(sub * CHUNK, (sub + 1) * CHUNK), _group_heads(g)))
            sink.append(_sink_rows(sink_ref, g))
            mask.append(visible)
    o = _softmax_pv(jnp.stack(q), jnp.stack(k), jnp.stack(v), jnp.stack(sink), jnp.stack(mask))
    for sub in range(SWA_RUN):
        for g in groups:
            for j, h in enumerate(_group_heads(g)):
                o_ref[sub * CHUNK:(sub + 1) * CHUNK, h * HEAD_DIM:(h + 1) * HEAD_DIM] = (
                    o[sub * N_KV_SWA + g, j * CHUNK:(j + 1) * CHUNK].astype(BF16))


def _attn_prompt(sinks, h_p, h_m):
    kcol = COL_K // D_SWA_KV
    vcol = COL_V // D_SWA_KV
    qcol = COL_Q // D_SWA_Q
    runs_per_seq = CHUNKS_PER_SEQ // SWA_RUN
    pairs_per_run = SWA_RUN // 2
    prev = lambda colblk: (lambda p: (p * pairs_per_run - jnp.where(p % runs_per_seq >= 1, 1, 0), colblk))
    kv_prev = lambda colblk: pl.BlockSpec((2 * CHUNK, D_SWA_KV), prev(colblk))
    kv_run = lambda colblk: pl.BlockSpec((SWA_RUN * CHUNK, D_SWA_KV), lambda p: (p, colblk))
    meta = lambda colblk: pl.BlockSpec((N_META, D_SWA_KV), lambda p: (META_ROW0 // N_META, colblk))
    return pl.pallas_call(
        _attn_run_body,
        grid=(N_PROMPT_ROWS // (SWA_RUN * CHUNK),),
        in_specs=[
            pl.BlockSpec(memory_space=pltpu.SMEM),
            pl.BlockSpec((SWA_RUN * CHUNK, D_SWA_Q), lambda p: (p, qcol)),
            meta(kcol), meta(vcol),
            kv_prev(kcol), kv_prev(vcol),
            kv_run(kcol), kv_run(vcol),
        ],
        out_specs=pl.BlockSpec((SWA_RUN * CHUNK, D_SWA_Q), lambda p: (p, 0)),
        out_shape=jax.ShapeDtypeStruct((N_PROMPT_ROWS, D_SWA_Q), BF16),
        compiler_params=pltpu.CompilerParams(dimension_semantics=("parallel",)),
        name="swa_prompt",
    )(sinks, h_p, h_m, h_m, h_p, h_p, h_p, h_p)


def _attn_sample(sinks, h_s, meta_k, meta_v, win_k, win_v):
    kcol = COL_K // D_SWA_KV
    vcol = COL_V // D_SWA_KV
    qcol = COL_Q // D_SWA_Q
    kv = lambda imap: pl.BlockSpec((CHUNK, D_SWA_KV), imap)
    return pl.pallas_call(
        _attn_sample_body,
        grid=(DEC_BATCH,),
        in_specs=[
            pl.BlockSpec(memory_space=pltpu.SMEM),
            pl.BlockSpec((CHUNK, D_SWA_Q), lambda s: (s, qcol)),
            pl.BlockSpec((N_META, D_SWA_KV), lambda s: (s, 0)),
            pl.BlockSpec((N_META, D_SWA_KV), lambda s: (s, 0)),
            kv(lambda s: (2 * s, 0)), kv(lambda s: (2 * s, 0)),
            kv(lambda s: (2 * s + 1, 0)), kv(lambda s: (2 * s + 1, 0)),
            kv(lambda s: (s, kcol)), kv(lambda s: (s, vcol)),
        ],
        out_specs=pl.BlockSpec((CHUNK, D_SWA_Q), lambda s: (s, 0)),
        out_shape=jax.ShapeDtypeStruct((N_SAMPLE_ROWS, D_SWA_Q), BF16),
        compiler_params=pltpu.CompilerParams(dimension_semantics=("parallel",)),
        name="swa_sample",
    )(sinks, h_s, meta_k, meta_v, win_k, win_v, win_k, win_v, h_s, h_s)


def _unit_lower_inverse(a, n):
    r = lax.broadcasted_iota(jnp.int32, (n, n), 0)
    c = lax.broadcasted_iota(jnp.int32, (n, n), 1)
    eye = (r == c).astype(F32)
    base = 3
    a0 = jnp.where((r >> base) == (c >> base), a, 0.0)
    a2 = _hdot(a0, a0)
    a4 = _hdot(a2, a2)
    x = _hdot(_hdot(eye - a0, eye + a2), eye + a4)
    shift = base
    while (1 << shift) < n:
        pair = jnp.logical_and((r >> (shift + 1)) == (c >> (shift + 1)), (r >> shift) != (c >> shift))
        ak = jnp.where(pair, a, 0.0)
        x = x - _hdot(x, _hdot(ak, x))
        shift += 1
    return x


def _gdn_body(x_ref, z_ref, t_ref, w_ref, alog_ref, dt_ref, nw_ref, s0_ref, b0_ref, *rest, C, n_chunks, n_cast):
    cast_src, (o_ref, sout_ref) = rest[:n_cast], rest[n_cast:n_cast + 2]
    cast_dst, (s_scr, xe_scr) = rest[n_cast + 2:2 * n_cast + 2], rest[2 * n_cast + 2:]
    c = pl.program_id(1)

    for src, dst in zip(cast_src, cast_dst):
        dst[...] = src[...].astype(BF16)

    @pl.when(c == 0)
    def _():
        s_scr[...] = s0_ref[0]
        xe_scr[0:CARRY, :] = b0_ref[0]

    @pl.when(c > 0)
    def _():
        xe_scr[0:CARRY, :] = xe_scr[C:C + CARRY, :]

    xe_scr[CARRY:CARRY + C, :] = x_ref[...]

    t = t_ref[...]
    beta = jax.nn.sigmoid(t)
    ta = t + dt_ref[...]
    softplus = jnp.maximum(ta, 0.0) + jnp.log(1.0 + jnp.exp(-jnp.abs(ta)))
    g = -jnp.exp(alog_ref[...]) * softplus
    r = lax.broadcasted_iota(jnp.int32, (C, C), 0)
    cc = lax.broadcasted_iota(jnp.int32, (C, C), 1)
    incl = r >= cc
    strict = r > cc
    gc = _dot(incl.astype(F32), g, HI)
    gc_t = gc.T
    heads = range(N_HEADS_GDN)
    bh = jnp.stack([beta[:, h:h + 1] for h in heads])
    gcol = jnp.stack([gc[:, N_HEADS_GDN + h:N_HEADS_GDN + h + 1] for h in heads])
    grow = jnp.stack([gc_t[N_HEADS_GDN + h:N_HEADS_GDN + h + 1, :] for h in heads])
    glast = gcol[:, C - 1:C, :]
    decay = jnp.where(incl, jnp.exp(jnp.where(incl, gcol - grow, 0.0)), 0.0)
    e_g = jnp.exp(gcol)

    xe = xe_scr[...]
    acc = w_ref[CONV_WIDTH - 1:CONV_WIDTH, :] * xe[CARRY:, :]
    for lag in range(1, CONV_WIDTH):
        tap = CONV_WIDTH - 1 - lag
        acc = acc + w_ref[tap:tap + 1, :] * pltpu.roll(xe, lag, 0)[CARRY:, :]
    y = _silu(acc)
    split = lambda base: jnp.stack([y[:, base + h * DK_GDN:base + (h + 1) * DK_GDN] for h in heads])
    q, k, v = split(0), split(D_GDN), split(2 * D_GDN)
    q = q * lax.rsqrt(jnp.sum(q * q, axis=-1, keepdims=True) + RMS_EPS) * (DK_GDN ** -0.5)
    k = k * lax.rsqrt(jnp.sum(k * k, axis=-1, keepdims=True) + RMS_EPS)
    kb = k * bh

    kq = _hdot_nt(jnp.concatenate([kb, q], axis=1), k)
    a = jnp.where(strict, kq[:, :C] * decay, 0.0)
    qk = kq[:, C:] * decay
    t_inv = _unit_lower_inverse(a, C)

    s_prev = s_scr[...]
    ws = _hdot(jnp.concatenate([kb * e_g, q * e_g], axis=1), s_prev)
    v_new = _hdot(t_inv, v * bh - ws[:, :C])
    o = ws[:, C:] + _hdot(qk, v_new)
    s_scr[...] = s_prev * jnp.exp(glast) + _hdot_tn(k * jnp.exp(glast - gcol), v_new)

    o = o * lax.rsqrt(jnp.mean(o * o, axis=-1, keepdims=True) + RMS_EPS) * nw_ref[...]
    for h in heads:
        hs = slice(h * DV_GDN, (h + 1) * DV_GDN)
        o_ref[:, hs] = (o[h] * _silu(z_ref[:, hs])).astype(BF16)

    @pl.when(c == n_chunks - 1)
    def _():
        sout_ref[0] = s_scr[...]


def _gdn(h_src, tail_src, conv_w, alog_row, dt_row, nw_row, s0, buf0, *, n_seq, n_chunks, C, shared_init,
         first_chunk=0, cast_jobs=()):
    init = (lambda s: 0) if shared_init else (lambda s: s)
    vec = pl.BlockSpec((1, TAIL_W), lambda s, c: (0, 0))
    src = lambda s, c: first_chunk + s * n_chunks + c
    in_specs = [
        pl.BlockSpec((C, D_CONV), lambda s, c: (src(s, c), COL_CONV // D_CONV)),
        pl.BlockSpec((C, D_GDN), lambda s, c: (src(s, c), COL_Z // D_GDN)),
        pl.BlockSpec((C, TAIL_W), lambda s, c: (src(s, c), 0)),
        pl.BlockSpec((CONV_WIDTH, D_CONV), lambda s, c: (0, 0)),
        vec, vec, vec,
        pl.BlockSpec((1, N_HEADS_GDN, DK_GDN, DV_GDN), lambda s, c: (init(s), 0, 0, 0)),
        pl.BlockSpec((1, CARRY, D_CONV), lambda s, c: (init(s), 0, 0)),
    ]
    out_specs = [
        pl.BlockSpec((C, D_GDN), lambda s, c: (s * n_chunks + c, 0)),
        pl.BlockSpec((1, N_HEADS_GDN, DK_GDN, DV_GDN), lambda s, c: (s, 0, 0, 0)),
    ]
    out_shape = [
        jax.ShapeDtypeStruct((n_seq * n_chunks * C, D_GDN), BF16),
        jax.ShapeDtypeStruct((n_seq, N_HEADS_GDN, DK_GDN, DV_GDN), F32),
    ]
    for w, prefix in cast_jobs:
        job_in, job_out, job_shape = _cast_job(w, prefix, n_seq * n_chunks, lambda s, c: s * n_chunks + c)
        in_specs.append(job_in)
        out_specs.append(job_out)
        out_shape.append(job_shape)
    return pl.pallas_call(
        functools.partial(_gdn_body, C=C, n_chunks=n_chunks, n_cast=len(cast_jobs)),
        grid=(n_seq, n_chunks),
        in_specs=in_specs,
        out_specs=out_specs,
        out_shape=out_shape,
        scratch_shapes=[
            pltpu.VMEM((N_HEADS_GDN, DK_GDN, DV_GDN), F32),
            pltpu.VMEM((CARRY + C, D_CONV), F32),
        ],
        compiler_params=pltpu.CompilerParams(
            dimension_semantics=("parallel", "arbitrary"), vmem_limit_bytes=VMEM_LIMIT),
        name="gdn",
    )(h_src, h_src, tail_src, conv_w, alog_row, dt_row, nw_row, s0, buf0, *[w for w, _ in cast_jobs])


def _out_body(a_ref, b_ref, w_ref, x_ref, g_ref, beta_ref, o_ref):
    o_ref[...] = ALPHA * x_ref[...] + _dot(a_ref[...], w_ref[0:D_SWA_Q, :]) + _dot(b_ref[...], w_ref[D_SWA_Q:, :])
    _layernorm_into(o_ref, lambda rows: o_ref[rows, :], g_ref, beta_ref)


def _out_proj(o_swa, o_gdn, w_out, x1, g, b, *, tm):
    rows = o_swa.shape[0]
    return pl.pallas_call(
        _out_body,
        grid=(rows // tm,),
        in_specs=[
            pl.BlockSpec((tm, D_SWA_Q), lambda i: (i, 0)),
            pl.BlockSpec((tm, D_GDN), lambda i: (i, 0)),
            pl.BlockSpec((D_MODEL, D_MODEL), lambda i: (0, 0), pipeline_mode=pl.Buffered(1)),
            pl.BlockSpec((tm, D_MODEL), lambda i: (i, 0)),
            pl.BlockSpec((1, D_MODEL), lambda i: (0, 0)),
            pl.BlockSpec((1, D_MODEL), lambda i: (0, 0)),
        ],
        out_specs=pl.BlockSpec((tm, D_MODEL), lambda i: (i, 0)),
        out_shape=jax.ShapeDtypeStruct((rows, D_MODEL), F32),
        compiler_params=pltpu.CompilerParams(
            dimension_semantics=("parallel",), vmem_limit_bytes=VMEM_LIMIT),
        name="proj_out",
    )(o_swa, o_gdn, w_out, x1, g, b)


def _rope_tables(pos):
    half = HEAD_DIM // 2
    inv = ROPE_THETA ** (-jnp.arange(half, dtype=F32) / half)
    ang = pos.astype(F32)[:, None] * inv[None, :]
    cos = jnp.cos(ang)
    sin = jnp.sin(ang)
    return jnp.concatenate([cos, cos], axis=1), jnp.concatenate([-sin, sin], axis=1)


def kernel(x_prompt, x_sample, cache_meta_k, cache_meta_v, cache_win_k, cache_win_v, state_conv, state_gdn,
           meta_tokens, ln_g, ln_b, ffn_w_gate, ffn_w_up, ffn_w_down, w_in, w_out, attn_sinks, conv_w,
           gdn_a_log, gdn_dt_bias, gdn_norm_w):
    l = 0
    wg1, wu1, wd1 = (w[l, 0].astype(BF16) for w in (ffn_w_gate, ffn_w_up, ffn_w_down))
    w_main = jnp.swapaxes(w_in[l], 0, 1)
    w_tail = jnp.pad(w_main[D_MAIN:], ((0, TAIL_W - 2 * N_HEADS_GDN), (0, 0))).astype(BF16)
    g1, g2, g3 = (ln_g[l, i][None, :] for i in range(3))
    b1, b2, b3 = (ln_b[l, i][None, :] for i in range(3))
    pad_tail = lambda v, off: jnp.pad(v.astype(F32), (off, TAIL_W - off - N_HEADS_GDN))[None, :]
    alog_row = pad_tail(gdn_a_log[l], N_HEADS_GDN)
    dt_row = pad_tail(gdn_dt_bias[l], N_HEADS_GDN)
    nw_row = gdn_norm_w[l].astype(F32)[None, :]
    cw = conv_w[l]
    sinks = attn_sinks[l].astype(F32)

    n_pad = N_SM_ROWS - N_SAMPLE_ROWS - N_META
    xp = x_prompt.reshape(N_PROMPT_ROWS, D_MODEL)
    xs = jnp.concatenate([x_sample.reshape(N_SAMPLE_ROWS, D_MODEL), meta_tokens.astype(F32),
                          jnp.zeros((n_pad, D_MODEL), F32)], axis=0)
    cos_p, sin_p = _rope_tables(jnp.tile(N_META + jnp.arange(SEQ, dtype=jnp.int32), BATCH))
    cos_s, sin_s = _rope_tables(jnp.concatenate([
        jnp.tile(N_META + PAST_LEN + jnp.arange(DEC_SEQ, dtype=jnp.int32), DEC_BATCH),
        jnp.arange(N_META, dtype=jnp.int32), jnp.zeros((n_pad,), jnp.int32)]))

    ffn1 = functools.partial(_ffn, wg=wg1, wu=wu1, wd=wd1, g=g1, b=b1, emit_bf16=True)
    x1_s, x1b_s = ffn1(xs, tm=SM_TILE)
    x1_p, x1b_p = ffn1(xp, tm=512)
    h_s, t_s, w_main_b = _proj(x1b_s, w_main, w_tail, cos_s, sin_s, tm=N_SM_ROWS, tn=PROJ_TN_F32,
                               emit_weights=True)
    h_p, t_p, wo = _proj(x1b_p, w_main_b, w_tail, cos_p, sin_p, tm=1024, tn=PROJ_TN_BF16, emit_weights=False,
                         cast_job=w_out[l])
    meta_rows = slice(META_ROW0, META_ROW0 + N_META)

    o_swa_p = _attn_prompt(sinks, h_p, h_s)
    o_swa_s = _attn_sample(
        sinks, h_s,
        cache_meta_k[l].reshape(DEC_BATCH * N_META, D_SWA_KV), cache_meta_v[l].reshape(DEC_BATCH * N_META, D_SWA_KV),
        cache_win_k[l].reshape(DEC_BATCH * WINDOW, D_SWA_KV), cache_win_v[l].reshape(DEC_BATCH * WINDOW, D_SWA_KV))

    zero_s = jnp.zeros((1, N_HEADS_GDN, DK_GDN, DV_GDN), F32)
    zero_buf = jnp.zeros((1, CARRY, D_CONV), F32)
    gdn = functools.partial(_gdn, conv_w=cw, alog_row=alog_row, dt_row=dt_row, nw_row=nw_row)
    _, s_meta = gdn(h_s, t_s, s0=zero_s, buf0=zero_buf, n_seq=1, n_chunks=1, C=N_META, shared_init=True,
                    first_chunk=META_ROW0 // N_META)[:2]
    buf_meta = h_s[None, META_ROW0 + N_META - CARRY:META_ROW0 + N_META, COL_CONV:COL_CONV + D_CONV]
    second = (l, 1)
    o_gdn_p, s_prompt, wg2, wu2, wd2 = gdn(
        h_p, t_p, s0=s_meta, buf0=buf_meta, n_seq=BATCH, n_chunks=CHUNKS_PER_SEQ, C=CHUNK, shared_init=True,
        cast_jobs=((ffn_w_gate, second), (ffn_w_up, second), (ffn_w_down, second)))
    buf_s = jnp.pad(state_conv[l].astype(F32), ((0, 0), (CARRY - (CONV_WIDTH - 1), 0), (0, 0)))
    o_gdn_s, s_sample = gdn(h_s, t_s, s0=state_gdn[l].astype(F32), buf0=buf_s, n_seq=DEC_BATCH, n_chunks=1,
                            C=CHUNK, shared_init=False)[:2]

    x2_p = _out_proj(o_swa_p, o_gdn_p, wo, x1_p, g2, b2, tm=256)
    x2_s = _out_proj(o_swa_s, o_gdn_s, wo, x1_s, g2, b2, tm=256)
    ffn2 = functools.partial(_ffn, wg=wg2, wu=wu2, wd=wd2, g=g3, b=b3, emit_bf16=False)
    (y_p,) = ffn2(x2_p, tm=512)
    (y_s,) = ffn2(x2_s, tm=512)

    y_prompt = y_p.reshape(BATCH, SEQ, D_MODEL)
    y_sample = y_s.reshape(DEC_BATCH, DEC_SEQ, D_MODEL)
    k_meta = h_s[meta_rows, COL_K:COL_K + D_SWA_KV].reshape(N_META, N_KV_SWA, HEAD_DIM)
    v_meta = h_s[meta_rows, COL_V:COL_V + D_SWA_KV].reshape(N_META, N_KV_SWA, HEAD_DIM)
    p_meta_k = jnp.broadcast_to(k_meta[None, None], (1, BATCH, N_META, N_KV_SWA, HEAD_DIM))
    p_meta_v = jnp.broadcast_to(v_meta[None, None], (1, BATCH, N_META, N_KV_SWA, HEAD_DIM))
    hp = h_p.reshape(BATCH, SEQ, D_MAIN)
    sample_cols = lambda c0, width: h_s[:N_SAMPLE_ROWS, c0:c0 + width].reshape(DEC_BATCH, DEC_SEQ, width)
    p_win_k = hp[:, SEQ - WINDOW:, COL_K:COL_K + D_SWA_KV].reshape(1, BATCH, WINDOW, N_KV_SWA, HEAD_DIM)
    p_win_v = hp[:, SEQ - WINDOW:, COL_V:COL_V + D_SWA_KV].reshape(1, BATCH, WINDOW, N_KV_SWA, HEAD_DIM)
    p_conv = hp[:, SEQ - (CONV_WIDTH - 1):, COL_CONV:COL_CONV + D_CONV][None]
    p_gdn = s_prompt[None]
    s_win_k = sample_cols(COL_K, D_SWA_KV).reshape(1, DEC_BATCH, DEC_SEQ, N_KV_SWA, HEAD_DIM)
    s_win_v = sample_cols(COL_V, D_SWA_KV).reshape(1, DEC_BATCH, DEC_SEQ, N_KV_SWA, HEAD_DIM)
    s_conv = sample_cols(COL_CONV, D_CONV)[:, DEC_SEQ - (CONV_WIDTH - 1):][None]
    s_gdn = s_sample[None]
    return (y_prompt, y_sample, p_meta_k, p_meta_v, p_win_k, p_win_v, p_conv, p_gdn, s_win_k, s_win_v, s_conv,
            s_gdn)
```

```python
import functools

import jax
import jax.numpy as jnp
from jax import lax
from jax.experimental import pallas as pl
from jax.experimental.pallas import tpu as pltpu

D_MODEL = 4096
BATCH = 4
SEQ = 2048
DEC_BATCH = 16
DEC_SEQ = 64
PAST_LEN = 2048
CHUNK = 64
N_META = 16
WINDOW = 128
HEAD_DIM = 128
N_HEADS_SWA = 16
N_KV_SWA = 4
GQA_GROUP = 4
N_HEADS_GDN = 16
DK_GDN = 128
DV_GDN = 128
CONV_WIDTH = 4
D_SWA_Q = N_HEADS_SWA * HEAD_DIM
D_SWA_KV = N_KV_SWA * HEAD_DIM
D_GDN = N_HEADS_GDN * DK_GDN
D_CONV = 3 * D_GDN
D_FF = 11008
ROPE_THETA = 10000.0
LN_EPS = 1e-5
RMS_EPS = 1e-6
ALPHA = 2.0 ** 0.25

N_PROMPT_ROWS = BATCH * SEQ
N_SAMPLE_ROWS = DEC_BATCH * DEC_SEQ
CHUNKS_PER_SEQ = SEQ // CHUNK
META_ROW0 = N_SAMPLE_ROWS
SM_TILE = 528
N_SM_ROWS = 2 * SM_TILE

COL_CONV = 0
COL_Z = D_CONV
COL_Q = COL_Z + D_GDN
COL_K = COL_Q + D_SWA_Q
COL_V = COL_K + D_SWA_KV
D_MAIN = COL_V + D_SWA_KV
TAIL_W = 128

FFN_TF = 256
LN_ROWS_CHOICES = (64, 48, 16, 8)
LN_UNROLL = 2
PROJ_TN_F32 = 512
PROJ_TN_BF16 = 1024
CAST_STRIP_ROWS = (16, 32, 64, 128, 256)
CARRY = 8

V7X_VMEM_BYTES = 64 * 1024 * 1024
VMEM_LIMIT = V7X_VMEM_BYTES - 4 * 1024 * 1024

BF16 = jnp.bfloat16
F32 = jnp.float32
HI = lax.Precision.HIGHEST


def _dot(a, b, precision=None):
    return jnp.dot(a, b, preferred_element_type=F32, precision=precision)


def _dot_nt(a, b):
    return lax.dot_general(a, b, (((1,), (1,)), ((), ())), preferred_element_type=F32)


def _head_dot(a, b, lhs_contract, rhs_contract):
    dims = (((lhs_contract,), (rhs_contract,)), ((0,), (0,)))
    return lax.dot_general(a.astype(BF16), b.astype(BF16), dims, preferred_element_type=F32)


_hdot = functools.partial(_head_dot, lhs_contract=2, rhs_contract=1)
_hdot_nt = functools.partial(_head_dot, lhs_contract=2, rhs_contract=2)
_hdot_tn = functools.partial(_head_dot, lhs_contract=1, rhs_contract=1)


def _layernorm_rows(y, g, b):
    mu = jnp.mean(y, axis=-1, keepdims=True)
    d = y - mu
    var = jnp.mean(d * d, axis=-1, keepdims=True)
    return d * lax.rsqrt(var + LN_EPS) * g + b


def _layernorm_into(o_ref, pre_norm_rows, g_ref, b_ref):
    n_rows = o_ref.shape[0]
    per_chunk = next(r for r in LN_ROWS_CHOICES if n_rows % r == 0)

    def chunk(c, carry):
        rows = pl.ds(pl.multiple_of(c * per_chunk, per_chunk), per_chunk)
        o_ref[rows, :] = _layernorm_rows(pre_norm_rows(rows), g_ref[...], b_ref[...])
        return carry

    lax.fori_loop(0, n_rows // per_chunk, chunk, 0, unroll=LN_UNROLL)


def _silu(x):
    return x * jax.nn.sigmoid(x)


def _cast_job(w, prefix, n_steps, step):
    r, c = w.shape[len(prefix):]
    strip = next(s for s in CAST_STRIP_ROWS if r % s == 0 and r // s <= n_steps)
    last = r // strip - 1
    strip_of = lambda *ids: jnp.minimum(step(*ids), last)
    in_spec = pl.BlockSpec((None,) * len(prefix) + (strip, c), lambda *ids: prefix + (strip_of(*ids), 0))
    out_spec = pl.BlockSpec((strip, c), lambda *ids: (strip_of(*ids), 0))
    return in_spec, out_spec, jax.ShapeDtypeStruct((r, c), BF16)


def _ffn_body(x_ref, wg_ref, wu_ref, wd_ref, g_ref, b_ref, o_ref, *rest, emit_bf16):
    if emit_bf16:
        ob_ref, xb_ref = rest
    else:
        (xb_ref,) = rest
    f = pl.program_id(1)

    @pl.when(f == 0)
    def _():
        xb_ref[...] = x_ref[...].astype(BF16)
        o_ref[...] = jnp.zeros_like(o_ref)

    xb = xb_ref[...]
    hidden = _silu(_dot(xb, wg_ref[...])) * _dot(xb, wu_ref[...])
    o_ref[...] += _dot(hidden.astype(BF16), wd_ref[...])

    @pl.when(f == pl.num_programs(1) - 1)
    def _():
        _layernorm_into(o_ref, lambda rows: ALPHA * x_ref[rows, :] + 0.5 * o_ref[rows, :], g_ref, b_ref)
        if emit_bf16:
            ob_ref[...] = o_ref[...].astype(BF16)


def _ffn(x, wg, wu, wd, g, b, *, tm, emit_bf16):
    rows = x.shape[0]
    row_spec = lambda: pl.BlockSpec((tm, D_MODEL), lambda i, f: (i, 0))
    out_shape = [jax.ShapeDtypeStruct((rows, D_MODEL), F32)]
    out_specs = [row_spec()]
    if emit_bf16:
        out_shape.append(jax.ShapeDtypeStruct((rows, D_MODEL), BF16))
        out_specs.append(row_spec())
    return pl.pallas_call(
        functools.partial(_ffn_body, emit_bf16=emit_bf16),
        grid=(rows // tm, D_FF // FFN_TF),
        in_specs=[
            row_spec(),
            pl.BlockSpec((D_MODEL, FFN_TF), lambda i, f: (0, f)),
            pl.BlockSpec((D_MODEL, FFN_TF), lambda i, f: (0, f)),
            pl.BlockSpec((FFN_TF, D_MODEL), lambda i, f: (f, 0)),
            pl.BlockSpec((1, D_MODEL), lambda i, f: (0, 0)),
            pl.BlockSpec((1, D_MODEL), lambda i, f: (0, 0)),
        ],
        out_specs=out_specs,
        out_shape=out_shape,
        scratch_shapes=[pltpu.VMEM((tm, D_MODEL), BF16)],
        compiler_params=pltpu.CompilerParams(
            dimension_semantics=("parallel", "arbitrary"), vmem_limit_bytes=VMEM_LIMIT),
        name="ffn",
    )(x, wg, wu, wd, g, b)


def _proj_body(x_ref, w_ref, wt_ref, cos_ref, sin_ref, *rest, tn, emit_weights, n_cast):
    cast_src, rest = rest[:n_cast], rest[n_cast:]
    o_ref, t_ref = rest[:2]
    j = pl.program_id(1)
    for src, dst in zip(cast_src, rest[len(rest) - n_cast:]):
        dst[...] = src[...].astype(BF16)
    wb = w_ref[...].astype(BF16)
    if emit_weights:
        rest[2][...] = wb
    acc = _dot_nt(x_ref[...], wb)
    n_full, n_part = divmod((D_SWA_Q + D_SWA_KV) // HEAD_DIM, tn // HEAD_DIM)

    def store(n_rope_heads):
        cos = cos_ref[...]
        sin = sin_ref[...]
        for s in range(tn // HEAD_DIM):
            cols = slice(s * HEAD_DIM, (s + 1) * HEAD_DIM)
            blk = acc[:, cols]
            o_ref[:, cols] = blk * cos + pltpu.roll(blk, HEAD_DIM // 2, 1) * sin if s < n_rope_heads else blk

    pl.when(j < n_full)(lambda: store(tn // HEAD_DIM))
    pl.when(j == n_full)(lambda: store(n_part))

    @pl.when(j > n_full)
    def _():
        o_ref[...] = acc

    @pl.when(j == 0)
    def _():
        t_ref[...] = _dot_nt(x_ref[...], wt_ref[...])


def _proj(xb, w_t, w_tail, cos, sin, *, tm, tn, emit_weights, cast_jobs=()):
    rows = xb.shape[0]
    grid = (rows // tm, D_MAIN // tn)
    n_swa = (D_SWA_Q + 2 * D_SWA_KV) // tn

    def dest(i, j):
        return i, jnp.where(j < n_swa, j + COL_Q // tn, j - n_swa)

    in_specs = [
        pl.BlockSpec((tm, D_MODEL), lambda i, j: (i, 0)),
        pl.BlockSpec((tn, D_MODEL), lambda i, j: (j, 0)),
        pl.BlockSpec((TAIL_W, D_MODEL), lambda i, j: (0, 0)),
        pl.BlockSpec((tm, HEAD_DIM), lambda i, j: (i, 0)),
        pl.BlockSpec((tm, HEAD_DIM), lambda i, j: (i, 0)),
    ]
    args = [xb, w_t, w_tail, cos, sin]
    out_specs = [pl.BlockSpec((tm, tn), dest), pl.BlockSpec((tm, TAIL_W), lambda i, j: (i, 0))]
    out_shape = [jax.ShapeDtypeStruct((rows, D_MAIN), F32), jax.ShapeDtypeStruct((rows, TAIL_W), F32)]
    if emit_weights:
        assert grid[0] == 1
        out_specs.append(pl.BlockSpec((tn, D_MODEL), lambda i, j: (j, 0)))
        out_shape.append(jax.ShapeDtypeStruct((D_MAIN, D_MODEL), BF16))
    for w, prefix in cast_jobs:
        job_in, job_out, job_shape = _cast_job(w, prefix, grid[0] * grid[1], lambda i, j: i * grid[1] + j)
        in_specs.append(job_in)
        args.append(w)
        out_specs.append(job_out)
        out_shape.append(job_shape)
    return pl.pallas_call(
        functools.partial(_proj_body, tn=tn, emit_weights=emit_weights, n_cast=len(cast_jobs)),
        grid=grid,
        in_specs=in_specs,
        out_specs=out_specs,
        out_shape=out_shape,
        compiler_params=pltpu.CompilerParams(
            dimension_semantics=("parallel", "arbitrary"), vmem_limit_bytes=VMEM_LIMIT),
        name="proj_in",
    )(*args)


N_KEYS = N_META + 3 * CHUNK
SWA_RUN = 8
SWA_SAMPLE_SEQS = 4


def _stack_heads(ref, rows, heads):
    return jnp.concatenate([ref[rows, h * HEAD_DIM:(h + 1) * HEAD_DIM] for h in heads], axis=0)


def _group_heads(g):
    return range(g * GQA_GROUP, (g + 1) * GQA_GROUP)


def _sink_rows(sink_ref, g):
    return jnp.concatenate([jnp.full((CHUNK, 1), sink_ref[h], F32) for h in _group_heads(g)], axis=0)


def _softmax_pv(q, k, v, sink, mask):
    s = _hdot_nt(q, k) * (HEAD_DIM ** -0.5)
    if mask is not None:
        s = jnp.where(mask, s, -jnp.inf)
    m = jnp.maximum(jnp.max(s, axis=-1, keepdims=True), sink)
    p = jnp.exp(s - m)
    den = jnp.sum(p, axis=-1, keepdims=True) + jnp.exp(sink - m)
    return _hdot(p, v) / den


def _attn_sample_body(sink_ref, q_ref, km_ref, vm_ref, kw_ref, vw_ref, k0_ref, v0_ref, o_ref):
    groups = range(N_KV_SWA)
    q, k, v, sink = [], [], [], []
    for i in range(SWA_SAMPLE_SEQS):
        rows = lambda n: slice(i * n, (i + 1) * n)
        for g in groups:
            k.append(jnp.concatenate([_stack_heads(km_ref, rows(N_META), [g]), _stack_heads(kw_ref, rows(WINDOW), [g]),
                                      _stack_heads(k0_ref, rows(CHUNK), [g])], axis=0))
            v.append(jnp.concatenate([_stack_heads(vm_ref, rows(N_META), [g]), _stack_heads(vw_ref, rows(WINDOW), [g]),
                                      _stack_heads(v0_ref, rows(CHUNK), [g])], axis=0))
            q.append(_stack_heads(q_ref, rows(CHUNK), _group_heads(g)))
            sink.append(_sink_rows(sink_ref, g))
    o = _softmax_pv(jnp.stack(q), jnp.stack(k), jnp.stack(v), jnp.stack(sink), None)
    for i in range(SWA_SAMPLE_SEQS):
        for g in groups:
            for j, h in enumerate(_group_heads(g)):
                o_ref[i * CHUNK:(i + 1) * CHUNK, h * HEAD_DIM:(h + 1) * HEAD_DIM] = (
                    o[i * N_KV_SWA + g, j * CHUNK:(j + 1) * CHUNK].astype(BF16))


def _attn_run_body(sink_ref, q_ref, km_ref, vm_ref, kp_ref, vp_ref, kc_ref, vc_ref, o_ref):
    has_prev = pl.program_id(0) % (CHUNKS_PER_SEQ // SWA_RUN) >= 1
    col = lax.broadcasted_iota(jnp.int32, (1, N_KEYS), 1)
    groups = range(N_KV_SWA)
    q, k, v, sink, mask = [], [], [], [], []
    for sub in range(SWA_RUN):
        n_prev = max(2 - sub, 0)
        prev_rows = slice((2 - n_prev) * CHUNK, 2 * CHUNK)
        cur_rows = slice(max(sub - 2, 0) * CHUNK, (sub + 1) * CHUNK)
        visible = jnp.logical_or(jnp.logical_or(col < N_META, col >= N_META + n_prev * CHUNK), has_prev)
        for g in groups:
            k_parts = [_stack_heads(km_ref, slice(None), [g]), _stack_heads(kc_ref, cur_rows, [g])]
            v_parts = [_stack_heads(vm_ref, slice(None), [g]), _stack_heads(vc_ref, cur_rows, [g])]
            if n_prev:
                k_parts.insert(1, _stack_heads(kp_ref, prev_rows, [g]))
                v_parts.insert(1, _stack_heads(vp_ref, prev_rows, [g]))
            k.append(jnp.concatenate(k_parts, axis=0))
            v.append(jnp.concatenate(v_parts, axis=0))
            q.append(_stack_heads(q_ref, slice(sub * CHUNK, (sub + 1) * CHUNK), _group_heads(g)))
            sink.append(_sink_rows(sink_ref, g))
            mask.append(visible)
    o = _softmax_pv(jnp.stack(q), jnp.stack(k), jnp.stack(v), jnp.stack(sink), jnp.stack(mask))
    for sub in range(SWA_RUN):
        for g in groups:
            for j, h in enumerate(_group_heads(g)):
                o_ref[sub * CHUNK:(sub + 1) * CHUNK, h * HEAD_DIM:(h + 1) * HEAD_DIM] = (
                    o[sub * N_KV_SWA + g, j * CHUNK:(j + 1) * CHUNK].astype(BF16))


def _attn_prompt(sinks, h_p, h_m):
    kcol = COL_K // D_SWA_KV
    vcol = COL_V // D_SWA_KV
    qcol = COL_Q // D_SWA_Q
    runs_per_seq = CHUNKS_PER_SEQ // SWA_RUN
    pairs_per_run = SWA_RUN // 2
    prev = lambda colblk: (lambda p: (p * pairs_per_run - jnp.where(p % runs_per_seq >= 1, 1, 0), colblk))
    kv_prev = lambda colblk: pl.BlockSpec((2 * CHUNK, D_SWA_KV), prev(colblk))
    kv_run = lambda colblk: pl.BlockSpec((SWA_RUN * CHUNK, D_SWA_KV), lambda p: (p, colblk))
    meta = lambda colblk: pl.BlockSpec((N_META, D_SWA_KV), lambda p: (META_ROW0 // N_META, colblk))
    return pl.pallas_call(
        _attn_run_body,
        grid=(N_PROMPT_ROWS // (SWA_RUN * CHUNK),),
        in_specs=[
            pl.BlockSpec(memory_space=pltpu.SMEM),
            pl.BlockSpec((SWA_RUN * CHUNK, D_SWA_Q), lambda p: (p, qcol)),
            meta(kcol), meta(vcol),
            kv_prev(kcol), kv_prev(vcol),
            kv_run(kcol), kv_run(vcol),
        ],
        out_specs=pl.BlockSpec((SWA_RUN * CHUNK, D_SWA_Q), lambda p: (p, 0)),
        out_shape=jax.ShapeDtypeStruct((N_PROMPT_ROWS, D_SWA_Q), BF16),
        compiler_params=pltpu.CompilerParams(dimension_semantics=("parallel",)),
        name="swa_prompt",
    )(sinks, h_p, h_m, h_m, h_p, h_p, h_p, h_p)


def _attn_sample(sinks, h_s, meta_k, meta_v, win_k, win_v):
    kcol = COL_K // D_SWA_KV
    vcol = COL_V // D_SWA_KV
    qcol = COL_Q // D_SWA_Q
    n = SWA_SAMPLE_SEQS
    cached = lambda rows_per_seq: pl.BlockSpec((n * rows_per_seq, D_SWA_KV), lambda s: (s, 0))
    own = lambda colblk: pl.BlockSpec((n * CHUNK, D_SWA_KV), lambda s: (s, colblk))
    return pl.pallas_call(
        _attn_sample_body,
        grid=(DEC_BATCH // n,),
        in_specs=[
            pl.BlockSpec(memory_space=pltpu.SMEM),
            pl.BlockSpec((n * CHUNK, D_SWA_Q), lambda s: (s, qcol)),
            cached(N_META), cached(N_META),
            cached(WINDOW), cached(WINDOW),
            own(kcol), own(vcol),
        ],
        out_specs=pl.BlockSpec((n * CHUNK, D_SWA_Q), lambda s: (s, 0)),
        out_shape=jax.ShapeDtypeStruct((N_SAMPLE_ROWS, D_SWA_Q), BF16),
        compiler_params=pltpu.CompilerParams(dimension_semantics=("parallel",)),
        name="swa_sample",
    )(sinks, h_s, meta_k, meta_v, win_k, win_v, h_s, h_s)


def _unit_lower_inverse(a, n):
    r = lax.broadcasted_iota(jnp.int32, (n, n), 0)
    c = lax.broadcasted_iota(jnp.int32, (n, n), 1)
    eye = (r == c).astype(F32)
    base = 3
    a0 = jnp.where((r >> base) == (c >> base), a, 0.0)
    a2 = _hdot(a0, a0)
    a4 = _hdot(a2, a2)
    x = _hdot(_hdot(eye - a0, eye + a2), eye + a4)
    shift = base
    while (1 << shift) < n:
        pair = jnp.logical_and((r >> (shift + 1)) == (c >> (shift + 1)), (r >> shift) != (c >> shift))
        ak = jnp.where(pair, a, 0.0)
        x = x - _hdot(x, _hdot(ak, x))
        shift += 1
    return x


def _gdn_body(x_ref, z_ref, t_ref, w_ref, alog_ref, dt_ref, nw_ref, s0_ref, b0_ref, *rest, C, n_chunks, n_cast):
    cast_src, (o_ref, sout_ref) = rest[:n_cast], rest[n_cast:n_cast + 2]
    cast_dst, (s_scr, xe_scr) = rest[n_cast + 2:2 * n_cast + 2], rest[2 * n_cast + 2:]
    c = pl.program_id(1)

    for src, dst in zip(cast_src, cast_dst):
        dst[...] = src[...].astype(BF16)

    @pl.when(c == 0)
    def _():
        s_scr[...] = s0_ref[0]
        xe_scr[0:CARRY, :] = b0_ref[0]

    @pl.when(c > 0)
    def _():
        xe_scr[0:CARRY, :] = xe_scr[C:C + CARRY, :]

    xe_scr[CARRY:CARRY + C, :] = x_ref[...]

    t = t_ref[...]
    beta = jax.nn.sigmoid(t)
    ta = t + dt_ref[...]
    softplus = jnp.maximum(ta, 0.0) + jnp.log(1.0 + jnp.exp(-jnp.abs(ta)))
    g = -jnp.exp(alog_ref[...]) * softplus
    r = lax.broadcasted_iota(jnp.int32, (C, C), 0)
    cc = lax.broadcasted_iota(jnp.int32, (C, C), 1)
    incl = r >= cc
    strict = r > cc
    gc = _dot(incl.astype(F32), g, HI)
    gc_t = gc.T
    heads = range(N_HEADS_GDN)
    bh = jnp.stack([beta[:, h:h + 1] for h in heads])
    gcol = jnp.stack([gc[:, N_HEADS_GDN + h:N_HEADS_GDN + h + 1] for h in heads])
    grow = jnp.stack([gc_t[N_HEADS_GDN + h:N_HEADS_GDN + h + 1, :] for h in heads])
    glast = gcol[:, C - 1:C, :]
    decay = jnp.where(incl, jnp.exp(jnp.where(incl, gcol - grow, 0.0)), 0.0)
    e_g = jnp.exp(gcol)

    xe = xe_scr[...]
    acc = w_ref[CONV_WIDTH - 1:CONV_WIDTH, :] * xe[CARRY:, :]
    for lag in range(1, CONV_WIDTH):
        tap = CONV_WIDTH - 1 - lag
        acc = acc + w_ref[tap:tap + 1, :] * pltpu.roll(xe, lag, 0)[CARRY:, :]
    y = _silu(acc)
    split = lambda base: jnp.stack([y[:, base + h * DK_GDN:base + (h + 1) * DK_GDN] for h in heads])
    q, k, v = split(0), split(D_GDN), split(2 * D_GDN)
    q = q * lax.rsqrt(jnp.sum(q * q, axis=-1, keepdims=True) + RMS_EPS) * (DK_GDN ** -0.5)
    k = k * lax.rsqrt(jnp.sum(k * k, axis=-1, keepdims=True) + RMS_EPS)
    kb = k * bh

    kq = _hdot_nt(jnp.concatenate([kb, q], axis=1), k)
    a = jnp.where(strict, kq[:, :C] * decay, 0.0)
    qk = kq[:, C:] * decay
    t_inv = _unit_lower_inverse(a, C)

    s_prev = s_scr[...]
    ws = _hdot(jnp.concatenate([kb * e_g, q * e_g], axis=1), s_prev)
    v_new = _hdot(t_inv, v * bh - ws[:, :C])
    o = ws[:, C:] + _hdot(qk, v_new)
    s_scr[...] = s_prev * jnp.exp(glast) + _hdot_tn(k * jnp.exp(glast - gcol), v_new)

    o = o * lax.rsqrt(jnp.mean(o * o, axis=-1, keepdims=True) + RMS_EPS) * nw_ref[...]
    for h in heads:
        hs = slice(h * DV_GDN, (h + 1) * DV_GDN)
        o_ref[:, hs] = (o[h] * _silu(z_ref[:, hs])).astype(BF16)

    @pl.when(c == n_chunks - 1)
    def _():
        sout_ref[0] = s_scr[...]


def _gdn(h_src, tail_src, conv_w, alog_row, dt_row, nw_row, s0, buf0, *, n_seq, n_chunks, C, shared_init,
         first_chunk=0, cast_jobs=()):
    init = (lambda s: 0) if shared_init else (lambda s: s)
    vec = pl.BlockSpec((1, TAIL_W), lambda s, c: (0, 0))
    src = lambda s, c: first_chunk + s * n_chunks + c
    in_specs = [
        pl.BlockSpec((C, D_CONV), lambda s, c: (src(s, c), COL_CONV // D_CONV)),
        pl.BlockSpec((C, D_GDN), lambda s, c: (src(s, c), COL_Z // D_GDN)),
        pl.BlockSpec((C, TAIL_W), lambda s, c: (src(s, c), 0)),
        pl.BlockSpec((CONV_WIDTH, D_CONV), lambda s, c: (0, 0)),
        vec, vec, vec,
        pl.BlockSpec((1, N_HEADS_GDN, DK_GDN, DV_GDN), lambda s, c: (init(s), 0, 0, 0)),
        pl.BlockSpec((1, CARRY, D_CONV), lambda s, c: (init(s), 0, 0)),
    ]
    out_specs = [
        pl.BlockSpec((C, D_GDN), lambda s, c: (s * n_chunks + c, 0)),
        pl.BlockSpec((1, N_HEADS_GDN, DK_GDN, DV_GDN), lambda s, c: (s, 0, 0, 0)),
    ]
    out_shape = [
        jax.ShapeDtypeStruct((n_seq * n_chunks * C, D_GDN), BF16),
        jax.ShapeDtypeStruct((n_seq, N_HEADS_GDN, DK_GDN, DV_GDN), F32),
    ]
    for w, prefix in cast_jobs:
        job_in, job_out, job_shape = _cast_job(w, prefix, n_seq * n_chunks, lambda s, c: s * n_chunks + c)
        in_specs.append(job_in)
        out_specs.append(job_out)
        out_shape.append(job_shape)
    return pl.pallas_call(
        functools.partial(_gdn_body, C=C, n_chunks=n_chunks, n_cast=len(cast_jobs)),
        grid=(n_seq, n_chunks),
        in_specs=in_specs,
        out_specs=out_specs,
        out_shape=out_shape,
        scratch_shapes=[
            pltpu.VMEM((N_HEADS_GDN, DK_GDN, DV_GDN), F32),
            pltpu.VMEM((CARRY + C, D_CONV), F32),
        ],
        compiler_params=pltpu.CompilerParams(
            dimension_semantics=("parallel", "arbitrary"), vmem_limit_bytes=VMEM_LIMIT),
        name="gdn",
    )(h_src, h_src, tail_src, conv_w, alog_row, dt_row, nw_row, s0, buf0, *[w for w, _ in cast_jobs])


def _out_body(a_ref, b_ref, w_ref, x_ref, g_ref, beta_ref, o_ref):
    o_ref[...] = ALPHA * x_ref[...] + _dot(a_ref[...], w_ref[0:D_SWA_Q, :]) + _dot(b_ref[...], w_ref[D_SWA_Q:, :])
    _layernorm_into(o_ref, lambda rows: o_ref[rows, :], g_ref, beta_ref)


def _out_proj(o_swa, o_gdn, w_out, x1, g, b, *, tm):
    rows = o_swa.shape[0]
    return pl.pallas_call(
        _out_body,
        grid=(rows // tm,),
        in_specs=[
            pl.BlockSpec((tm, D_SWA_Q), lambda i: (i, 0)),
            pl.BlockSpec((tm, D_GDN), lambda i: (i, 0)),
            pl.BlockSpec((D_MODEL, D_MODEL), lambda i: (0, 0), pipeline_mode=pl.Buffered(1)),
            pl.BlockSpec((tm, D_MODEL), lambda i: (i, 0)),
            pl.BlockSpec((1, D_MODEL), lambda i: (0, 0)),
            pl.BlockSpec((1, D_MODEL), lambda i: (0, 0)),
        ],
        out_specs=pl.BlockSpec((tm, D_MODEL), lambda i: (i, 0)),
        out_shape=jax.ShapeDtypeStruct((rows, D_MODEL), F32),
        compiler_params=pltpu.CompilerParams(
            dimension_semantics=("parallel",), vmem_limit_bytes=VMEM_LIMIT),
        name="proj_out",
    )(o_swa, o_gdn, w_out, x1, g, b)


def _rope_tables(pos):
    half = HEAD_DIM // 2
    inv = ROPE_THETA ** (-jnp.arange(half, dtype=F32) / half)
    ang = pos.astype(F32)[:, None] * inv[None, :]
    cos = jnp.cos(ang)
    sin = jnp.sin(ang)
    return jnp.concatenate([cos, cos], axis=1), jnp.concatenate([-sin, sin], axis=1)


def kernel(x_prompt, x_sample, cache_meta_k, cache_meta_v, cache_win_k, cache_win_v, state_conv, state_gdn,
           meta_tokens, ln_g, ln_b, ffn_w_gate, ffn_w_up, ffn_w_down, w_in, w_out, attn_sinks, conv_w,
           gdn_a_log, gdn_dt_bias, gdn_norm_w):
    l = 0
    wg1, wu1, wd1 = (w[l, 0].astype(BF16) for w in (ffn_w_gate, ffn_w_up, ffn_w_down))
    w_main = jnp.swapaxes(w_in[l], 0, 1)
    w_tail = jnp.pad(w_main[D_MAIN:], ((0, TAIL_W - 2 * N_HEADS_GDN), (0, 0))).astype(BF16)
    g1, g2, g3 = (ln_g[l, i][None, :] for i in range(3))
    b1, b2, b3 = (ln_b[l, i][None, :] for i in range(3))
    pad_tail = lambda v, off: jnp.pad(v.astype(F32), (off, TAIL_W - off - N_HEADS_GDN))[None, :]
    alog_row = pad_tail(gdn_a_log[l], N_HEADS_GDN)
    dt_row = pad_tail(gdn_dt_bias[l], N_HEADS_GDN)
    nw_row = gdn_norm_w[l].astype(F32)[None, :]
    cw = conv_w[l]
    sinks = attn_sinks[l].astype(F32)

    n_pad = N_SM_ROWS - N_SAMPLE_ROWS - N_META
    xp = x_prompt.reshape(N_PROMPT_ROWS, D_MODEL)
    xs = jnp.concatenate([x_sample.reshape(N_SAMPLE_ROWS, D_MODEL), meta_tokens.astype(F32),
                          jnp.zeros((n_pad, D_MODEL), F32)], axis=0)
    cos_p, sin_p = _rope_tables(jnp.tile(N_META + jnp.arange(SEQ, dtype=jnp.int32), BATCH))
    cos_s, sin_s = _rope_tables(jnp.concatenate([
        jnp.tile(N_META + PAST_LEN + jnp.arange(DEC_SEQ, dtype=jnp.int32), DEC_BATCH),
        jnp.arange(N_META, dtype=jnp.int32), jnp.zeros((n_pad,), jnp.int32)]))

    ffn1 = functools.partial(_ffn, wg=wg1, wu=wu1, wd=wd1, g=g1, b=b1, emit_bf16=True)
    x1_s, x1b_s = ffn1(xs, tm=SM_TILE)
    x1_p, x1b_p = ffn1(xp, tm=512)
    h_s, t_s, w_main_b = _proj(x1b_s, w_main, w_tail, cos_s, sin_s, tm=N_SM_ROWS, tn=PROJ_TN_F32,
                               emit_weights=True)
    h_p, t_p, wo = _proj(x1b_p, w_main_b, w_tail, cos_p, sin_p, tm=1024, tn=PROJ_TN_BF16, emit_weights=False,
                         cast_jobs=((w_out, (l,)),))
    meta_rows = slice(META_ROW0, META_ROW0 + N_META)

    o_swa_p = _attn_prompt(sinks, h_p, h_s)
    o_swa_s = _attn_sample(
        sinks, h_s,
        cache_meta_k[l].reshape(DEC_BATCH * N_META, D_SWA_KV), cache_meta_v[l].reshape(DEC_BATCH * N_META, D_SWA_KV),
        cache_win_k[l].reshape(DEC_BATCH * WINDOW, D_SWA_KV), cache_win_v[l].reshape(DEC_BATCH * WINDOW, D_SWA_KV))

    zero_s = jnp.zeros((1, N_HEADS_GDN, DK_GDN, DV_GDN), F32)
    zero_buf = jnp.zeros((1, CARRY, D_CONV), F32)
    gdn = functools.partial(_gdn, conv_w=cw, alog_row=alog_row, dt_row=dt_row, nw_row=nw_row)
    _, s_meta = gdn(h_s, t_s, s0=zero_s, buf0=zero_buf, n_seq=1, n_chunks=1, C=N_META, shared_init=True,
                    first_chunk=META_ROW0 // N_META)[:2]
    buf_meta = h_s[None, META_ROW0 + N_META - CARRY:META_ROW0 + N_META, COL_CONV:COL_CONV + D_CONV]
    second = (l, 1)
    o_gdn_p, s_prompt, wg2, wu2, wd2 = gdn(
        h_p, t_p, s0=s_meta, buf0=buf_meta, n_seq=BATCH, n_chunks=CHUNKS_PER_SEQ, C=CHUNK, shared_init=True,
        cast_jobs=((ffn_w_gate, second), (ffn_w_up, second), (ffn_w_down, second)))
    buf_s = jnp.pad(state_conv[l].astype(F32), ((0, 0), (CARRY - (CONV_WIDTH - 1), 0), (0, 0)))
    o_gdn_s, s_sample = gdn(h_s, t_s, s0=state_gdn[l].astype(F32), buf0=buf_s, n_seq=DEC_BATCH, n_chunks=1,
                            C=CHUNK, shared_init=False)[:2]

    x2_p = _out_proj(o_swa_p, o_gdn_p, wo, x1_p, g2, b2, tm=256)
    x2_s = _out_proj(o_swa_s, o_gdn_s, wo, x1_s, g2, b2, tm=256)
    ffn2 = functools.partial(_ffn, wg=wg2, wu=wu2, wd=wd2, g=g3, b=b3, emit_bf16=False)
    (y_p,) = ffn2(x2_p, tm=512)
    (y_s,) = ffn2(x2_s, tm=512)

    y_prompt = y_p.reshape(BATCH, SEQ, D_MODEL)
    y_sample = y_s.reshape(DEC_BATCH, DEC_SEQ, D_MODEL)
    k_meta = h_s[meta_rows, COL_K:COL_K + D_SWA_KV].reshape(N_META, N_KV_SWA, HEAD_DIM)
    v_meta = h_s[meta_rows, COL_V:COL_V + D_SWA_KV].reshape(N_META, N_KV_SWA, HEAD_DIM)
    p_meta_k = jnp.broadcast_to(k_meta[None, None], (1, BATCH, N_META, N_KV_SWA, HEAD_DIM))
    p_meta_v = jnp.broadcast_to(v_meta[None, None], (1, BATCH, N_META, N_KV_SWA, HEAD_DIM))
    hp = h_p.reshape(BATCH, SEQ, D_MAIN)
    sample_cols = lambda c0, width: h_s[:N_SAMPLE_ROWS, c0:c0 + width].reshape(DEC_BATCH, DEC_SEQ, width)
    p_win_k = hp[:, SEQ - WINDOW:, COL_K:COL_K + D_SWA_KV].reshape(1, BATCH, WINDOW, N_KV_SWA, HEAD_DIM)
    p_win_v = hp[:, SEQ - WINDOW:, COL_V:COL_V + D_SWA_KV].reshape(1, BATCH, WINDOW, N_KV_SWA, HEAD_DIM)
    p_conv = hp[:, SEQ - (CONV_WIDTH - 1):, COL_CONV:COL_CONV + D_CONV][None]
    p_gdn = s_prompt[None]
    s_win_k = sample_cols(COL_K, D_SWA_KV).reshape(1, DEC_BATCH, DEC_SEQ, N_KV_SWA, HEAD_DIM)
    s_win_v = sample_cols(COL_V, D_SWA_KV).reshape(1, DEC_BATCH, DEC_SEQ, N_KV_SWA, HEAD_DIM)
    s_conv = sample_cols(COL_CONV, D_CONV)[:, DEC_SEQ - (CONV_WIDTH - 1):][None]
    s_gdn = s_sample[None]
    return (y_prompt, y_sample, p_meta_k, p_meta_v, p_win_k, p_win_v, p_conv, p_gdn, s_win_k, s_win_v, s_conv,
            s_gdn)
```

```python
import functools

import jax
import jax.numpy as jnp
from jax import lax
from jax.experimental import pallas as pl
from jax.experimental.pallas import tpu as pltpu

D_MODEL = 4096
BATCH = 4
SEQ = 2048
DEC_BATCH = 16
DEC_SEQ = 64
PAST_LEN = 2048
CHUNK = 64
N_META = 16
WINDOW = 128
HEAD_DIM = 128
N_HEADS_SWA = 16
N_KV_SWA = 4
GQA_GROUP = 4
N_HEADS_GDN = 16
DK_GDN = 128
DV_GDN = 128
CONV_WIDTH = 4
D_SWA_Q = N_HEADS_SWA * HEAD_DIM
D_SWA_KV = N_KV_SWA * HEAD_DIM
D_GDN = N_HEADS_GDN * DK_GDN
D_CONV = 3 * D_GDN
D_FF = 11008
ROPE_THETA = 10000.0
LN_EPS = 1e-5
RMS_EPS = 1e-6
ALPHA = 2.0 ** 0.25

N_PROMPT_ROWS = BATCH * SEQ
N_SAMPLE_ROWS = DEC_BATCH * DEC_SEQ
CHUNKS_PER_SEQ = SEQ // CHUNK
META_ROW0 = N_SAMPLE_ROWS
SM_TILE = 528
N_SM_ROWS = 2 * SM_TILE

COL_CONV = 0
COL_Z = D_CONV
COL_Q = COL_Z + D_GDN
COL_K = COL_Q + D_SWA_Q
COL_V = COL_K + D_SWA_KV
D_MAIN = COL_V + D_SWA_KV
TAIL_W = 128

FFN_TF = 256
LN_ROWS_CHOICES = (128, 48, 16, 8)
LN_UNROLL = 2
PROJ_TN_F32 = 512
PROJ_TN_BF16 = 1024
CAST_STRIP_ROWS = (16, 32, 64, 128, 256)
CARRY = 8

V7X_VMEM_BYTES = 64 * 1024 * 1024
VMEM_LIMIT = V7X_VMEM_BYTES - 4 * 1024 * 1024

BF16 = jnp.bfloat16
F32 = jnp.float32
HI = lax.Precision.HIGHEST


def _dot(a, b, precision=None):
    return jnp.dot(a, b, preferred_element_type=F32, precision=precision)


def _dot_nt(a, b):
    return lax.dot_general(a, b, (((1,), (1,)), ((), ())), preferred_element_type=F32)


def _head_dot(a, b, lhs_contract, rhs_contract):
    dims = (((lhs_contract,), (rhs_contract,)), ((0,), (0,)))
    return lax.dot_general(a.astype(BF16), b.astype(BF16), dims, preferred_element_type=F32)


_hdot = functools.partial(_head_dot, lhs_contract=2, rhs_contract=1)
_hdot_nt = functools.partial(_head_dot, lhs_contract=2, rhs_contract=2)
_hdot_tn = functools.partial(_head_dot, lhs_contract=1, rhs_contract=1)


def _layernorm_rows(y, g, b, eps=LN_EPS):
    mu = jnp.mean(y, axis=-1, keepdims=True)
    d = y - mu
    var = jnp.mean(d * d, axis=-1, keepdims=True)
    return d * lax.rsqrt(var + eps) * g + b


def _layernorm_into(o_ref, pre_norm_rows, g_ref, b_ref, eps=LN_EPS):
    n_rows = o_ref.shape[0]
    per_chunk = next(r for r in LN_ROWS_CHOICES if n_rows % r == 0)

    def chunk(c, carry):
        rows = pl.ds(pl.multiple_of(c * per_chunk, per_chunk), per_chunk)
        o_ref[rows, :] = _layernorm_rows(pre_norm_rows(rows), g_ref[...], b_ref[...], eps)
        return carry

    lax.fori_loop(0, n_rows // per_chunk, chunk, 0, unroll=LN_UNROLL)


def _silu(x):
    return x * jax.nn.sigmoid(x)


def _cast_job(w, prefix, n_steps, step):
    r, c = w.shape[len(prefix):]
    strip = next(s for s in CAST_STRIP_ROWS if r % s == 0 and r // s <= n_steps)
    last = r // strip - 1
    strip_of = lambda *ids: jnp.minimum(step(*ids), last)
    in_spec = pl.BlockSpec((None,) * len(prefix) + (strip, c), lambda *ids: prefix + (strip_of(*ids), 0))
    out_spec = pl.BlockSpec((strip, c), lambda *ids: (strip_of(*ids), 0))
    return in_spec, out_spec, jax.ShapeDtypeStruct((r, c), BF16)


def _ffn_body(x_ref, wg_ref, wu_ref, wd_ref, g_ref, b_ref, o_ref, *rest, emit_bf16):
    if emit_bf16:
        ob_ref, xb_ref = rest
    else:
        (xb_ref,) = rest
    f = pl.program_id(1)

    @pl.when(f == 0)
    def _():
        xb_ref[...] = x_ref[...].astype(BF16)
        o_ref[...] = jnp.zeros_like(o_ref)

    xb = xb_ref[...]
    hidden = _silu(_dot(xb, wg_ref[...])) * _dot(xb, wu_ref[...])
    o_ref[...] += _dot(hidden.astype(BF16), wd_ref[...])

    @pl.when(f == pl.num_programs(1) - 1)
    def _():
        _layernorm_into(o_ref, lambda rows: x_ref[rows, :] + (0.5 / ALPHA) * o_ref[rows, :], g_ref, b_ref,
                        eps=LN_EPS / ALPHA ** 2)
        if emit_bf16:
            ob_ref[...] = o_ref[...].astype(BF16)


def _ffn(x, wg, wu, wd, g, b, *, tm, emit_bf16):
    rows = x.shape[0]
    row_spec = lambda: pl.BlockSpec((tm, D_MODEL), lambda i, f: (i, 0))
    out_shape = [jax.ShapeDtypeStruct((rows, D_MODEL), F32)]
    out_specs = [row_spec()]
    if emit_bf16:
        out_shape.append(jax.ShapeDtypeStruct((rows, D_MODEL), BF16))
        out_specs.append(row_spec())
    return pl.pallas_call(
        functools.partial(_ffn_body, emit_bf16=emit_bf16),
        grid=(rows // tm, D_FF // FFN_TF),
        in_specs=[
            row_spec(),
            pl.BlockSpec((D_MODEL, FFN_TF), lambda i, f: (0, f)),
            pl.BlockSpec((D_MODEL, FFN_TF), lambda i, f: (0, f)),
            pl.BlockSpec((FFN_TF, D_MODEL), lambda i, f: (f, 0)),
            pl.BlockSpec((1, D_MODEL), lambda i, f: (0, 0)),
            pl.BlockSpec((1, D_MODEL), lambda i, f: (0, 0)),
        ],
        out_specs=out_specs,
        out_shape=out_shape,
        scratch_shapes=[pltpu.VMEM((tm, D_MODEL), BF16)],
        compiler_params=pltpu.CompilerParams(
            dimension_semantics=("parallel", "arbitrary"), vmem_limit_bytes=VMEM_LIMIT),
        name="ffn",
    )(x, wg, wu, wd, g, b)


def _proj_body(x_ref, w_ref, wt_ref, cos_ref, sin_ref, *rest, tn, emit_weights, n_cast):
    cast_src, rest = rest[:n_cast], rest[n_cast:]
    o_ref, t_ref = rest[:2]
    j = pl.program_id(1)
    for src, dst in zip(cast_src, rest[len(rest) - n_cast:]):
        dst[...] = src[...].astype(BF16)
    wb = w_ref[...].astype(BF16)
    if emit_weights:
        rest[2][...] = wb
    acc = _dot_nt(x_ref[...], wb)
    n_full, n_part = divmod((D_SWA_Q + D_SWA_KV) // HEAD_DIM, tn // HEAD_DIM)

    def store(n_rope_heads):
        cos = cos_ref[...]
        sin = sin_ref[...]
        for s in range(tn // HEAD_DIM):
            cols = slice(s * HEAD_DIM, (s + 1) * HEAD_DIM)
            blk = acc[:, cols]
            o_ref[:, cols] = blk * cos + pltpu.roll(blk, HEAD_DIM // 2, 1) * sin if s < n_rope_heads else blk

    pl.when(j < n_full)(lambda: store(tn // HEAD_DIM))
    pl.when(j == n_full)(lambda: store(n_part))

    @pl.when(j > n_full)
    def _():
        o_ref[...] = acc

    @pl.when(j == 0)
    def _():
        t_ref[...] = _dot_nt(x_ref[...], wt_ref[...])


def _proj(xb, w_t, w_tail, cos, sin, *, tm, tn, emit_weights, cast_jobs=()):
    rows = xb.shape[0]
    grid = (rows // tm, D_MAIN // tn)
    rope_period = cos.shape[0] // tm
    n_swa = (D_SWA_Q + 2 * D_SWA_KV) // tn

    def dest(i, j):
        return i, jnp.where(j < n_swa, j + COL_Q // tn, j - n_swa)

    in_specs = [
        pl.BlockSpec((tm, D_MODEL), lambda i, j: (i, 0)),
        pl.BlockSpec((tn, D_MODEL), lambda i, j: (j, 0)),
        pl.BlockSpec((TAIL_W, D_MODEL), lambda i, j: (0, 0)),
        pl.BlockSpec((tm, HEAD_DIM), lambda i, j: (i % rope_period, 0)),
        pl.BlockSpec((tm, HEAD_DIM), lambda i, j: (i % rope_period, 0)),
    ]
    args = [xb, w_t, w_tail, cos, sin]
    out_specs = [pl.BlockSpec((tm, tn), dest), pl.BlockSpec((tm, TAIL_W), lambda i, j: (i, 0))]
    out_shape = [jax.ShapeDtypeStruct((rows, D_MAIN), F32), jax.ShapeDtypeStruct((rows, TAIL_W), F32)]
    if emit_weights:
        assert grid[0] == 1
        out_specs.append(pl.BlockSpec((tn, D_MODEL), lambda i, j: (j, 0)))
        out_shape.append(jax.ShapeDtypeStruct((D_MAIN, D_MODEL), BF16))
    for w, prefix in cast_jobs:
        job_in, job_out, job_shape = _cast_job(w, prefix, grid[0] * grid[1], lambda i, j: i * grid[1] + j)
        in_specs.append(job_in)
        args.append(w)
        out_specs.append(job_out)
        out_shape.append(job_shape)
    return pl.pallas_call(
        functools.partial(_proj_body, tn=tn, emit_weights=emit_weights, n_cast=len(cast_jobs)),
        grid=grid,
        in_specs=in_specs,
        out_specs=out_specs,
        out_shape=out_shape,
        compiler_params=pltpu.CompilerParams(
            dimension_semantics=("parallel", "arbitrary"), vmem_limit_bytes=VMEM_LIMIT),
        name="proj_in",
    )(*args)


N_KEYS = N_META + 3 * CHUNK
SWA_RUN = 8
SWA_SAMPLE_SEQS = 4


def _stack_heads(ref, rows, heads):
    return jnp.concatenate([ref[rows, h * HEAD_DIM:(h + 1) * HEAD_DIM] for h in heads], axis=0)


def _group_heads(g):
    return range(g * GQA_GROUP, (g + 1) * GQA_GROUP)


def _sink_rows(sink_ref, g):
    return jnp.concatenate([jnp.full((CHUNK, 1), sink_ref[h], F32) for h in _group_heads(g)], axis=0)


def _softmax_pv(q, k, v, sink, mask):
    s = _hdot_nt(q, k) * (HEAD_DIM ** -0.5)
    if mask is not None:
        s = jnp.where(mask, s, -jnp.inf)
    m = jnp.maximum(jnp.max(s, axis=-1, keepdims=True), sink)
    p = jnp.exp(s - m)
    den = jnp.sum(p, axis=-1, keepdims=True) + jnp.exp(sink - m)
    return _hdot(p, v) / den


def _attn_sample_body(sink_ref, q_ref, km_ref, vm_ref, kw_ref, vw_ref, k0_ref, v0_ref, o_ref):
    groups = range(N_KV_SWA)
    q, k, v, sink = [], [], [], []
    for i in range(SWA_SAMPLE_SEQS):
        rows = lambda n: slice(i * n, (i + 1) * n)
        for g in groups:
            k.append(jnp.concatenate([_stack_heads(km_ref, rows(N_META), [g]), _stack_heads(kw_ref, rows(WINDOW), [g]),
                                      _stack_heads(k0_ref, rows(CHUNK), [g])], axis=0))
            v.append(jnp.concatenate([_stack_heads(vm_ref, rows(N_META), [g]), _stack_heads(vw_ref, rows(WINDOW), [g]),
                                      _stack_heads(v0_ref, rows(CHUNK), [g])], axis=0))
            q.append(_stack_heads(q_ref, rows(CHUNK), _group_heads(g)))
            sink.append(_sink_rows(sink_ref, g))
    o = _softmax_pv(jnp.stack(q), jnp.stack(k), jnp.stack(v), jnp.stack(sink), None)
    for i in range(SWA_SAMPLE_SEQS):
        for g in groups:
            for j, h in enumerate(_group_heads(g)):
                o_ref[i * CHUNK:(i + 1) * CHUNK, h * HEAD_DIM:(h + 1) * HEAD_DIM] = (
                    o[i * N_KV_SWA + g, j * CHUNK:(j + 1) * CHUNK].astype(BF16))


def _attn_run_body(sink_ref, q_ref, km_ref, vm_ref, kp_ref, vp_ref, kc_ref, vc_ref, o_ref):
    has_prev = pl.program_id(0) % (CHUNKS_PER_SEQ // SWA_RUN) >= 1
    col = lax.broadcasted_iota(jnp.int32, (1, N_KEYS), 1)
    groups = range(N_KV_SWA)
    q, k, v, sink, mask = [], [], [], [], []
    for sub in range(SWA_RUN):
        n_prev = max(2 - sub, 0)
        prev_rows = slice((2 - n_prev) * CHUNK, 2 * CHUNK)
        cur_rows = slice(max(sub - 2, 0) * CHUNK, (sub + 1) * CHUNK)
        visible = jnp.logical_or(jnp.logical_or(col < N_META, col >= N_META + n_prev * CHUNK), has_prev)
        for g in groups:
            k_parts = [_stack_heads(km_ref, slice(None), [g]), _stack_heads(kc_ref, cur_rows, [g])]
            v_parts = [_stack_heads(vm_ref, slice(None), [g]), _stack_heads(vc_ref, cur_rows, [g])]
            if n_prev:
                k_parts.insert(1, _stack_heads(kp_ref, prev_rows, [g]))
                v_parts.insert(1, _stack_heads(vp_ref, prev_rows, [g]))
            k.append(jnp.concatenate(k_parts, axis=0))
            v.append(jnp.concatenate(v_parts, axis=0))
            q.append(_stack_heads(q_ref, slice(sub * CHUNK, (sub + 1) * CHUNK), _group_heads(g)))
            sink.append(_sink_rows(sink_ref, g))
            mask.append(visible)
    o = _softmax_pv(jnp.stack(q), jnp.stack(k), jnp.stack(v), jnp.stack(sink), jnp.stack(mask))
    for sub in range(SWA_RUN):
        for g in groups:
            for j, h in enumerate(_group_heads(g)):
                o_ref[sub * CHUNK:(sub + 1) * CHUNK, h * HEAD_DIM:(h + 1) * HEAD_DIM] = (
                    o[sub * N_KV_SWA + g, j * CHUNK:(j + 1) * CHUNK].astype(BF16))


def _attn_prompt(sinks, h_p, h_m):
    kcol = COL_K // D_SWA_KV
    vcol = COL_V // D_SWA_KV
    qcol = COL_Q // D_SWA_Q
    runs_per_seq = CHUNKS_PER_SEQ // SWA_RUN
    pairs_per_run = SWA_RUN // 2
    prev = lambda colblk: (lambda p: (p * pairs_per_run - jnp.where(p % runs_per_seq >= 1, 1, 0), colblk))
    kv_prev = lambda colblk: pl.BlockSpec((2 * CHUNK, D_SWA_KV), prev(colblk))
    kv_run = lambda colblk: pl.BlockSpec((SWA_RUN * CHUNK, D_SWA_KV), lambda p: (p, colblk))
    meta = lambda colblk: pl.BlockSpec((N_META, D_SWA_KV), lambda p: (META_ROW0 // N_META, colblk))
    return pl.pallas_call(
        _attn_run_body,
        grid=(N_PROMPT_ROWS // (SWA_RUN * CHUNK),),
        in_specs=[
            pl.BlockSpec(memory_space=pltpu.SMEM),
            pl.BlockSpec((SWA_RUN * CHUNK, D_SWA_Q), lambda p: (p, qcol)),
            meta(kcol), meta(vcol),
            kv_prev(kcol), kv_prev(vcol),
            kv_run(kcol), kv_run(vcol),
        ],
        out_specs=pl.BlockSpec((SWA_RUN * CHUNK, D_SWA_Q), lambda p: (p, 0)),
        out_shape=jax.ShapeDtypeStruct((N_PROMPT_ROWS, D_SWA_Q), BF16),
        compiler_params=pltpu.CompilerParams(dimension_semantics=("parallel",)),
        name="swa_prompt",
    )(sinks, h_p, h_m, h_m, h_p, h_p, h_p, h_p)


def _attn_sample(sinks, h_s, meta_k, meta_v, win_k, win_v):
    kcol = COL_K // D_SWA_KV
    vcol = COL_V // D_SWA_KV
    qcol = COL_Q // D_SWA_Q
    n = SWA_SAMPLE_SEQS
    cached = lambda rows_per_seq: pl.BlockSpec((n * rows_per_seq, D_SWA_KV), lambda s: (s, 0))
    own = lambda colblk: pl.BlockSpec((n * CHUNK, D_SWA_KV), lambda s: (s, colblk))
    return pl.pallas_call(
        _attn_sample_body,
        grid=(DEC_BATCH // n,),
        in_specs=[
            pl.BlockSpec(memory_space=pltpu.SMEM),
            pl.BlockSpec((n * CHUNK, D_SWA_Q), lambda s: (s, qcol)),
            cached(N_META), cached(N_META),
            cached(WINDOW), cached(WINDOW),
            own(kcol), own(vcol),
        ],
        out_specs=pl.BlockSpec((n * CHUNK, D_SWA_Q), lambda s: (s, 0)),
        out_shape=jax.ShapeDtypeStruct((N_SAMPLE_ROWS, D_SWA_Q), BF16),
        compiler_params=pltpu.CompilerParams(dimension_semantics=("parallel",)),
        name="swa_sample",
    )(sinks, h_s, meta_k, meta_v, win_k, win_v, h_s, h_s)


def _unit_lower_inverse(a, n):
    r = lax.broadcasted_iota(jnp.int32, (n, n), 0)
    c = lax.broadcasted_iota(jnp.int32, (n, n), 1)
    eye = (r == c).astype(F32)
    base = 3
    a0 = jnp.where((r >> base) == (c >> base), a, 0.0)
    a2 = _hdot(a0, a0)
    a4 = _hdot(a2, a2)
    x = _hdot(_hdot(eye - a0, eye + a2), eye + a4)
    shift = base
    while (1 << shift) < n:
        pair = jnp.logical_and((r >> (shift + 1)) == (c >> (shift + 1)), (r >> shift) != (c >> shift))
        ak = jnp.where(pair, a, 0.0)
        x = x - _hdot(x, _hdot(ak, x))
        shift += 1
    return x


def _gdn_body(x_ref, z_ref, t_ref, w_ref, alog_ref, dt_ref, nw_ref, s0_ref, b0_ref, *rest, C, n_chunks, n_cast):
    cast_src, (o_ref, sout_ref) = rest[:n_cast], rest[n_cast:n_cast + 2]
    cast_dst, (s_scr, xe_scr) = rest[n_cast + 2:2 * n_cast + 2], rest[2 * n_cast + 2:]
    c = pl.program_id(1)

    for src, dst in zip(cast_src, cast_dst):
        dst[...] = src[...].astype(BF16)

    @pl.when(c == 0)
    def _():
        s_scr[...] = s0_ref[0]
        xe_scr[0:CARRY, :] = b0_ref[0]

    @pl.when(c > 0)
    def _():
        xe_scr[0:CARRY, :] = xe_scr[C:C + CARRY, :]

    xe_scr[CARRY:CARRY + C, :] = x_ref[...]

    t = t_ref[...]
    beta = jax.nn.sigmoid(t)
    ta = t + dt_ref[...]
    softplus = jnp.maximum(ta, 0.0) + jnp.log(1.0 + jnp.exp(-jnp.abs(ta)))
    g = -jnp.exp(alog_ref[...]) * softplus
    r = lax.broadcasted_iota(jnp.int32, (C, C), 0)
    cc = lax.broadcasted_iota(jnp.int32, (C, C), 1)
    incl = r >= cc
    strict = r > cc
    gc = _dot(incl.astype(F32), g, HI)
    gc_t = gc.T
    heads = range(N_HEADS_GDN)
    bh = jnp.stack([beta[:, h:h + 1] for h in heads])
    gcol = jnp.stack([gc[:, N_HEADS_GDN + h:N_HEADS_GDN + h + 1] for h in heads])
    grow = jnp.stack([gc_t[N_HEADS_GDN + h:N_HEADS_GDN + h + 1, :] for h in heads])
    glast = gcol[:, C - 1:C, :]
    decay = jnp.where(incl, jnp.exp(jnp.where(incl, gcol - grow, 0.0)), 0.0)
    e_g = jnp.exp(gcol)

    xe = xe_scr[...]
    acc = w_ref[CONV_WIDTH - 1:CONV_WIDTH, :] * xe[CARRY:, :]
    for lag in range(1, CONV_WIDTH):
        tap = CONV_WIDTH - 1 - lag
        acc = acc + w_ref[tap:tap + 1, :] * pltpu.roll(xe, lag, 0)[CARRY:, :]
    y = _silu(acc)
    split = lambda base: jnp.stack([y[:, base + h * DK_GDN:base + (h + 1) * DK_GDN] for h in heads])
    q, k, v = split(0), split(D_GDN), split(2 * D_GDN)
    q = q * lax.rsqrt(jnp.sum(q * q, axis=-1, keepdims=True) + RMS_EPS) * (DK_GDN ** -0.5)
    k = k * lax.rsqrt(jnp.sum(k * k, axis=-1, keepdims=True) + RMS_EPS)
    kb = k * bh

    kq = _hdot_nt(jnp.concatenate([kb, q], axis=1), k)
    a = jnp.where(strict, kq[:, :C] * decay, 0.0)
    qk = kq[:, C:] * decay
    t_inv = _unit_lower_inverse(a, C)

    s_prev = s_scr[...]
    ws = _hdot(jnp.concatenate([kb * e_g, q * e_g], axis=1), s_prev)
    v_new = _hdot(t_inv, v * bh - ws[:, :C])
    o = ws[:, C:] + _hdot(qk, v_new)
    s_scr[...] = s_prev * jnp.exp(glast) + _hdot_tn(k * jnp.exp(glast - gcol), v_new)

    o = o * lax.rsqrt(jnp.mean(o * o, axis=-1, keepdims=True) + RMS_EPS) * nw_ref[...]
    for h in heads:
        hs = slice(h * DV_GDN, (h + 1) * DV_GDN)
        o_ref[:, hs] = (o[h] * _silu(z_ref[:, hs])).astype(BF16)

    @pl.when(c == n_chunks - 1)
    def _():
        sout_ref[0] = s_scr[...]


def _gdn(h_src, tail_src, conv_w, alog_row, dt_row, nw_row, s0, buf0, *, n_seq, n_chunks, C, shared_init,
         first_chunk=0, cast_jobs=()):
    init = (lambda s: 0) if shared_init else (lambda s: s)
    vec = pl.BlockSpec((1, TAIL_W), lambda s, c: (0, 0))
    src = lambda s, c: first_chunk + s * n_chunks + c
    in_specs = [
        pl.BlockSpec((C, D_CONV), lambda s, c: (src(s, c), COL_CONV // D_CONV)),
        pl.BlockSpec((C, D_GDN), lambda s, c: (src(s, c), COL_Z // D_GDN)),
        pl.BlockSpec((C, TAIL_W), lambda s, c: (src(s, c), 0)),
        pl.BlockSpec((CONV_WIDTH, D_CONV), lambda s, c: (0, 0)),
        vec, vec, vec,
        pl.BlockSpec((1, N_HEADS_GDN, DK_GDN, DV_GDN), lambda s, c: (init(s), 0, 0, 0)),
        pl.BlockSpec((1, CARRY, D_CONV), lambda s, c: (init(s), 0, 0)),
    ]
    out_specs = [
        pl.BlockSpec((C, D_GDN), lambda s, c: (s * n_chunks + c, 0)),
        pl.BlockSpec((1, N_HEADS_GDN, DK_GDN, DV_GDN), lambda s, c: (s, 0, 0, 0)),
    ]
    out_shape = [
        jax.ShapeDtypeStruct((n_seq * n_chunks * C, D_GDN), BF16),
        jax.ShapeDtypeStruct((n_seq, N_HEADS_GDN, DK_GDN, DV_GDN), F32),
    ]
    for w, prefix in cast_jobs:
        job_in, job_out, job_shape = _cast_job(w, prefix, n_seq * n_chunks, lambda s, c: s * n_chunks + c)
        in_specs.append(job_in)
        out_specs.append(job_out)
        out_shape.append(job_shape)
    return pl.pallas_call(
        functools.partial(_gdn_body, C=C, n_chunks=n_chunks, n_cast=len(cast_jobs)),
        grid=(n_seq, n_chunks),
        in_specs=in_specs,
        out_specs=out_specs,
        out_shape=out_shape,
        scratch_shapes=[
            pltpu.VMEM((N_HEADS_GDN, DK_GDN, DV_GDN), F32),
            pltpu.VMEM((CARRY + C, D_CONV), F32),
        ],
        compiler_params=pltpu.CompilerParams(
            dimension_semantics=("parallel", "arbitrary"), vmem_limit_bytes=VMEM_LIMIT),
        name="gdn",
    )(h_src, h_src, tail_src, conv_w, alog_row, dt_row, nw_row, s0, buf0, *[w for w, _ in cast_jobs])


def _out_body(a_ref, b_ref, w_ref, x_ref, g_ref, beta_ref, o_ref):
    o_ref[...] = ALPHA * x_ref[...] + _dot(a_ref[...], w_ref[0:D_SWA_Q, :]) + _dot(b_ref[...], w_ref[D_SWA_Q:, :])
    _layernorm_into(o_ref, lambda rows: o_ref[rows, :], g_ref, beta_ref)


def _out_proj(o_swa, o_gdn, w_out, x1, g, b, *, tm):
    rows = o_swa.shape[0]
    return pl.pallas_call(
        _out_body,
        grid=(rows // tm,),
        in_specs=[
            pl.BlockSpec((tm, D_SWA_Q), lambda i: (i, 0)),
            pl.BlockSpec((tm, D_GDN), lambda i: (i, 0)),
            pl.BlockSpec((D_MODEL, D_MODEL), lambda i: (0, 0), pipeline_mode=pl.Buffered(1)),
            pl.BlockSpec((tm, D_MODEL), lambda i: (i, 0)),
            pl.BlockSpec((1, D_MODEL), lambda i: (0, 0)),
            pl.BlockSpec((1, D_MODEL), lambda i: (0, 0)),
        ],
        out_specs=pl.BlockSpec((tm, D_MODEL), lambda i: (i, 0)),
        out_shape=jax.ShapeDtypeStruct((rows, D_MODEL), F32),
        compiler_params=pltpu.CompilerParams(
            dimension_semantics=("parallel",), vmem_limit_bytes=VMEM_LIMIT),
        name="proj_out",
    )(o_swa, o_gdn, w_out, x1, g, b)


def _rope_tables(pos):
    half = HEAD_DIM // 2
    inv = ROPE_THETA ** (-jnp.arange(half, dtype=F32) / half)
    ang = pos.astype(F32)[:, None] * inv[None, :]
    cos = jnp.cos(ang)
    sin = jnp.sin(ang)
    return jnp.concatenate([cos, cos], axis=1), jnp.concatenate([-sin, sin], axis=1)


def kernel(x_prompt, x_sample, cache_meta_k, cache_meta_v, cache_win_k, cache_win_v, state_conv, state_gdn,
           meta_tokens, ln_g, ln_b, ffn_w_gate, ffn_w_up, ffn_w_down, w_in, w_out, attn_sinks, conv_w,
           gdn_a_log, gdn_dt_bias, gdn_norm_w):
    l = 0
    wg1, wu1, wd1 = (w[l, 0].astype(BF16) for w in (ffn_w_gate, ffn_w_up, ffn_w_down))
    w_main = jnp.swapaxes(w_in[l], 0, 1)
    w_tail = jnp.pad(w_main[D_MAIN:], ((0, TAIL_W - 2 * N_HEADS_GDN), (0, 0))).astype(BF16)
    g1, g2, g3 = (ln_g[l, i][None, :] for i in range(3))
    b1, b2, b3 = (ln_b[l, i][None, :] for i in range(3))
    pad_tail = lambda v, off: jnp.pad(v.astype(F32), (off, TAIL_W - off - N_HEADS_GDN))[None, :]
    alog_row = pad_tail(gdn_a_log[l], N_HEADS_GDN)
    dt_row = pad_tail(gdn_dt_bias[l], N_HEADS_GDN)
    nw_row = gdn_norm_w[l].astype(F32)[None, :]
    cw = conv_w[l]
    sinks = attn_sinks[l].astype(F32)

    n_pad = N_SM_ROWS - N_SAMPLE_ROWS - N_META
    xp = x_prompt.reshape(N_PROMPT_ROWS, D_MODEL)
    xs = jnp.concatenate([x_sample.reshape(N_SAMPLE_ROWS, D_MODEL), meta_tokens.astype(F32),
                          jnp.zeros((n_pad, D_MODEL), F32)], axis=0)
    cos_p, sin_p = _rope_tables(N_META + jnp.arange(SEQ, dtype=jnp.int32))
    cos_s, sin_s = _rope_tables(jnp.concatenate([
        jnp.tile(N_META + PAST_LEN + jnp.arange(DEC_SEQ, dtype=jnp.int32), DEC_BATCH),
        jnp.arange(N_META, dtype=jnp.int32), jnp.zeros((n_pad,), jnp.int32)]))

    ffn1 = functools.partial(_ffn, wg=wg1, wu=wu1, wd=wd1, g=g1, b=b1, emit_bf16=True)
    x1_s, x1b_s = ffn1(xs, tm=SM_TILE)
    x1_p, x1b_p = ffn1(xp, tm=512)
    h_s, t_s, w_main_b = _proj(x1b_s, w_main, w_tail, cos_s, sin_s, tm=N_SM_ROWS, tn=PROJ_TN_F32,
                               emit_weights=True)
    h_p, t_p, wo = _proj(x1b_p, w_main_b, w_tail, cos_p, sin_p, tm=1024, tn=PROJ_TN_BF16, emit_weights=False,
                         cast_jobs=((w_out, (l,)),))
    meta_rows = slice(META_ROW0, META_ROW0 + N_META)

    o_swa_p = _attn_prompt(sinks, h_p, h_s)
    o_swa_s = _attn_sample(
        sinks, h_s,
        cache_meta_k[l].reshape(DEC_BATCH * N_META, D_SWA_KV), cache_meta_v[l].reshape(DEC_BATCH * N_META, D_SWA_KV),
        cache_win_k[l].reshape(DEC_BATCH * WINDOW, D_SWA_KV), cache_win_v[l].reshape(DEC_BATCH * WINDOW, D_SWA_KV))

    zero_s = jnp.zeros((1, N_HEADS_GDN, DK_GDN, DV_GDN), F32)
    zero_buf = jnp.zeros((1, CARRY, D_CONV), F32)
    gdn = functools.partial(_gdn, conv_w=cw, alog_row=alog_row, dt_row=dt_row, nw_row=nw_row)
    _, s_meta = gdn(h_s, t_s, s0=zero_s, buf0=zero_buf, n_seq=1, n_chunks=1, C=N_META, shared_init=True,
                    first_chunk=META_ROW0 // N_META)[:2]
    buf_meta = h_s[None, META_ROW0 + N_META - CARRY:META_ROW0 + N_META, COL_CONV:COL_CONV + D_CONV]
    second = (l, 1)
    o_gdn_p, s_prompt, wg2, wu2, wd2 = gdn(
        h_p, t_p, s0=s_meta, buf0=buf_meta, n_seq=BATCH, n_chunks=CHUNKS_PER_SEQ, C=CHUNK, shared_init=True,
        cast_jobs=((ffn_w_gate, second), (ffn_w_up, second), (ffn_w_down, second)))
    buf_s = jnp.pad(state_conv[l].astype(F32), ((0, 0), (CARRY - (CONV_WIDTH - 1), 0), (0, 0)))
    o_gdn_s, s_sample = gdn(h_s, t_s, s0=state_gdn[l].astype(F32), buf0=buf_s, n_seq=DEC_BATCH, n_chunks=1,
                            C=CHUNK, shared_init=False)[:2]

    x2_p = _out_proj(o_swa_p, o_gdn_p, wo, x1_p, g2, b2, tm=256)
    x2_s = _out_proj(o_swa_s, o_gdn_s, wo, x1_s, g2, b2, tm=256)
    ffn2 = functools.partial(_ffn, wg=wg2, wu=wu2, wd=wd2, g=g3, b=b3, emit_bf16=False)
    (y_p,) = ffn2(x2_p, tm=512)
    (y_s,) = ffn2(x2_s, tm=512)

    y_prompt = y_p.reshape(BATCH, SEQ, D_MODEL)
    y_sample = y_s.reshape(DEC_BATCH, DEC_SEQ, D_MODEL)
    k_meta = h_s[meta_rows, COL_K:COL_K + D_SWA_KV].reshape(N_META, N_KV_SWA, HEAD_DIM)
    v_meta = h_s[meta_rows, COL_V:COL_V + D_SWA_KV].reshape(N_META, N_KV_SWA, HEAD_DIM)
    p_meta_k = jnp.broadcast_to(k_meta[None, None], (1, BATCH, N_META, N_KV_SWA, HEAD_DIM))
    p_meta_v = jnp.broadcast_to(v_meta[None, None], (1, BATCH, N_META, N_KV_SWA, HEAD_DIM))
    hp = h_p.reshape(BATCH, SEQ, D_MAIN)
    sample_cols = lambda c0, width: h_s[:N_SAMPLE_ROWS, c0:c0 + width].reshape(DEC_BATCH, DEC_SEQ, width)
    p_win_k = hp[:, SEQ - WINDOW:, COL_K:COL_K + D_SWA_KV].reshape(1, BATCH, WINDOW, N_KV_SWA, HEAD_DIM)
    p_win_v = hp[:, SEQ - WINDOW:, COL_V:COL_V + D_SWA_KV].reshape(1, BATCH, WINDOW, N_KV_SWA, HEAD_DIM)
    p_conv = hp[:, SEQ - (CONV_WIDTH - 1):, COL_CONV:COL_CONV + D_CONV][None]
    p_gdn = s_prompt[None]
    s_win_k = sample_cols(COL_K, D_SWA_KV).reshape(1, DEC_BATCH, DEC_SEQ, N_KV_SWA, HEAD_DIM)
    s_win_v = sample_cols(COL_V, D_SWA_KV).reshape(1, DEC_BATCH, DEC_SEQ, N_KV_SWA, HEAD_DIM)
    s_conv = sample_cols(COL_CONV, D_CONV)[:, DEC_SEQ - (CONV_WIDTH - 1):][None]
    s_gdn = s_sample[None]
    return (y_prompt, y_sample, p_meta_k, p_meta_v, p_win_k, p_win_v, p_conv, p_gdn, s_win_k, s_win_v, s_conv,
            s_gdn)
```

```python
import functools

import jax
import jax.numpy as jnp
from jax import lax
from jax.experimental import pallas as pl
from jax.experimental.pallas import tpu as pltpu

D_MODEL = 4096
BATCH = 4
SEQ = 2048
DEC_BATCH = 16
DEC_SEQ = 64
PAST_LEN = 2048
CHUNK = 64
N_META = 16
WINDOW = 128
HEAD_DIM = 128
N_HEADS_SWA = 16
N_KV_SWA = 4
GQA_GROUP = 4
N_HEADS_GDN = 16
DK_GDN = 128
DV_GDN = 128
CONV_WIDTH = 4
D_SWA_Q = N_HEADS_SWA * HEAD_DIM
D_SWA_KV = N_KV_SWA * HEAD_DIM
D_GDN = N_HEADS_GDN * DK_GDN
D_CONV = 3 * D_GDN
D_FF = 11008
ROPE_THETA = 10000.0
LN_EPS = 1e-5
RMS_EPS = 1e-6
ALPHA = 2.0 ** 0.25

N_PROMPT_ROWS = BATCH * SEQ
N_SAMPLE_ROWS = DEC_BATCH * DEC_SEQ
CHUNKS_PER_SEQ = SEQ // CHUNK
META_ROW0 = N_SAMPLE_ROWS
SM_TILE = 528
N_SM_ROWS = 2 * SM_TILE

COL_CONV = 0
COL_Z = D_CONV
COL_Q = COL_Z + D_GDN
COL_K = COL_Q + D_SWA_Q
COL_V = COL_K + D_SWA_KV
D_MAIN = COL_V + D_SWA_KV
TAIL_W = 128

FFN_TF = 256
LN_ROWS_CHOICES = (64, 48, 16, 8)
LN_UNROLL = 2
PROJ_TN_F32 = 512
PROJ_TN_BF16 = 1024
CAST_STRIP_ROWS = (16, 32, 64, 128, 256)
CARRY = 8

V7X_VMEM_BYTES = 64 * 1024 * 1024
VMEM_LIMIT = V7X_VMEM_BYTES - 4 * 1024 * 1024

BF16 = jnp.bfloat16
F32 = jnp.float32
HI = lax.Precision.HIGHEST


def _dot(a, b, precision=None):
    return jnp.dot(a, b, preferred_element_type=F32, precision=precision)


def _dot_nt(a, b):
    return lax.dot_general(a, b, (((1,), (1,)), ((), ())), preferred_element_type=F32)


def _head_dot(a, b, lhs_contract, rhs_contract):
    dims = (((lhs_contract,), (rhs_contract,)), ((0,), (0,)))
    return lax.dot_general(a.astype(BF16), b.astype(BF16), dims, preferred_element_type=F32)


_hdot = functools.partial(_head_dot, lhs_contract=2, rhs_contract=1)
_hdot_nt = functools.partial(_head_dot, lhs_contract=2, rhs_contract=2)
_hdot_tn = functools.partial(_head_dot, lhs_contract=1, rhs_contract=1)


def _layernorm_rows(y, g, b):
    mu = jnp.mean(y, axis=-1, keepdims=True)
    d = y - mu
    var = jnp.mean(d * d, axis=-1, keepdims=True)
    return d * lax.rsqrt(var + LN_EPS) * g + b


def _layernorm_into(o_ref, pre_norm_rows, g_ref, b_ref):
    n_rows = o_ref.shape[0]
    per_chunk = next(r for r in LN_ROWS_CHOICES if n_rows % r == 0)

    def chunk(c, carry):
        rows = pl.ds(pl.multiple_of(c * per_chunk, per_chunk), per_chunk)
        o_ref[rows, :] = _layernorm_rows(pre_norm_rows(rows), g_ref[...], b_ref[...])
        return carry

    lax.fori_loop(0, n_rows // per_chunk, chunk, 0, unroll=LN_UNROLL)


def _silu(x):
    half = 0.5 * x
    return half + half * jnp.tanh(half)


def _cast_job(w, prefix, n_steps, step):
    r, c = w.shape[len(prefix):]
    strip = next(s for s in CAST_STRIP_ROWS if r % s == 0 and r // s <= n_steps)
    last = r // strip - 1
    strip_of = lambda *ids: jnp.minimum(step(*ids), last)
    in_spec = pl.BlockSpec((None,) * len(prefix) + (strip, c), lambda *ids: prefix + (strip_of(*ids), 0))
    out_spec = pl.BlockSpec((strip, c), lambda *ids: (strip_of(*ids), 0))
    return in_spec, out_spec, jax.ShapeDtypeStruct((r, c), BF16)


def _ffn_body(x_ref, wg_ref, wu_ref, wd_ref, g_ref, b_ref, o_ref, *rest, emit_bf16):
    if emit_bf16:
        ob_ref, xb_ref = rest
    else:
        (xb_ref,) = rest
    f = pl.program_id(1)

    @pl.when(f == 0)
    def _():
        xb_ref[...] = x_ref[...].astype(BF16)
        o_ref[...] = jnp.zeros_like(o_ref)

    xb = xb_ref[...]
    hidden = _silu(_dot(xb, wg_ref[...])) * _dot(xb, wu_ref[...])
    o_ref[...] += _dot(hidden.astype(BF16), wd_ref[...])

    @pl.when(f == pl.num_programs(1) - 1)
    def _():
        _layernorm_into(o_ref, lambda rows: ALPHA * x_ref[rows, :] + 0.5 * o_ref[rows, :], g_ref, b_ref)
        if emit_bf16:
            ob_ref[...] = o_ref[...].astype(BF16)


def _ffn(x, wg, wu, wd, g, b, *, tm, emit_bf16):
    rows = x.shape[0]
    row_spec = lambda: pl.BlockSpec((tm, D_MODEL), lambda i, f: (i, 0))
    out_shape = [jax.ShapeDtypeStruct((rows, D_MODEL), F32)]
    out_specs = [row_spec()]
    if emit_bf16:
        out_shape.append(jax.ShapeDtypeStruct((rows, D_MODEL), BF16))
        out_specs.append(row_spec())
    return pl.pallas_call(
        functools.partial(_ffn_body, emit_bf16=emit_bf16),
        grid=(rows // tm, D_FF // FFN_TF),
        in_specs=[
            row_spec(),
            pl.BlockSpec((D_MODEL, FFN_TF), lambda i, f: (0, f)),
            pl.BlockSpec((D_MODEL, FFN_TF), lambda i, f: (0, f)),
            pl.BlockSpec((FFN_TF, D_MODEL), lambda i, f: (f, 0)),
            pl.BlockSpec((1, D_MODEL), lambda i, f: (0, 0)),
            pl.BlockSpec((1, D_MODEL), lambda i, f: (0, 0)),
        ],
        out_specs=out_specs,
        out_shape=out_shape,
        scratch_shapes=[pltpu.VMEM((tm, D_MODEL), BF16)],
        compiler_params=pltpu.CompilerParams(
            dimension_semantics=("parallel", "arbitrary"), vmem_limit_bytes=VMEM_LIMIT),
        name="ffn",
    )(x, wg, wu, wd, g, b)


def _proj_body(x_ref, w_ref, wt_ref, cos_ref, sin_ref, *rest, tn, emit_weights, n_cast):
    cast_src, rest = rest[:n_cast], rest[n_cast:]
    o_ref, t_ref = rest[:2]
    j = pl.program_id(1)
    for src, dst in zip(cast_src, rest[len(rest) - n_cast:]):
        dst[...] = src[...].astype(BF16)
    wb = w_ref[...].astype(BF16)
    if emit_weights:
        rest[2][...] = wb
    acc = _dot_nt(x_ref[...], wb)
    n_full, n_part = divmod((D_SWA_Q + D_SWA_KV) // HEAD_DIM, tn // HEAD_DIM)

    def store(n_rope_heads):
        cos = cos_ref[...]
        sin = sin_ref[...]
        for s in range(tn // HEAD_DIM):
            cols = slice(s * HEAD_DIM, (s + 1) * HEAD_DIM)
            blk = acc[:, cols]
            o_ref[:, cols] = blk * cos + pltpu.roll(blk, HEAD_DIM // 2, 1) * sin if s < n_rope_heads else blk

    pl.when(j < n_full)(lambda: store(tn // HEAD_DIM))
    pl.when(j == n_full)(lambda: store(n_part))

    @pl.when(j > n_full)
    def _():
        o_ref[...] = acc

    @pl.when(j == 0)
    def _():
        t_ref[...] = _dot_nt(x_ref[...], wt_ref[...])


def _proj(xb, w_t, w_tail, cos, sin, *, tm, tn, emit_weights, cast_jobs=()):
    rows = xb.shape[0]
    grid = (rows // tm, D_MAIN // tn)
    n_swa = (D_SWA_Q + 2 * D_SWA_KV) // tn

    def dest(i, j):
        return i, jnp.where(j < n_swa, j + COL_Q // tn, j - n_swa)

    in_specs = [
        pl.BlockSpec((tm, D_MODEL), lambda i, j: (i, 0)),
        pl.BlockSpec((tn, D_MODEL), lambda i, j: (j, 0)),
        pl.BlockSpec((TAIL_W, D_MODEL), lambda i, j: (0, 0)),
        pl.BlockSpec((tm, HEAD_DIM), lambda i, j: (i, 0)),
        pl.BlockSpec((tm, HEAD_DIM), lambda i, j: (i, 0)),
    ]
    args = [xb, w_t, w_tail, cos, sin]
    out_specs = [pl.BlockSpec((tm, tn), dest), pl.BlockSpec((tm, TAIL_W), lambda i, j: (i, 0))]
    out_shape = [jax.ShapeDtypeStruct((rows, D_MAIN), F32), jax.ShapeDtypeStruct((rows, TAIL_W), F32)]
    if emit_weights:
        assert grid[0] == 1
        out_specs.append(pl.BlockSpec((tn, D_MODEL), lambda i, j: (j, 0)))
        out_shape.append(jax.ShapeDtypeStruct((D_MAIN, D_MODEL), BF16))
    for w, prefix in cast_jobs:
        job_in, job_out, job_shape = _cast_job(w, prefix, grid[0] * grid[1], lambda i, j: i * grid[1] + j)
        in_specs.append(job_in)
        args.append(w)
        out_specs.append(job_out)
        out_shape.append(job_shape)
    return pl.pallas_call(
        functools.partial(_proj_body, tn=tn, emit_weights=emit_weights, n_cast=len(cast_jobs)),
        grid=grid,
        in_specs=in_specs,
        out_specs=out_specs,
        out_shape=out_shape,
        compiler_params=pltpu.CompilerParams(
            dimension_semantics=("parallel", "arbitrary"), vmem_limit_bytes=VMEM_LIMIT),
        name="proj_in",
    )(*args)


N_KEYS = N_META + 3 * CHUNK
SWA_RUN = 8
SWA_SAMPLE_SEQS = 4


def _stack_heads(ref, rows, heads):
    return jnp.concatenate([ref[rows, h * HEAD_DIM:(h + 1) * HEAD_DIM] for h in heads], axis=0)


def _group_heads(g):
    return range(g * GQA_GROUP, (g + 1) * GQA_GROUP)


def _sink_rows(sink_ref, g):
    return jnp.concatenate([jnp.full((CHUNK, 1), sink_ref[h], F32) for h in _group_heads(g)], axis=0)


def _softmax_pv(q, k, v, sink, mask):
    s = _hdot_nt(q, k) * (HEAD_DIM ** -0.5)
    if mask is not None:
        s = jnp.where(mask, s, -jnp.inf)
    m = jnp.maximum(jnp.max(s, axis=-1, keepdims=True), sink)
    p = jnp.exp(s - m)
    den = jnp.sum(p, axis=-1, keepdims=True) + jnp.exp(sink - m)
    return _hdot(p, v) / den


def _attn_sample_body(sink_ref, q_ref, km_ref, vm_ref, kw_ref, vw_ref, k0_ref, v0_ref, o_ref):
    groups = range(N_KV_SWA)
    q, k, v, sink = [], [], [], []
    for i in range(SWA_SAMPLE_SEQS):
        rows = lambda n: slice(i * n, (i + 1) * n)
        for g in groups:
            k.append(jnp.concatenate([_stack_heads(km_ref, rows(N_META), [g]), _stack_heads(kw_ref, rows(WINDOW), [g]),
                                      _stack_heads(k0_ref, rows(CHUNK), [g])], axis=0))
            v.append(jnp.concatenate([_stack_heads(vm_ref, rows(N_META), [g]), _stack_heads(vw_ref, rows(WINDOW), [g]),
                                      _stack_heads(v0_ref, rows(CHUNK), [g])], axis=0))
            q.append(_stack_heads(q_ref, rows(CHUNK), _group_heads(g)))
            sink.append(_sink_rows(sink_ref, g))
    o = _softmax_pv(jnp.stack(q), jnp.stack(k), jnp.stack(v), jnp.stack(sink), None)
    for i in range(SWA_SAMPLE_SEQS):
        for g in groups:
            for j, h in enumerate(_group_heads(g)):
                o_ref[i * CHUNK:(i + 1) * CHUNK, h * HEAD_DIM:(h + 1) * HEAD_DIM] = (
                    o[i * N_KV_SWA + g, j * CHUNK:(j + 1) * CHUNK].astype(BF16))


def _attn_run_body(sink_ref, q_ref, km_ref, vm_ref, kp_ref, vp_ref, kc_ref, vc_ref, o_ref):
    has_prev = pl.program_id(0) % (CHUNKS_PER_SEQ // SWA_RUN) >= 1
    col = lax.broadcasted_iota(jnp.int32, (1, N_KEYS), 1)
    groups = range(N_KV_SWA)
    q, k, v, sink, mask = [], [], [], [], []
    for sub in range(SWA_RUN):
        n_prev = max(2 - sub, 0)
        prev_rows = slice((2 - n_prev) * CHUNK, 2 * CHUNK)
        cur_rows = slice(max(sub - 2, 0) * CHUNK, (sub + 1) * CHUNK)
        visible = jnp.logical_or(jnp.logical_or(col < N_META, col >= N_META + n_prev * CHUNK), has_prev)
        for g in groups:
            k_parts = [_stack_heads(km_ref, slice(None), [g]), _stack_heads(kc_ref, cur_rows, [g])]
            v_parts = [_stack_heads(vm_ref, slice(None), [g]), _stack_heads(vc_ref, cur_rows, [g])]
            if n_prev:
                k_parts.insert(1, _stack_heads(kp_ref, prev_rows, [g]))
                v_parts.insert(1, _stack_heads(vp_ref, prev_rows, [g]))
            k.append(jnp.concatenate(k_parts, axis=0))
            v.append(jnp.concatenate(v_parts, axis=0))
            q.append(_stack_heads(q_ref, slice(sub * CHUNK, (sub + 1) * CHUNK), _group_heads(g)))
            sink.append(_sink_rows(sink_ref, g))
            mask.append(visible)
    o = _softmax_pv(jnp.stack(q), jnp.stack(k), jnp.stack(v), jnp.stack(sink), jnp.stack(mask))
    for sub in range(SWA_RUN):
        for g in groups:
            for j, h in enumerate(_group_heads(g)):
                o_ref[sub * CHUNK:(sub + 1) * CHUNK, h * HEAD_DIM:(h + 1) * HEAD_DIM] = (
                    o[sub * N_KV_SWA + g, j * CHUNK:(j + 1) * CHUNK].astype(BF16))


def _attn_prompt(sinks, h_p, h_m):
    kcol = COL_K // D_SWA_KV
    vcol = COL_V // D_SWA_KV
    qcol = COL_Q // D_SWA_Q
    runs_per_seq = CHUNKS_PER_SEQ // SWA_RUN
    pairs_per_run = SWA_RUN // 2
    prev = lambda colblk: (lambda p: (p * pairs_per_run - jnp.where(p % runs_per_seq >= 1, 1, 0), colblk))
    kv_prev = lambda colblk: pl.BlockSpec((2 * CHUNK, D_SWA_KV), prev(colblk))
    kv_run = lambda colblk: pl.BlockSpec((SWA_RUN * CHUNK, D_SWA_KV), lambda p: (p, colblk))
    meta = lambda colblk: pl.BlockSpec((N_META, D_SWA_KV), lambda p: (META_ROW0 // N_META, colblk))
    return pl.pallas_call(
        _attn_run_body,
        grid=(N_PROMPT_ROWS // (SWA_RUN * CHUNK),),
        in_specs=[
            pl.BlockSpec(memory_space=pltpu.SMEM),
            pl.BlockSpec((SWA_RUN * CHUNK, D_SWA_Q), lambda p: (p, qcol)),
            meta(kcol), meta(vcol),
            kv_prev(kcol), kv_prev(vcol),
            kv_run(kcol), kv_run(vcol),
        ],
        out_specs=pl.BlockSpec((SWA_RUN * CHUNK, D_SWA_Q), lambda p: (p, 0)),
        out_shape=jax.ShapeDtypeStruct((N_PROMPT_ROWS, D_SWA_Q), BF16),
        compiler_params=pltpu.CompilerParams(dimension_semantics=("parallel",)),
        name="swa_prompt",
    )(sinks, h_p, h_m, h_m, h_p, h_p, h_p, h_p)


def _attn_sample(sinks, h_s, meta_k, meta_v, win_k, win_v):
    kcol = COL_K // D_SWA_KV
    vcol = COL_V // D_SWA_KV
    qcol = COL_Q // D_SWA_Q
    n = SWA_SAMPLE_SEQS
    cached = lambda rows_per_seq: pl.BlockSpec((n * rows_per_seq, D_SWA_KV), lambda s: (s, 0))
    own = lambda colblk: pl.BlockSpec((n * CHUNK, D_SWA_KV), lambda s: (s, colblk))
    return pl.pallas_call(
        _attn_sample_body,
        grid=(DEC_BATCH // n,),
        in_specs=[
            pl.BlockSpec(memory_space=pltpu.SMEM),
            pl.BlockSpec((n * CHUNK, D_SWA_Q), lambda s: (s, qcol)),
            cached(N_META), cached(N_META),
            cached(WINDOW), cached(WINDOW),
            own(kcol), own(vcol),
        ],
        out_specs=pl.BlockSpec((n * CHUNK, D_SWA_Q), lambda s: (s, 0)),
        out_shape=jax.ShapeDtypeStruct((N_SAMPLE_ROWS, D_SWA_Q), BF16),
        compiler_params=pltpu.CompilerParams(dimension_semantics=("parallel",)),
        name="swa_sample",
    )(sinks, h_s, meta_k, meta_v, win_k, win_v, h_s, h_s)


def _unit_lower_inverse(a, n):
    r = lax.broadcasted_iota(jnp.int32, (n, n), 0)
    c = lax.broadcasted_iota(jnp.int32, (n, n), 1)
    eye = (r == c).astype(F32)
    base = 3
    a0 = jnp.where((r >> base) == (c >> base), a, 0.0)
    a2 = _hdot(a0, a0)
    a4 = _hdot(a2, a2)
    x = _hdot(_hdot(eye - a0, eye + a2), eye + a4)
    shift = base
    while (1 << shift) < n:
        pair = jnp.logical_and((r >> (shift + 1)) == (c >> (shift + 1)), (r >> shift) != (c >> shift))
        ak = jnp.where(pair, a, 0.0)
        x = x - _hdot(x, _hdot(ak, x))
        shift += 1
    return x


def _gdn_body(x_ref, z_ref, t_ref, w_ref, alog_ref, dt_ref, nw_ref, s0_ref, b0_ref, *rest, C, n_chunks, n_cast):
    cast_src, (o_ref, sout_ref) = rest[:n_cast], rest[n_cast:n_cast + 2]
    cast_dst, (s_scr, xe_scr) = rest[n_cast + 2:2 * n_cast + 2], rest[2 * n_cast + 2:]
    c = pl.program_id(1)

    for src, dst in zip(cast_src, cast_dst):
        dst[...] = src[...].astype(BF16)

    @pl.when(c == 0)
    def _():
        s_scr[...] = s0_ref[0]
        xe_scr[0:CARRY, :] = b0_ref[0]

    @pl.when(c > 0)
    def _():
        xe_scr[0:CARRY, :] = xe_scr[C:C + CARRY, :]

    xe_scr[CARRY:CARRY + C, :] = x_ref[...]

    t = t_ref[...]
    beta = jax.nn.sigmoid(t)
    ta = t + dt_ref[...]
    softplus = jnp.maximum(ta, 0.0) + jnp.log(1.0 + jnp.exp(-jnp.abs(ta)))
    g = -jnp.exp(alog_ref[...]) * softplus
    r = lax.broadcasted_iota(jnp.int32, (C, C), 0)
    cc = lax.broadcasted_iota(jnp.int32, (C, C), 1)
    incl = r >= cc
    strict = r > cc
    gc = _dot(incl.astype(F32), g, HI)
    gc_t = gc.T
    heads = range(N_HEADS_GDN)
    bh = jnp.stack([beta[:, h:h + 1] for h in heads])
    gcol = jnp.stack([gc[:, N_HEADS_GDN + h:N_HEADS_GDN + h + 1] for h in heads])
    grow = jnp.stack([gc_t[N_HEADS_GDN + h:N_HEADS_GDN + h + 1, :] for h in heads])
    glast = gcol[:, C - 1:C, :]
    decay = jnp.where(incl, jnp.exp(jnp.where(incl, gcol - grow, 0.0)), 0.0)
    e_g = jnp.exp(gcol)

    xe = xe_scr[...]
    acc = w_ref[CONV_WIDTH - 1:CONV_WIDTH, :] * xe[CARRY:, :]
    for lag in range(1, CONV_WIDTH):
        tap = CONV_WIDTH - 1 - lag
        acc = acc + w_ref[tap:tap + 1, :] * pltpu.roll(xe, lag, 0)[CARRY:, :]
    y = _silu(acc)
    split = lambda base: jnp.stack([y[:, base + h * DK_GDN:base + (h + 1) * DK_GDN] for h in heads])
    q, k, v = split(0), split(D_GDN), split(2 * D_GDN)
    q = q * lax.rsqrt(jnp.sum(q * q, axis=-1, keepdims=True) + RMS_EPS) * (DK_GDN ** -0.5)
    k = k * lax.rsqrt(jnp.sum(k * k, axis=-1, keepdims=True) + RMS_EPS)
    kb = k * bh

    kq = _hdot_nt(jnp.concatenate([kb, q], axis=1), k)
    a = jnp.where(strict, kq[:, :C] * decay, 0.0)
    qk = kq[:, C:] * decay
    t_inv = _unit_lower_inverse(a, C)

    s_prev = s_scr[...]
    ws = _hdot(jnp.concatenate([kb * e_g, q * e_g], axis=1), s_prev)
    v_new = _hdot(t_inv, v * bh - ws[:, :C])
    o = ws[:, C:] + _hdot(qk, v_new)
    s_scr[...] = s_prev * jnp.exp(glast) + _hdot_tn(k * jnp.exp(glast - gcol), v_new)

    o = o * lax.rsqrt(jnp.mean(o * o, axis=-1, keepdims=True) + RMS_EPS) * nw_ref[...]
    for h in heads:
        hs = slice(h * DV_GDN, (h + 1) * DV_GDN)
        o_ref[:, hs] = (o[h] * _silu(z_ref[:, hs])).astype(BF16)

    @pl.when(c == n_chunks - 1)
    def _():
        sout_ref[0] = s_scr[...]


def _gdn(h_src, tail_src, conv_w, alog_row, dt_row, nw_row, s0, buf0, *, n_seq, n_chunks, C, shared_init,
         first_chunk=0, cast_jobs=()):
    init = (lambda s: 0) if shared_init else (lambda s: s)
    vec = pl.BlockSpec((1, TAIL_W), lambda s, c: (0, 0))
    src = lambda s, c: first_chunk + s * n_chunks + c
    in_specs = [
        pl.BlockSpec((C, D_CONV), lambda s, c: (src(s, c), COL_CONV // D_CONV)),
        pl.BlockSpec((C, D_GDN), lambda s, c: (src(s, c), COL_Z // D_GDN)),
        pl.BlockSpec((C, TAIL_W), lambda s, c: (src(s, c), 0)),
        pl.BlockSpec((CONV_WIDTH, D_CONV), lambda s, c: (0, 0)),
        vec, vec, vec,
        pl.BlockSpec((1, N_HEADS_GDN, DK_GDN, DV_GDN), lambda s, c: (init(s), 0, 0, 0)),
        pl.BlockSpec((1, CARRY, D_CONV), lambda s, c: (init(s), 0, 0)),
    ]
    out_specs = [
        pl.BlockSpec((C, D_GDN), lambda s, c: (s * n_chunks + c, 0)),
        pl.BlockSpec((1, N_HEADS_GDN, DK_GDN, DV_GDN), lambda s, c: (s, 0, 0, 0)),
    ]
    out_shape = [
        jax.ShapeDtypeStruct((n_seq * n_chunks * C, D_GDN), BF16),
        jax.ShapeDtypeStruct((n_seq, N_HEADS_GDN, DK_GDN, DV_GDN), F32),
    ]
    for w, prefix in cast_jobs:
        job_in, job_out, job_shape = _cast_job(w, prefix, n_seq * n_chunks, lambda s, c: s * n_chunks + c)
        in_specs.append(job_in)
        out_specs.append(job_out)
        out_shape.append(job_shape)
    return pl.pallas_call(
        functools.partial(_gdn_body, C=C, n_chunks=n_chunks, n_cast=len(cast_jobs)),
        grid=(n_seq, n_chunks),
        in_specs=in_specs,
        out_specs=out_specs,
        out_shape=out_shape,
        scratch_shapes=[
            pltpu.VMEM((N_HEADS_GDN, DK_GDN, DV_GDN), F32),
            pltpu.VMEM((CARRY + C, D_CONV), F32),
        ],
        compiler_params=pltpu.CompilerParams(
            dimension_semantics=("parallel", "arbitrary"), vmem_limit_bytes=VMEM_LIMIT),
        name="gdn",
    )(h_src, h_src, tail_src, conv_w, alog_row, dt_row, nw_row, s0, buf0, *[w for w, _ in cast_jobs])


def _out_body(a_ref, b_ref, w_ref, x_ref, g_ref, beta_ref, o_ref):
    o_ref[...] = ALPHA * x_ref[...] + _dot(a_ref[...], w_ref[0:D_SWA_Q, :]) + _dot(b_ref[...], w_ref[D_SWA_Q:, :])
    _layernorm_into(o_ref, lambda rows: o_ref[rows, :], g_ref, beta_ref)


def _out_proj(o_swa, o_gdn, w_out, x1, g, b, *, tm):
    rows = o_swa.shape[0]
    return pl.pallas_call(
        _out_body,
        grid=(rows // tm,),
        in_specs=[
            pl.BlockSpec((tm, D_SWA_Q), lambda i: (i, 0)),
            pl.BlockSpec((tm, D_GDN), lambda i: (i, 0)),
            pl.BlockSpec((D_MODEL, D_MODEL), lambda i: (0, 0), pipeline_mode=pl.Buffered(1)),
            pl.BlockSpec((tm, D_MODEL), lambda i: (i, 0)),
            pl.BlockSpec((1, D_MODEL), lambda i: (0, 0)),
            pl.BlockSpec((1, D_MODEL), lambda i: (0, 0)),
        ],
        out_specs=pl.BlockSpec((tm, D_MODEL), lambda i: (i, 0)),
        out_shape=jax.ShapeDtypeStruct((rows, D_MODEL), F32),
        compiler_params=pltpu.CompilerParams(
            dimension_semantics=("parallel",), vmem_limit_bytes=VMEM_LIMIT),
        name="proj_out",
    )(o_swa, o_gdn, w_out, x1, g, b)


def _rope_tables(pos):
    half = HEAD_DIM // 2
    inv = ROPE_THETA ** (-jnp.arange(half, dtype=F32) / half)
    ang = pos.astype(F32)[:, None] * inv[None, :]
    cos = jnp.cos(ang)
    sin = jnp.sin(ang)
    return jnp.concatenate([cos, cos], axis=1), jnp.concatenate([-sin, sin], axis=1)


def kernel(x_prompt, x_sample, cache_meta_k, cache_meta_v, cache_win_k, cache_win_v, state_conv, state_gdn,
           meta_tokens, ln_g, ln_b, ffn_w_gate, ffn_w_up, ffn_w_down, w_in, w_out, attn_sinks, conv_w,
           gdn_a_log, gdn_dt_bias, gdn_norm_w):
    l = 0
    wg1, wu1, wd1 = (w[l, 0].astype(BF16) for w in (ffn_w_gate, ffn_w_up, ffn_w_down))
    w_main = jnp.swapaxes(w_in[l], 0, 1)
    w_tail = jnp.pad(w_main[D_MAIN:], ((0, TAIL_W - 2 * N_HEADS_GDN), (0, 0))).astype(BF16)
    g1, g2, g3 = (ln_g[l, i][None, :] for i in range(3))
    b1, b2, b3 = (ln_b[l, i][None, :] for i in range(3))
    pad_tail = lambda v, off: jnp.pad(v.astype(F32), (off, TAIL_W - off - N_HEADS_GDN))[None, :]
    alog_row = pad_tail(gdn_a_log[l], N_HEADS_GDN)
    dt_row = pad_tail(gdn_dt_bias[l], N_HEADS_GDN)
    nw_row = gdn_norm_w[l].astype(F32)[None, :]
    cw = conv_w[l]
    sinks = attn_sinks[l].astype(F32)

    n_pad = N_SM_ROWS - N_SAMPLE_ROWS - N_META
    xp = x_prompt.reshape(N_PROMPT_ROWS, D_MODEL)
    xs = jnp.concatenate([x_sample.reshape(N_SAMPLE_ROWS, D_MODEL), meta_tokens.astype(F32),
                          jnp.zeros((n_pad, D_MODEL), F32)], axis=0)
    cos_p, sin_p = _rope_tables(jnp.tile(N_META + jnp.arange(SEQ, dtype=jnp.int32), BATCH))
    cos_s, sin_s = _rope_tables(jnp.concatenate([
        jnp.tile(N_META + PAST_LEN + jnp.arange(DEC_SEQ, dtype=jnp.int32), DEC_BATCH),
        jnp.arange(N_META, dtype=jnp.int32), jnp.zeros((n_pad,), jnp.int32)]))

    ffn1 = functools.partial(_ffn, wg=wg1, wu=wu1, wd=wd1, g=g1, b=b1, emit_bf16=True)
    x1_s, x1b_s = ffn1(xs, tm=SM_TILE)
    x1_p, x1b_p = ffn1(xp, tm=512)
    h_s, t_s, w_main_b = _proj(x1b_s, w_main, w_tail, cos_s, sin_s, tm=N_SM_ROWS, tn=PROJ_TN_F32,
                               emit_weights=True)
    h_p, t_p, wo = _proj(x1b_p, w_main_b, w_tail, cos_p, sin_p, tm=1024, tn=PROJ_TN_BF16, emit_weights=False,
                         cast_jobs=((w_out, (l,)),))
    meta_rows = slice(META_ROW0, META_ROW0 + N_META)

    o_swa_p = _attn_prompt(sinks, h_p, h_s)
    o_swa_s = _attn_sample(
        sinks, h_s,
        cache_meta_k[l].reshape(DEC_BATCH * N_META, D_SWA_KV), cache_meta_v[l].reshape(DEC_BATCH * N_META, D_SWA_KV),
        cache_win_k[l].reshape(DEC_BATCH * WINDOW, D_SWA_KV), cache_win_v[l].reshape(DEC_BATCH * WINDOW, D_SWA_KV))

    zero_s = jnp.zeros((1, N_HEADS_GDN, DK_GDN, DV_GDN), F32)
    zero_buf = jnp.zeros((1, CARRY, D_CONV), F32)
    gdn = functools.partial(_gdn, conv_w=cw, alog_row=alog_row, dt_row=dt_row, nw_row=nw_row)
    _, s_meta = gdn(h_s, t_s, s0=zero_s, buf0=zero_buf, n_seq=1, n_chunks=1, C=N_META, shared_init=True,
                    first_chunk=META_ROW0 // N_META)[:2]
    buf_meta = h_s[None, META_ROW0 + N_META - CARRY:META_ROW0 + N_META, COL_CONV:COL_CONV + D_CONV]
    second = (l, 1)
    o_gdn_p, s_prompt, wg2, wu2, wd2 = gdn(
        h_p, t_p, s0=s_meta, buf0=buf_meta, n_seq=BATCH, n_chunks=CHUNKS_PER_SEQ, C=CHUNK, shared_init=True,
        cast_jobs=((ffn_w_gate, second), (ffn_w_up, second), (ffn_w_down, second)))
    buf_s = jnp.pad(state_conv[l].astype(F32), ((0, 0), (CARRY - (CONV_WIDTH - 1), 0), (0, 0)))
    o_gdn_s, s_sample = gdn(h_s, t_s, s0=state_gdn[l].astype(F32), buf0=buf_s, n_seq=DEC_BATCH, n_chunks=1,
                            C=CHUNK, shared_init=False)[:2]

    x2_p = _out_proj(o_swa_p, o_gdn_p, wo, x1_p, g2, b2, tm=256)
    x2_s = _out_proj(o_swa_s, o_gdn_s, wo, x1_s, g2, b2, tm=256)
    ffn2 = functools.partial(_ffn, wg=wg2, wu=wu2, wd=wd2, g=g3, b=b3, emit_bf16=False)
    (y_p,) = ffn2(x2_p, tm=512)
    (y_s,) = ffn2(x2_s, tm=512)

    y_prompt = y_p.reshape(BATCH, SEQ, D_MODEL)
    y_sample = y_s.reshape(DEC_BATCH, DEC_SEQ, D_MODEL)
    k_meta = h_s[meta_rows, COL_K:COL_K + D_SWA_KV].reshape(N_META, N_KV_SWA, HEAD_DIM)
    v_meta = h_s[meta_rows, COL_V:COL_V + D_SWA_KV].reshape(N_META, N_KV_SWA, HEAD_DIM)
    p_meta_k = jnp.broadcast_to(k_meta[None, None], (1, BATCH, N_META, N_KV_SWA, HEAD_DIM))
    p_meta_v = jnp.broadcast_to(v_meta[None, None], (1, BATCH, N_META, N_KV_SWA, HEAD_DIM))
    hp = h_p.reshape(BATCH, SEQ, D_MAIN)
    sample_cols = lambda c0, width: h_s[:N_SAMPLE_ROWS, c0:c0 + width].reshape(DEC_BATCH, DEC_SEQ, width)
    p_win_k = hp[:, SEQ - WINDOW:, COL_K:COL_K + D_SWA_KV].reshape(1, BATCH, WINDOW, N_KV_SWA, HEAD_DIM)
    p_win_v = hp[:, SEQ - WINDOW:, COL_V:COL_V + D_SWA_KV].reshape(1, BATCH, WINDOW, N_KV_SWA, HEAD_DIM)
    p_conv = hp[:, SEQ - (CONV_WIDTH - 1):, COL_CONV:COL_CONV + D_CONV][None]
    p_gdn = s_prompt[None]
    s_win_k = sample_cols(COL_K, D_SWA_KV).reshape(1, DEC_BATCH, DEC_SEQ, N_KV_SWA, HEAD_DIM)
    s_win_v = sample_cols(COL_V, D_SWA_KV).reshape(1, DEC_BATCH, DEC_SEQ, N_KV_SWA, HEAD_DIM)
    s_conv = sample_cols(COL_CONV, D_CONV)[:, DEC_SEQ - (CONV_WIDTH - 1):][None]
    s_gdn = s_sample[None]
    return (y_prompt, y_sample, p_meta_k, p_meta_v, p_win_k, p_win_v, p_conv, p_gdn, s_win_k, s_win_v, s_conv,
            s_gdn)
```
